```python
import math
import jax, jax.numpy as jnp
from jax import lax
import numpy as np

D_MODEL = 2048
BATCH = 2
SEQ = 16384
DEPTH = 1
DEC_BATCH = 1
DEC_SEQ = 8192
PAST_LEN = 128

D_HYENA = D_MODEL // 2
HYENA_ORDER = 2
N_DIR = 2
FILTER_EMB = 33
FILTER_HID = 64
DECAY_FAST_PCT = 0.3
DECAY_SLOW_PCT = 1.5
DECAY_TARGET = 1e-2
D_SGU = D_MODEL // 2
CHUNK = 128
SGU_GROUPS = 8
SGU_GROUP_DIM = D_SGU // SGU_GROUPS
N_EXPERTS = 16
EC_CAPACITY = 2
D_EXPERT = D_MODEL
NORM_EPS = 1e-6
LN_EPS = 1e-5
OFF_HY = 0
OFF_SG = 3 * D_HYENA
OFF_GH = OFF_SG + 2 * D_SGU
OFF_GS = OFF_GH + D_MODEL
D_IN = OFF_GS + D_MODEL

kernel_name = "hyena_sgu_ec_moe_bidir_encoder"


def rmsnorm(x, g):
    xf = x.astype(jnp.float32)
    y = xf * lax.rsqrt(jnp.mean(xf * xf, axis=-1, keepdims=True) + NORM_EPS) * g.astype(jnp.float32)
    return y.astype(x.dtype)


def positional_features(L):
    t = jnp.linspace(0.0, 1.0, L, dtype=jnp.float32)[:, None]
    bands = (FILTER_EMB - 1) // 2
    w = (2.0 * math.pi / L) * jnp.arange(L, dtype=jnp.float32)[:, None]
    f = jnp.linspace(1e-4, bands - 1, bands, dtype=jnp.float32)[None, :]
    z = f * w
    return jnp.concatenate([t, jnp.cos(z), -jnp.sin(z)], axis=-1), t


def hyena_filter_spectra(L, w1, b1, w2, b2, w3, b3, w4, freq):
    f32 = jnp.float32
    feat, t = positional_features(L)
    fr = freq.astype(f32)
    h = jnp.sin(fr * (feat @ w1.astype(f32) + b1.astype(f32)))
    h = jnp.sin(fr * (h @ w2.astype(f32) + b2.astype(f32)))
    h = jnp.sin(fr * (h @ w3.astype(f32) + b3.astype(f32)))
    k = (h @ w4.astype(f32)).reshape(L, HYENA_ORDER, N_DIR, D_HYENA)
    deltas = jnp.abs(jnp.linspace(math.log(DECAY_TARGET) / DECAY_SLOW_PCT,
                                  math.log(DECAY_TARGET) / DECAY_FAST_PCT, D_HYENA, dtype=f32))
    k = k * jnp.exp(-t * deltas)[:, None, None, :]
    fwd = k[:, :, 0]
    bwd = k[:, :, 1]
    kern = jnp.concatenate([fwd, jnp.zeros((1, HYENA_ORDER, D_HYENA), f32), bwd[:0:-1]], axis=0)
    kern = kern / jnp.sum(jnp.abs(kern), axis=0, keepdims=True)
    return jnp.fft.rfft(kern, axis=0)


def fftconv(z, kf, d):
    L = z.shape[1]
    Z = jnp.fft.rfft(z, n=2 * L, axis=1)
    y = jnp.fft.irfft(Z * kf[None], n=2 * L, axis=1)[:, :L]
    return y + z * d


def short_conv(p, w, b):
    pp = jnp.pad(p, ((0, 0), (1, 1), (0, 0)))
    return pp[:, :-2] * w[0] + pp[:, 1:-1] * w[1] + pp[:, 2:] * w[2] + b


def hyena_branch(p3, conv_w, conv_b, kf, d):
    u = short_conv(p3, conv_w, conv_b).astype(jnp.float32)
    x1, x2, v = jnp.split(u, 3, axis=-1)
    df = d.astype(jnp.float32)
    z = v
    for o, gate in enumerate((x1, x2)):
        z = gate * fftconv(z, kf[:, o], df[o])
    return z


def sgu_branch(p2, ln_g, ln_b, w_s, b_s):
    a = jax.nn.gelu(p2, approximate=False)
    u, v = jnp.split(a, 2, axis=-1)
    vf = v.astype(jnp.float32)
    mu = jnp.mean(vf, axis=-1, keepdims=True)
    var = jnp.mean(jnp.square(vf - mu), axis=-1, keepdims=True)
    vn = (vf - mu) * lax.rsqrt(var + LN_EPS) * ln_g.astype(jnp.float32) + ln_b.astype(jnp.float32)
    B, L, _ = v.shape
    vc = vn.reshape(B, L // CHUNK, CHUNK, SGU_GROUPS, SGU_GROUP_DIM)
    s = jnp.einsum('gpq,bnqgc->bnpgc', w_s.astype(jnp.float32), vc)
    s = s + b_s.astype(jnp.float32).T[None, None, :, :, None]
    return u * s.reshape(B, L, D_SGU).astype(u.dtype)


def ec_moe(x, w_router, w_gate, w_up, w_down):
    B, L, D = x.shape
    n = B * L
    cap = EC_CAPACITY * n // N_EXPERTS
    xt = x.reshape(n, D)
    probs = jax.nn.softmax((xt @ w_router).astype(jnp.float32), axis=-1)
    gates, idx = lax.top_k(probs.T, cap)
    xe = jnp.take(xt, idx, axis=0)
    h = jax.nn.silu(jnp.einsum('ecd,edf->ecf', xe, w_gate)) * jnp.einsum('ecd,edf->ecf', xe, w_up)
    ye = jnp.einsum('ecf,efd->ecd', h, w_down) * gates[..., None].astype(xt.dtype)
    out = jnp.zeros_like(xt).at[idx.reshape(-1)].add(ye.reshape(-1, D))
    return out.reshape(B, L, D)


def encoder_layer(x, norm_mix_g, w_in, hy_conv_w, hy_conv_b, flt_w1, flt_b1, flt_w2, flt_b2,
                  flt_w3, flt_b3, flt_w4, flt_sin_freq, hy_d, sg_ln_g, sg_ln_b, sg_w, sg_b,
                  w_branch_hy, w_branch_sg, w_out, norm_moe_g, w_router, w_gate, w_up, w_down):
    L = x.shape[1]
    p = rmsnorm(x, norm_mix_g) @ w_in
    p_hy = p[..., OFF_HY:OFF_SG]
    p_sg = p[..., OFF_SG:OFF_GH]
    g_hy = p[..., OFF_GH:OFF_GS]
    g_sg = p[..., OFF_GS:D_IN]
    kf = hyena_filter_spectra(L, flt_w1, flt_b1, flt_w2, flt_b2, flt_w3, flt_b3, flt_w4, flt_sin_freq)
    y_hy = hyena_branch(p_hy, hy_conv_w, hy_conv_b, kf, hy_d).astype(x.dtype)
    y_sg = sgu_branch(p_sg, sg_ln_g, sg_ln_b, sg_w, sg_b)
    merged = jax.nn.sigmoid(g_hy) * (y_hy @ w_branch_hy) + jax.nn.sigmoid(g_sg) * (y_sg @ w_branch_sg)
    x = x + merged @ w_out
    x = x + ec_moe(rmsnorm(x, norm_moe_g), w_router, w_gate, w_up, w_down)
    return x


def setup_inputs(seed: int = 0) -> dict:
    key = jax.random.key(seed)
    ks = jax.random.split(key, 32)
    f32 = jnp.float32
    nrm = lambda k, shape, s: (jax.random.normal(k, shape, f32) * s).astype(f32)
    D, Lr = D_MODEL, DEPTH
    return {
        "x_prompt": nrm(ks[0], (BATCH, SEQ, D), 1.0),
        "x_sample": nrm(ks[1], (DEC_BATCH, DEC_SEQ, D), 1.0),
        "norm_mix_g": 1.0 + nrm(ks[2], (Lr, D), 0.02),
        "w_in": nrm(ks[3], (Lr, D, D_IN), D ** -0.5),
        "hy_conv_w": nrm(ks[4], (Lr, 3, 3 * D_HYENA), 0.5),
        "hy_conv_b": nrm(ks[5], (Lr, 3 * D_HYENA), 0.02),
        "flt_w1": nrm(ks[6], (Lr, FILTER_EMB, FILTER_HID), FILTER_EMB ** -0.5),
        "flt_b1": nrm(ks[7], (Lr, FILTER_HID), 0.5),
        "flt_w2": nrm(ks[8], (Lr, FILTER_HID, FILTER_HID), FILTER_HID ** -0.5),
        "flt_b2": nrm(ks[9], (Lr, FILTER_HID), 0.5),
        "flt_w3": nrm(ks[10], (Lr, FILTER_HID, FILTER_HID), FILTER_HID ** -0.5),
        "flt_b3": nrm(ks[11], (Lr, FILTER_HID), 0.5),
        "flt_w4": nrm(ks[12], (Lr, FILTER_HID, HYENA_ORDER * N_DIR * D_HYENA), FILTER_HID ** -0.5),
        "flt_sin_freq": 1.0 + nrm(ks[13], (Lr, FILTER_HID), 0.1),
        "hy_d": nrm(ks[14], (Lr, HYENA_ORDER, D_HYENA), 0.5),
        "sg_ln_g": 1.0 + nrm(ks[15], (Lr, D_SGU), 0.02),
        "sg_ln_b": nrm(ks[16], (Lr, D_SGU), 0.02),
        "sg_w": nrm(ks[17], (Lr, SGU_GROUPS, CHUNK, CHUNK), CHUNK ** -0.5),
        "sg_b": 1.0 + nrm(ks[18], (Lr, SGU_GROUPS, CHUNK), 0.1),
        "w_branch_hy": nrm(ks[19], (Lr, D_HYENA, D), D_HYENA ** -0.5),
        "w_branch_sg": nrm(ks[20], (Lr, D_SGU, D), D_SGU ** -0.5),
        "w_out": nrm(ks[21], (Lr, D, D), D ** -0.5),
        "norm_moe_g": 1.0 + nrm(ks[22], (Lr, D), 0.02),
        "w_router": nrm(ks[23], (Lr, D, N_EXPERTS), D ** -0.5),
        "w_gate": nrm(ks[24], (Lr, N_EXPERTS, D, D_EXPERT), D ** -0.5),
        "w_up": nrm(ks[25], (Lr, N_EXPERTS, D, D_EXPERT), D ** -0.5),
        "w_down": nrm(ks[26], (Lr, N_EXPERTS, D_EXPERT, D), D_EXPERT ** -0.5),
        "norm_final_g": 1.0 + nrm(ks[27], (D,), 0.02),
    }


def reference(x_prompt, x_sample, norm_mix_g, w_in, hy_conv_w, hy_conv_b, flt_w1, flt_b1, flt_w2,
              flt_b2, flt_w3, flt_b3, flt_w4, flt_sin_freq, hy_d, sg_ln_g, sg_ln_b, sg_w, sg_b,
              w_branch_hy, w_branch_sg, w_out, norm_moe_g, w_router, w_gate, w_up, w_down,
              norm_final_g):
    layer_weights = (norm_mix_g, w_in, hy_conv_w, hy_conv_b, flt_w1, flt_b1, flt_w2, flt_b2,
                     flt_w3, flt_b3, flt_w4, flt_sin_freq, hy_d, sg_ln_g, sg_ln_b, sg_w, sg_b,
                     w_branch_hy, w_branch_sg, w_out, norm_moe_g, w_router, w_gate, w_up, w_down)

    def trunk(x):
        for layer in range(DEPTH):
            x = encoder_layer(x, *[w[layer] for w in layer_weights])
        return rmsnorm(x, norm_final_g)

    y_prompt = trunk(x_prompt)
    y_sample = trunk(x_sample)
    return (y_prompt, y_sample)
```

```python
import functools
import math

import jax
import jax.numpy as jnp
from jax import lax
from jax.experimental import pallas as pl
from jax.experimental.pallas import tpu as pltpu

f32 = jnp.float32
bf16 = jnp.bfloat16

D_MODEL = 2048
D_HYENA = D_MODEL // 2
HYENA_ORDER = 2
N_DIR = 2
FILTER_EMB = 33
DECAY_FAST_PCT = 0.3
DECAY_SLOW_PCT = 1.5
DECAY_TARGET = 1e-2
D_SGU = D_MODEL // 2
CHUNK = 128
SGU_GROUPS = 8
SGU_GROUP_DIM = D_SGU // SGU_GROUPS
N_EXPERTS = 16
EC_CAPACITY = 2
NORM_EPS = 1e-6
LN_EPS = 1e-5
OFF_SG = 3 * D_HYENA
OFF_GH = OFF_SG + 2 * D_SGU
OFF_GS = OFF_GH + D_MODEL
D_IN = OFF_GS + D_MODEL

VMEM_LIMIT_BYTES = 56 * 1024 * 1024


def _params(*sem):
    return pltpu.CompilerParams(dimension_semantics=sem, vmem_limit_bytes=VMEM_LIMIT_BYTES)


def _inproj_kernel(x_ref, g_ref, w_ref, o_ref, xn_ref):
    @pl.when(pl.program_id(1) == 0)
    def _():
        x = x_ref[...]
        ms = jnp.mean(x * x, axis=-1, keepdims=True)
        xn_ref[...] = (x * lax.rsqrt(ms + NORM_EPS) * g_ref[...]).astype(bf16)

    o_ref[...] = jnp.dot(xn_ref[...], w_ref[...], preferred_element_type=f32).astype(o_ref.dtype)


def in_proj(x, g, w, *, tm=1024, tn=1024):
    n, d = x.shape
    dn = w.shape[1]
    return pl.pallas_call(
        _inproj_kernel,
        grid=(n // tm, dn // tn),
        in_specs=[
            pl.BlockSpec((tm, d), lambda i, j: (i, 0)),
            pl.BlockSpec((1, d), lambda i, j: (0, 0)),
            pl.BlockSpec((d, tn), lambda i, j: (0, j)),
        ],
        out_specs=pl.BlockSpec((tm, tn), lambda i, j: (i, j)),
        out_shape=jax.ShapeDtypeStruct((n, dn), bf16),
        scratch_shapes=[pltpu.VMEM((tm, d), bf16)],
        compiler_params=_params("parallel", "arbitrary"),
        name="in_proj",
    )(x, g, w)


def _merge_kernel(yh_ref, ys_ref, wbh_ref, wbs_ref, gh_ref, gs_ref, o_ref):
    a = jnp.dot(yh_ref[...], wbh_ref[...], preferred_element_type=f32)
    b = jnp.dot(ys_ref[...], wbs_ref[...], preferred_element_type=f32)
    o = jax.nn.sigmoid(gh_ref[...].astype(f32)) * a + jax.nn.sigmoid(gs_ref[...].astype(f32)) * b
    o_ref[...] = o.astype(o_ref.dtype)


def branch_merge(y_hy, y_sg, w_bh, w_bs, p, *, tm=1024, tn=1024):
    n, dh = y_hy.shape
    d = w_bh.shape[1]
    gh0 = OFF_GH // tn
    gs0 = OFF_GS // tn
    return pl.pallas_call(
        _merge_kernel,
        grid=(n // tm, d // tn),
        in_specs=[
            pl.BlockSpec((tm, dh), lambda i, j: (i, 0)),
            pl.BlockSpec((tm, dh), lambda i, j: (i, 0)),
            pl.BlockSpec((dh, tn), lambda i, j: (0, j)),
            pl.BlockSpec((dh, tn), lambda i, j: (0, j)),
            pl.BlockSpec((tm, tn), lambda i, j: (i, gh0 + j)),
            pl.BlockSpec((tm, tn), lambda i, j: (i, gs0 + j)),
        ],
        out_specs=pl.BlockSpec((tm, tn), lambda i, j: (i, j)),
        out_shape=jax.ShapeDtypeStruct((n, d), bf16),
        compiler_params=_params("parallel", "arbitrary"),
        name="branch_merge",
    )(y_hy, y_sg, w_bh, w_bs, p, p)


def _outproj_kernel(m_ref, w_ref, x_ref, o_ref):
    o_ref[...] = x_ref[...] + jnp.dot(m_ref[...], w_ref[...], preferred_element_type=f32)


def out_proj(merged, w, x, *, tm=1024, tn=1024):
    n, d = merged.shape
    dn = w.shape[1]
    return pl.pallas_call(
        _outproj_kernel,
        grid=(n // tm, dn // tn),
        in_specs=[
            pl.BlockSpec((tm, d), lambda i, j: (i, 0)),
            pl.BlockSpec((d, tn), lambda i, j: (0, j)),
            pl.BlockSpec((tm, tn), lambda i, j: (i, j)),
        ],
        out_specs=pl.BlockSpec((tm, tn), lambda i, j: (i, j)),
        out_shape=jax.ShapeDtypeStruct((n, dn), f32),
        compiler_params=_params("parallel", "arbitrary"),
        name="out_proj",
    )(merged, w, x)


def _expert_kernel(xe_ref, wg_ref, wu_ref, wd_ref, o_ref, acc_ref):
    f = pl.program_id(2)
    xe = xe_ref[0]
    hg = jnp.dot(xe, wg_ref[0], preferred_element_type=f32)
    hu = jnp.dot(xe, wu_ref[0], preferred_element_type=f32)
    h = (hg * jax.nn.sigmoid(hg) * hu).astype(bf16)
    part = jnp.dot(h, wd_ref[0], preferred_element_type=f32)

    @pl.when(f == 0)
    def _():
        acc_ref[...] = part

    @pl.when(f != 0)
    def _():
        acc_ref[...] += part

    @pl.when(f == pl.num_programs(2) - 1)
    def _():
        o_ref[0] = acc_ref[...].astype(o_ref.dtype)


def expert_ffn(xe, wg, wu, wd, *, tm=1024, tf=512):
    e, c, d = xe.shape
    dff = wg.shape[2]
    return pl.pallas_call(
        _expert_kernel,
        grid=(e, c // tm, dff // tf),
        in_specs=[
            pl.BlockSpec((1, tm, d), lambda e_, i, f: (e_, i, 0)),
            pl.BlockSpec((1, d, tf), lambda e_, i, f: (e_, 0, f)),
            pl.BlockSpec((1, d, tf), lambda e_, i, f: (e_, 0, f)),
            pl.BlockSpec((1, tf, d), lambda e_, i, f: (e_, f, 0)),
        ],
        out_specs=pl.BlockSpec((1, tm, d), lambda e_, i, f: (e_, i, 0)),
        out_shape=jax.ShapeDtypeStruct((e, c, d), bf16),
        scratch_shapes=[pltpu.VMEM((tm, d), f32)],
        compiler_params=_params("parallel", "parallel", "arbitrary"),
        name="expert_ffn",
    )(xe, wg, wu, wd)


def _rmsnorm(x, g):
    xf = x.astype(f32)
    return xf * lax.rsqrt(jnp.mean(xf * xf, axis=-1, keepdims=True) + NORM_EPS) * g.astype(f32)


def _positional_features(L):
    t = jnp.linspace(0.0, 1.0, L, dtype=f32)[:, None]
    bands = (FILTER_EMB - 1) // 2
    w = (2.0 * math.pi / L) * jnp.arange(L, dtype=f32)[:, None]
    f = jnp.linspace(1e-4, bands - 1, bands, dtype=f32)[None, :]
    z = f * w
    return jnp.concatenate([t, jnp.cos(z), -jnp.sin(z)], axis=-1), t


def _filter_spectra(L, w1, b1, w2, b2, w3, b3, w4, freq):
    hp = lax.Precision.HIGHEST
    feat, t = _positional_features(L)
    h = jnp.sin(freq * (jnp.dot(feat, w1, precision=hp) + b1))
    h = jnp.sin(freq * (jnp.dot(h, w2, precision=hp) + b2))
    h = jnp.sin(freq * (jnp.dot(h, w3, precision=hp) + b3))
    k = jnp.dot(h, w4, precision=hp).reshape(L, HYENA_ORDER, N_DIR, D_HYENA)
    deltas = jnp.abs(jnp.linspace(math.log(DECAY_TARGET) / DECAY_SLOW_PCT,
                                  math.log(DECAY_TARGET) / DECAY_FAST_PCT, D_HYENA, dtype=f32))
    k = k * jnp.exp(-t * deltas)[:, None, None, :]
    fwd = k[:, :, 0]
    bwd = k[:, :, 1]
    kern = jnp.concatenate([fwd, jnp.zeros((1, HYENA_ORDER, D_HYENA), f32), bwd[:0:-1]], axis=0)
    kern = kern / jnp.sum(jnp.abs(kern), axis=0, keepdims=True)
    return jnp.fft.rfft(kern, axis=0)


def _fftconv(z, kf, d):
    L = z.shape[1]
    Z = jnp.fft.rfft(z, n=2 * L, axis=1)
    y = jnp.fft.irfft(Z * kf[None], n=2 * L, axis=1)[:, :L]
    return y + z * d


def _short_conv(p, w, b):
    pp = jnp.pad(p, ((0, 0), (1, 1), (0, 0)))
    return pp[:, :-2] * w[0] + pp[:, 1:-1] * w[1] + pp[:, 2:] * w[2] + b


def _hyena_branch(p3, conv_w, conv_b, kf, d):
    u = _short_conv(p3, conv_w, conv_b)
    x1, x2, v = jnp.split(u, 3, axis=-1)
    z = v
    for o, gate in enumerate((x1, x2)):
        z = gate * _fftconv(z, kf[:, o], d[o])
    return z


def _sgu_branch(p2, ln_g, ln_b, w_s, b_s):
    a = jax.nn.gelu(p2, approximate=False)
    u, v = jnp.split(a, 2, axis=-1)
    mu = jnp.mean(v, axis=-1, keepdims=True)
    var = jnp.mean(jnp.square(v - mu), axis=-1, keepdims=True)
    vn = (v - mu) * lax.rsqrt(var + LN_EPS) * ln_g + ln_b
    B, L, _ = v.shape
    vc = vn.reshape(B, L // CHUNK, CHUNK, SGU_GROUPS, SGU_GROUP_DIM)
    s = jnp.einsum('gpq,bnqgc->bnpgc', w_s, vc, precision=lax.Precision.HIGHEST)
    s = s + b_s.T[None, None, :, :, None]
    return u * s.reshape(B, L, D_SGU)


def _route(xn, w_router):
    n = xn.shape[0]
    cap = EC_CAPACITY * n // N_EXPERTS
    logits = jnp.dot(xn, w_router, precision=lax.Precision.HIGHEST)
    probs = jax.nn.softmax(logits, axis=-1)
    gates, idx = lax.top_k(probs.T, cap)
    return gates, idx


def kernel(x_prompt, x_sample, norm_mix_g, w_in, hy_conv_w, hy_conv_b, flt_w1, flt_b1, flt_w2, flt_b2,
           flt_w3, flt_b3, flt_w4, flt_sin_freq, hy_d, sg_ln_g, sg_ln_b, sg_w, sg_b, w_branch_hy,
           w_branch_sg, w_out, norm_moe_g, w_router, w_gate, w_up, w_down, norm_final_g):
    w_in_b = w_in[0].astype(bf16)
    w_bh_b = w_branch_hy[0].astype(bf16)
    w_bs_b = w_branch_sg[0].astype(bf16)
    w_out_b = w_out[0].astype(bf16)
    wg_b = w_gate[0].astype(bf16)
    wu_b = w_up[0].astype(bf16)
    wd_b = w_down[0].astype(bf16)
    g_mix = norm_mix_g[0].reshape(1, D_MODEL)

    def mixer(x):
        B, L, D = x.shape
        n = B * L
        xt = x.reshape(n, D)
        p = in_proj(xt, g_mix, w_in_b)
        p_hy = p[:, :OFF_SG].astype(f32).reshape(B, L, OFF_SG)
        p_sg = p[:, OFF_SG:OFF_GH].astype(f32).reshape(B, L, 2 * D_SGU)
        kf = _filter_spectra(L, flt_w1[0], flt_b1[0], flt_w2[0], flt_b2[0], flt_w3[0], flt_b3[0],
                             flt_w4[0], flt_sin_freq[0])
        y_hy = _hyena_branch(p_hy, hy_conv_w[0], hy_conv_b[0], kf, hy_d[0]).reshape(n, D_HYENA)
        y_sg = _sgu_branch(p_sg, sg_ln_g[0], sg_ln_b[0], sg_w[0], sg_b[0]).reshape(n, D_SGU)
        merged = branch_merge(y_hy.astype(bf16), y_sg.astype(bf16), w_bh_b, w_bs_b, p)
        return out_proj(merged, w_out_b, xt)

    xm_p = mixer(x_prompt)
    xm_s = mixer(x_sample)

    groups = []
    for xm in (xm_p, xm_s):
        xn = _rmsnorm(xm, norm_moe_g[0])
        gates, idx = _route(xn, w_router[0])
        xe = jnp.take(xn.astype(bf16), idx, axis=0)
        groups.append((xm, gates, idx, xe))

    xe_all = jnp.concatenate([g[3] for g in groups], axis=1)
    ye_all = expert_ffn(xe_all, wg_b, wu_b, wd_b)

    outs = []
    off = 0
    for (xm, gates, idx, xe), xin in zip(groups, (x_prompt, x_sample)):
        cap = idx.shape[1]
        ye = ye_all[:, off:off + cap].astype(f32) * gates[..., None]
        off += cap
        moe = jnp.zeros_like(xm).at[idx.reshape(-1)].add(ye.reshape(-1, D_MODEL))
        y = _rmsnorm(xm + moe, norm_final_g)
        outs.append(y.reshape(xin.shape))
    return tuple(outs)
```

```python
import functools
import math

import jax
import jax.numpy as jnp
from jax import lax
from jax.experimental import pallas as pl
from jax.experimental.pallas import tpu as pltpu

f32 = jnp.float32
bf16 = jnp.bfloat16

D_MODEL = 2048
D_HYENA = D_MODEL // 2
HYENA_ORDER = 2
N_DIR = 2
FILTER_EMB = 33
DECAY_FAST_PCT = 0.3
DECAY_SLOW_PCT = 1.5
DECAY_TARGET = 1e-2
D_SGU = D_MODEL // 2
CHUNK = 128
SGU_GROUPS = 8
SGU_GROUP_DIM = D_SGU // SGU_GROUPS
N_EXPERTS = 16
EC_CAPACITY = 2
NORM_EPS = 1e-6
LN_EPS = 1e-5
OFF_SG = 3 * D_HYENA
OFF_GH = OFF_SG + 2 * D_SGU
OFF_GS = OFF_GH + D_MODEL
D_IN = OFF_GS + D_MODEL

VMEM_LIMIT_BYTES = 56 * 1024 * 1024


def _params(*sem):
    return pltpu.CompilerParams(dimension_semantics=sem, vmem_limit_bytes=VMEM_LIMIT_BYTES)


def _inproj_kernel(x_ref, g_ref, w_ref, o_ref, xn_ref):
    @pl.when(pl.program_id(1) == 0)
    def _():
        x = x_ref[...]
        ms = jnp.mean(x * x, axis=-1, keepdims=True)
        xn_ref[...] = (x * lax.rsqrt(ms + NORM_EPS) * g_ref[...]).astype(bf16)

    o_ref[...] = jnp.dot(xn_ref[...], w_ref[...], preferred_element_type=f32).astype(o_ref.dtype)


def in_proj(x, g, w, *, tm=1024, tn=1024):
    n, d = x.shape
    dn = w.shape[1]
    return pl.pallas_call(
        _inproj_kernel,
        grid=(n // tm, dn // tn),
        in_specs=[
            pl.BlockSpec((tm, d), lambda i, j: (i, 0)),
            pl.BlockSpec((1, d), lambda i, j: (0, 0)),
            pl.BlockSpec((d, tn), lambda i, j: (0, j)),
        ],
        out_specs=pl.BlockSpec((tm, tn), lambda i, j: (i, j)),
        out_shape=jax.ShapeDtypeStruct((n, dn), bf16),
        scratch_shapes=[pltpu.VMEM((tm, d), bf16)],
        compiler_params=_params("parallel", "arbitrary"),
        name="in_proj",
    )(x, g, w)


def _merge_kernel(yh_ref, ys_ref, wbh_ref, wbs_ref, gh_ref, gs_ref, o_ref):
    a = jnp.dot(yh_ref[...], wbh_ref[...], preferred_element_type=f32)
    b = jnp.dot(ys_ref[...], wbs_ref[...], preferred_element_type=f32)
    o = jax.nn.sigmoid(gh_ref[...].astype(f32)) * a + jax.nn.sigmoid(gs_ref[...].astype(f32)) * b
    o_ref[...] = o.astype(o_ref.dtype)


def branch_merge(y_hy, y_sg, w_bh, w_bs, p, *, tm=1024, tn=1024):
    n, dh = y_hy.shape
    d = w_bh.shape[1]
    gh0 = OFF_GH // tn
    gs0 = OFF_GS // tn
    return pl.pallas_call(
        _merge_kernel,
        grid=(n // tm, d // tn),
        in_specs=[
            pl.BlockSpec((tm, dh), lambda i, j: (i, 0)),
            pl.BlockSpec((tm, dh), lambda i, j: (i, 0)),
            pl.BlockSpec((dh, tn), lambda i, j: (0, j)),
            pl.BlockSpec((dh, tn), lambda i, j: (0, j)),
            pl.BlockSpec((tm, tn), lambda i, j: (i, gh0 + j)),
            pl.BlockSpec((tm, tn), lambda i, j: (i, gs0 + j)),
        ],
        out_specs=pl.BlockSpec((tm, tn), lambda i, j: (i, j)),
        out_shape=jax.ShapeDtypeStruct((n, d), bf16),
        compiler_params=_params("parallel", "arbitrary"),
        name="branch_merge",
    )(y_hy, y_sg, w_bh, w_bs, p, p)


def _outproj_kernel(m_ref, w_ref, x_ref, o_ref):
    o_ref[...] = x_ref[...] + jnp.dot(m_ref[...], w_ref[...], preferred_element_type=f32)


def out_proj(merged, w, x, *, tm=1024, tn=1024):
    n, d = merged.shape
    dn = w.shape[1]
    return pl.pallas_call(
        _outproj_kernel,
        grid=(n // tm, dn // tn),
        in_specs=[
            pl.BlockSpec((tm, d), lambda i, j: (i, 0)),
            pl.BlockSpec((d, tn), lambda i, j: (0, j)),
            pl.BlockSpec((tm, tn), lambda i, j: (i, j)),
        ],
        out_specs=pl.BlockSpec((tm, tn), lambda i, j: (i, j)),
        out_shape=jax.ShapeDtypeStruct((n, dn), f32),
        compiler_params=_params("parallel", "arbitrary"),
        name="out_proj",
    )(merged, w, x)


def _expert_kernel(xe_ref, wg_ref, wu_ref, wd_ref, o_ref, acc_ref):
    f = pl.program_id(2)
    xe = xe_ref[0]
    hg = jnp.dot(xe, wg_ref[0], preferred_element_type=f32)
    hu = jnp.dot(xe, wu_ref[0], preferred_element_type=f32)
    h = (hg * jax.nn.sigmoid(hg) * hu).astype(bf16)
    part = jnp.dot(h, wd_ref[0], preferred_element_type=f32)

    @pl.when(f == 0)
    def _():
        acc_ref[...] = part

    @pl.when(f != 0)
    def _():
        acc_ref[...] += part

    @pl.when(f == pl.num_programs(2) - 1)
    def _():
        o_ref[0] = acc_ref[...].astype(o_ref.dtype)


def expert_ffn(xe, wg, wu, wd, *, tm=1024, tf=512):
    e, c, d = xe.shape
    dff = wg.shape[2]
    return pl.pallas_call(
        _expert_kernel,
        grid=(e, c // tm, dff // tf),
        in_specs=[
            pl.BlockSpec((1, tm, d), lambda e_, i, f: (e_, i, 0)),
            pl.BlockSpec((1, d, tf), lambda e_, i, f: (e_, 0, f)),
            pl.BlockSpec((1, d, tf), lambda e_, i, f: (e_, 0, f)),
            pl.BlockSpec((1, tf, d), lambda e_, i, f: (e_, f, 0)),
        ],
        out_specs=pl.BlockSpec((1, tm, d), lambda e_, i, f: (e_, i, 0)),
        out_shape=jax.ShapeDtypeStruct((e, c, d), bf16),
        scratch_shapes=[pltpu.VMEM((tm, d), f32)],
        compiler_params=_params("parallel", "parallel", "arbitrary"),
        name="expert_ffn",
    )(xe, wg, wu, wd)


N2 = 128


def _angles_stage1(H):
    n_total = 2 * N2 * H
    n2 = jnp.arange(N2, dtype=jnp.int32)[:, None, None]
    k1 = jnp.arange(H, dtype=jnp.int32)[None, :, None]
    n1 = jnp.arange(H, dtype=jnp.int32)[None, None, :]
    m = ((2 * k1 + 1) * (N2 * n1 + n2)) % (2 * n_total)
    return m.astype(f32) * (math.pi / n_total)


def _stage1_tables(H):
    th = _angles_stage1(H)
    c, s = jnp.cos(th), jnp.sin(th)
    fwd = jnp.concatenate([c, -s], axis=1).astype(bf16)
    scale = 1.0 / (N2 * H)
    ct, st = jnp.swapaxes(c, 1, 2), jnp.swapaxes(s, 1, 2)
    inv = (jnp.concatenate([ct, -st], axis=2) * scale).astype(bf16)
    return fwd, inv


def _stage2_tables():
    j = jnp.arange(N2, dtype=jnp.int32)
    th = ((j[:, None] * j[None, :]) % N2).astype(f32) * (2.0 * math.pi / N2)
    c, s = jnp.cos(th), jnp.sin(th)
    fwd = jnp.concatenate([jnp.concatenate([c, s], 1), jnp.concatenate([-s, c], 1)], 0)
    inv = jnp.concatenate([jnp.concatenate([c, -s], 1), jnp.concatenate([s, c], 1)], 0)
    return fwd.astype(bf16), inv.astype(bf16)


def _shortconv_kernel(x_ref, w_ref, b_ref, o_ref, s_ref, *, rows):
    L = x_ref.shape[0]
    zeros8 = jnp.zeros((8, x_ref.shape[1]), f32)
    s_ref[0:8, :] = zeros8
    s_ref[L + 8:L + 16, :] = zeros8

    def load(i, c):
        r0 = pl.multiple_of(i * rows, rows)
        s_ref[pl.ds(r0 + 8, rows), :] = x_ref[pl.ds(r0, rows), :].astype(f32)
        return c

    lax.fori_loop(0, L // rows, load, 0)
    w0, w1, w2, b = w_ref[0:1, :], w_ref[1:2, :], w_ref[2:3, :], b_ref[...]

    def conv(i, c):
        r0 = pl.multiple_of(i * rows, rows)
        prev = s_ref[pl.ds(r0 + 7, rows), :]
        cur = s_ref[pl.ds(r0 + 8, rows), :]
        nxt = s_ref[pl.ds(r0 + 9, rows), :]
        o_ref[pl.ds(r0, rows), :] = (prev * w0 + cur * w1 + nxt * w2 + b).astype(o_ref.dtype)
        return c

    lax.fori_loop(0, L // rows, conv, 0)


def short_conv(p3, w, b, col0, ncols, out_dtype, *, rows=256):
    B, L, _ = p3.shape
    cb = col0 // 128
    return pl.pallas_call(
        functools.partial(_shortconv_kernel, rows=rows),
        grid=(B, ncols // 128),
        in_specs=[
            pl.BlockSpec((None, L, 128), lambda b_, j: (b_, 0, cb + j)),
            pl.BlockSpec((3, 128), lambda b_, j: (0, cb + j)),
            pl.BlockSpec((1, 128), lambda b_, j: (0, cb + j)),
        ],
        out_specs=pl.BlockSpec((None, L, 128), lambda b_, j: (b_, 0, j)),
        out_shape=jax.ShapeDtypeStruct((B, L, ncols), out_dtype),
        scratch_shapes=[pltpu.VMEM((L + 16, 128), f32)],
        compiler_params=_params("parallel", "parallel"),
        name="short_conv",
    )(p3, w, b)


def _dft1_kernel(tab_ref, z_ref, o_ref, *, n2b, K):
    g = pl.program_id(2)
    for q in range(n2b):
        zs = z_ref[pl.ds(g * n2b + q, K, stride=N2), :].astype(bf16)
        o_ref[q] = jnp.dot(tab_ref[q], zs, preferred_element_type=f32)


def dft_stage1(z, tab, *, n2b=16):
    B, Lz, C = z.shape
    _, M, K = tab.shape
    assert Lz == K * N2
    return pl.pallas_call(
        functools.partial(_dft1_kernel, n2b=n2b, K=K),
        grid=(B, C // 128, N2 // n2b),
        in_specs=[
            pl.BlockSpec((n2b, M, K), lambda b_, c, g: (g, 0, 0)),
            pl.BlockSpec((None, Lz, 128), lambda b_, c, g: (b_, 0, c)),
        ],
        out_specs=pl.BlockSpec((None, n2b, M, 128), lambda b_, c, g: (b_, g, 0, c)),
        out_shape=jax.ShapeDtypeStruct((B, N2, M, C), f32),
        compiler_params=_params("parallel", "parallel", "arbitrary"),
        name="dft_stage1",
    )(tab, z)


def _cat_bf16(re, im):
    return jnp.concatenate([re, im], axis=0).astype(bf16)


def _mid_kernel(m2_ref, m2i_ref, a_ref, kf_ref, o_ref, *, k1b):
    for q in range(k1b):
        x = jnp.dot(m2_ref[...], _cat_bf16(a_ref[:, 0, q, :], a_ref[:, 1, q, :]), preferred_element_type=f32)
        xr, xi = x[:N2], x[N2:]
        kf = kf_ref[q]
        kr, ki = kf[:N2], kf[N2:]
        y = _cat_bf16(xr * kr - xi * ki, xr * ki + xi * kr)
        bq = jnp.dot(m2i_ref[...], y, preferred_element_type=f32)
        o_ref[:, 0, q, :] = bq[:N2]
        o_ref[:, 1, q, :] = bq[N2:]


def spectral_mid(a, kf, m2, m2i, order, *, k1b=16):
    B, _, _, H, C = a.shape
    k1b = min(k1b, H)
    cpo = C // 128
    return pl.pallas_call(
        functools.partial(_mid_kernel, k1b=k1b),
        grid=(B, cpo, H // k1b),
        in_specs=[
            pl.BlockSpec((2 * N2, 2 * N2), lambda b_, c, k: (0, 0)),
            pl.BlockSpec((2 * N2, 2 * N2), lambda b_, c, k: (0, 0)),
            pl.BlockSpec((None, N2, 2, k1b, 128), lambda b_, c, k: (b_, 0, 0, k, c)),
            pl.BlockSpec((k1b, 2 * N2, 128), lambda b_, c, k: (k, 0, order * cpo + c)),
        ],
        out_specs=pl.BlockSpec((None, N2, 2, k1b, 128), lambda b_, c, k: (b_, 0, 0, k, c)),
        out_shape=jax.ShapeDtypeStruct(a.shape, f32),
        compiler_params=_params("parallel", "parallel", "arbitrary"),
        name="spectral_mid",
    )(m2, m2i, a, kf)


def _idft1_gate_kernel(tab_ref, b_ref, x_ref, v_ref, d_ref, o_ref, *scratch, n2b, H, rows):
    g = pl.program_id(2)
    y_ref = scratch[0] if scratch else o_ref
    for q in range(n2b):
        y = jnp.dot(tab_ref[q], b_ref[q].astype(bf16), preferred_element_type=f32)
        y_ref[pl.ds(g * n2b + q, H, stride=N2), :] = y

    @pl.when(g == pl.num_programs(2) - 1)
    def _():
        d = d_ref[...]

        def gate(i, c):
            r = pl.ds(pl.multiple_of(i * rows, rows), rows)
            o_ref[r, :] = (x_ref[r, :].astype(f32) * (y_ref[r, :] + v_ref[r, :] * d)).astype(o_ref.dtype)
            return c

        lax.fori_loop(0, (H * N2) // rows, gate, 0)


def idft1_gate(bm, tab, xg, xcol0, v, d, out_dtype, *, n2b=16, rows=256):
    B, _, M, C = bm.shape
    H = M // 2
    L = H * N2
    xb = xcol0 // 128
    inplace = out_dtype == f32
    return pl.pallas_call(
        functools.partial(_idft1_gate_kernel, n2b=n2b, H=H, rows=rows),
        grid=(B, C // 128, N2 // n2b),
        in_specs=[
            pl.BlockSpec((n2b, H, M), lambda b_, c, g: (g, 0, 0)),
            pl.BlockSpec((None, n2b, M, 128), lambda b_, c, g: (b_, g, 0, c)),
            pl.BlockSpec((None, L, 128), lambda b_, c, g: (b_, 0, xb + c)),
            pl.BlockSpec((None, L, 128), lambda b_, c, g: (b_, 0, c)),
            pl.BlockSpec((1, 128), lambda b_, c, g: (0, c)),
        ],
        out_specs=pl.BlockSpec((None, L, 128), lambda b_, c, g: (b_, 0, c)),
        out_shape=jax.ShapeDtypeStruct((B, L, C), out_dtype),
        scratch_shapes=[] if inplace else [pltpu.VMEM((L, 128), f32)],
        compiler_params=_params("parallel", "parallel", "arbitrary"),
        name="idft1_gate",
    )(tab, bm, xg, v, d)


def _filter_mlp_kernel(feat_ref, w1_ref, b1_ref, w2_ref, b2_ref, w3_ref, b3_ref, w4_ref, fr_ref, dl_ref,
                       p_ref, q_ref, nrm_ref):
    hp = lax.Precision.HIGHEST
    i = pl.program_id(0)
    feat = feat_ref[...]
    fr = fr_ref[...]
    h = jnp.sin(fr * (jnp.dot(feat, w1_ref[...], precision=hp, preferred_element_type=f32) + b1_ref[...]))
    h = jnp.sin(fr * (jnp.dot(h, w2_ref[...], precision=hp, preferred_element_type=f32) + b2_ref[...]))
    h = jnp.sin(fr * (jnp.dot(h, w3_ref[...], precision=hp, preferred_element_type=f32) + b3_ref[...]))
    win = jnp.exp(-feat[:, 0:1] * dl_ref[...])
    row = lax.broadcasted_iota(jnp.int32, win.shape, 0) + i * feat.shape[0]
    C = D_HYENA
    nrm = []
    for o in range(HYENA_ORDER):
        c0 = o * N_DIR * C
        fwd = jnp.dot(h, w4_ref[:, c0:c0 + C], precision=hp, preferred_element_type=f32) * win
        bwd = jnp.dot(h, w4_ref[:, c0 + C:c0 + 2 * C], precision=hp, preferred_element_type=f32) * win
        bwd = jnp.where(row == 0, 0.0, bwd)
        p_ref[:, o * C:(o + 1) * C] = fwd + bwd
        q_ref[:, o * C:(o + 1) * C] = fwd - bwd
        nrm.append(jnp.sum(jnp.abs(fwd) + jnp.abs(bwd), axis=0, keepdims=True))
    nrm = jnp.concatenate(nrm, axis=1)

    @pl.when(i == 0)
    def _():
        nrm_ref[...] = nrm

    @pl.when(i != 0)
    def _():
        nrm_ref[...] += nrm


def filter_mlp(feat, w1, b1, w2, b2, w3, b3, w4, freq, deltas, *, tm=256):
    L = feat.shape[0]
    CO = HYENA_ORDER * D_HYENA
    full = lambda a: pl.BlockSpec(a.shape, lambda i: (0,) * a.ndim)
    args = (w1, b1, w2, b2, w3, b3, w4, freq, deltas)
    return pl.pallas_call(
        _filter_mlp_kernel,
        grid=(L // tm,),
        in_specs=[pl.BlockSpec((tm, 128), lambda i: (i, 0))] + [full(a) for a in args],
        out_specs=[pl.BlockSpec((tm, CO), lambda i: (i, 0)), pl.BlockSpec((tm, CO), lambda i: (i, 0)),
                   pl.BlockSpec((1, CO), lambda i: (0, 0))],
        out_shape=[jax.ShapeDtypeStruct((L, CO), f32), jax.ShapeDtypeStruct((L, CO), f32),
                   jax.ShapeDtypeStruct((1, CO), f32)],
        compiler_params=_params("arbitrary"),
        name="filter_mlp",
    )(feat, *args)


def _filter_stage2_kernel(m2_ref, ap_ref, aq_ref, nrm_ref, o_ref, *, k1b):
    inv = 1.0 / nrm_ref[...]

    def split_dot(m, a):
        a_hi = a.astype(bf16)
        a_lo = (a - a_hi.astype(f32)).astype(bf16)
        return jnp.dot(m, a_hi, preferred_element_type=f32) + jnp.dot(m, a_lo, preferred_element_type=f32)

    for q in range(k1b):
        ap = jnp.concatenate([ap_ref[:, 0, q, :], ap_ref[:, 1, q, :]], axis=0)
        aq = jnp.concatenate([aq_ref[:, 0, q, :], aq_ref[:, 1, q, :]], axis=0)
        o_ref[q, 0:N2, :] = split_dot(m2_ref[0:N2, :], ap) * inv
        o_ref[q, N2:2 * N2, :] = split_dot(m2_ref[N2:2 * N2, :], aq) * inv


def filter_stage2(apq, m2, nrm, *, k1b=16):
    _, _, _, H, CO = apq.shape
    k1b = min(k1b, H)
    return pl.pallas_call(
        functools.partial(_filter_stage2_kernel, k1b=k1b),
        grid=(CO // 128, H // k1b),
        in_specs=[
            pl.BlockSpec((2 * N2, 2 * N2), lambda c, k: (0, 0)),
            pl.BlockSpec((None, N2, 2, k1b, 128), lambda c, k: (0, 0, 0, k, c)),
            pl.BlockSpec((None, N2, 2, k1b, 128), lambda c, k: (1, 0, 0, k, c)),
            pl.BlockSpec((1, 128), lambda c, k: (0, c)),
        ],
        out_specs=pl.BlockSpec((k1b, 2 * N2, 128), lambda c, k: (k, 0, c)),
        out_shape=jax.ShapeDtypeStruct((H, 2 * N2, CO), f32),
        compiler_params=_params("parallel", "arbitrary"),
        name="filter_stage2",
    )(m2, apq, apq, nrm)


def _pad2(a, rows, cols):
    return jnp.pad(a, ((0, rows - a.shape[0]), (0, cols - a.shape[1])))


def hyena_filter(L, w1, b1, w2, b2, w3, b3, w4, freq, tab_f, m2):
    pos = jnp.arange(L, dtype=f32)[:, None]
    t = jnp.linspace(0.0, 1.0, L, dtype=f32)[:, None]
    bands = (FILTER_EMB - 1) // 2
    z = jnp.linspace(1e-4, bands - 1, bands, dtype=f32)[None, :] * ((2.0 * math.pi / L) * pos)
    feat = _pad2(jnp.concatenate([t, jnp.cos(z), -jnp.sin(z)], axis=-1), L, 128)
    deltas = jnp.abs(jnp.linspace(math.log(DECAY_TARGET) / DECAY_SLOW_PCT,
                                  math.log(DECAY_TARGET) / DECAY_FAST_PCT, D_HYENA, dtype=f32))[None, :]
    row = lambda v: _pad2(v[None, :], 1, 128)
    p, q, nrm = filter_mlp(feat, _pad2(w1, 128, 128), row(b1), _pad2(w2, 128, 128), row(b2),
                           _pad2(w3, 128, 128), row(b3), _pad2(w4, 128, w4.shape[1]), row(freq), deltas)
    apq = dft_stage1(jnp.stack([p, q]), tab_f)
    H = L // N2
    return filter_stage2(apq.reshape(2, N2, 2, H, HYENA_ORDER * D_HYENA), m2, nrm)


def hyena_mixer(p3, conv_w, conv_b, hy_d, kf, tabs):
    tab_f, tab_i, m2, m2i = tabs
    B, L, _ = p3.shape
    C = D_HYENA
    H = L // N2
    cb = conv_b[None, :]
    xg = short_conv(p3, conv_w, cb, 0, 2 * C, bf16)
    z = short_conv(p3, conv_w, cb, 2 * C, C, f32)
    for o in range(HYENA_ORDER):
        a = dft_stage1(z, tab_f).reshape(B, N2, 2, H, C)
        bm = spectral_mid(a, kf, m2, m2i, o).reshape(B, N2, 2 * H, C)
        last = o == HYENA_ORDER - 1
        z = idft1_gate(bm, tab_i, xg, o * C, z, hy_d[o:o + 1], bf16 if last else f32)
    return z


def _rmsnorm(x, g):
    xf = x.astype(f32)
    return xf * lax.rsqrt(jnp.mean(xf * xf, axis=-1, keepdims=True) + NORM_EPS) * g.astype(f32)


def _gelu(x):
    return 0.5 * x * (1.0 + lax.erf(x * (1.0 / math.sqrt(2.0))))


def _sgu_kernel(pu_ref, pv_ref, g_ref, b_ref, ws_ref, bs_ref, o_ref):
    v = _gelu(pv_ref[...].astype(f32))
    mu = jnp.mean(v, axis=-1, keepdims=True)
    vc = v - mu
    var = jnp.mean(vc * vc, axis=-1, keepdims=True)
    vn = (vc * lax.rsqrt(var + LN_EPS) * g_ref[...] + b_ref[...]).astype(bf16)
    for c in range(pv_ref.shape[0] // CHUNK):
        rows = slice(c * CHUNK, (c + 1) * CHUNK)
        for g in range(SGU_GROUPS):
            cols = slice(g * SGU_GROUP_DIM, (g + 1) * SGU_GROUP_DIM)
            s = jnp.dot(ws_ref[g], vn[rows, cols], preferred_element_type=f32) + bs_ref[:, g:g + 1]
            u = _gelu(pu_ref[rows, cols].astype(f32))
            o_ref[rows, cols] = (u * s).astype(o_ref.dtype)


def sgu(p, ln_g, ln_b, w_s, b_s_t, *, tm=512):
    n = p.shape[0]
    ub = OFF_SG // D_SGU
    return pl.pallas_call(
        _sgu_kernel,
        grid=(n // tm,),
        in_specs=[
            pl.BlockSpec((tm, D_SGU), lambda i: (i, ub)),
            pl.BlockSpec((tm, D_SGU), lambda i: (i, ub + 1)),
            pl.BlockSpec((1, D_SGU), lambda i: (0, 0)),
            pl.BlockSpec((1, D_SGU), lambda i: (0, 0)),
            pl.BlockSpec((SGU_GROUPS, CHUNK, CHUNK), lambda i: (0, 0, 0)),
            pl.BlockSpec((CHUNK, SGU_GROUPS), lambda i: (0, 0)),
        ],
        out_specs=pl.BlockSpec((tm, D_SGU), lambda i: (i, 0)),
        out_shape=jax.ShapeDtypeStruct((n, D_SGU), bf16),
        compiler_params=_params("parallel"),
        name="sgu",
    )(p, p, ln_g, ln_b, w_s, b_s_t)


def _route(xn, w_router):
    n = xn.shape[0]
    cap = EC_CAPACITY * n // N_EXPERTS
    logits = jnp.dot(xn, w_router, precision=lax.Precision.HIGHEST)
    probs = jax.nn.softmax(logits, axis=-1)
    gates, idx = lax.top_k(probs.T, cap)
    return gates, idx


def kernel(x_prompt, x_sample, norm_mix_g, w_in, hy_conv_w, hy_conv_b, flt_w1, flt_b1, flt_w2, flt_b2,
           flt_w3, flt_b3, flt_w4, flt_sin_freq, hy_d, sg_ln_g, sg_ln_b, sg_w, sg_b, w_branch_hy,
           w_branch_sg, w_out, norm_moe_g, w_router, w_gate, w_up, w_down, norm_final_g):
    w_in_b = w_in[0].astype(bf16)
    w_bh_b = w_branch_hy[0].astype(bf16)
    w_bs_b = w_branch_sg[0].astype(bf16)
    w_out_b = w_out[0].astype(bf16)
    wg_b = w_gate[0].astype(bf16)
    wu_b = w_up[0].astype(bf16)
    wd_b = w_down[0].astype(bf16)
    g_mix = norm_mix_g[0].reshape(1, D_MODEL)
    sg_w_b = sg_w[0].astype(bf16)
    sg_b_t = sg_b[0].T

    m2, m2i = _stage2_tables()

    def mixer(x):
        B, L, D = x.shape
        n = B * L
        xt = x.reshape(n, D)
        p = in_proj(xt, g_mix, w_in_b)
        p3 = p.reshape(B, L, D_IN)
        tab_f, tab_i = _stage1_tables(L // N2)
        kf = hyena_filter(L, flt_w1[0], flt_b1[0], flt_w2[0], flt_b2[0], flt_w3[0], flt_b3[0],
                          flt_w4[0], flt_sin_freq[0], tab_f, m2)
        y_hy = hyena_mixer(p3, hy_conv_w[0], hy_conv_b[0], hy_d[0], kf,
                           (tab_f, tab_i, m2, m2i)).reshape(n, D_HYENA)
        y_sg = sgu(p, sg_ln_g, sg_ln_b, sg_w_b, sg_b_t)
        merged = branch_merge(y_hy, y_sg, w_bh_b, w_bs_b, p)
        return out_proj(merged, w_out_b, xt)

    xm_p = mixer(x_prompt)
    xm_s = mixer(x_sample)

    groups = []
    for xm in (xm_p, xm_s):
        xn = _rmsnorm(xm, norm_moe_g[0])
        gates, idx = _route(xn, w_router[0])
        xe = jnp.take(xn.astype(bf16), idx, axis=0)
        groups.append((xm, gates, idx, xe))

    xe_all = jnp.concatenate([g[3] for g in groups], axis=1)
    ye_all = expert_ffn(xe_all, wg_b, wu_b, wd_b)

    outs = []
    off = 0
    for (xm, gates, idx, xe), xin in zip(groups, (x_prompt, x_sample)):
        cap = idx.shape[1]
        ye = ye_all[:, off:off + cap].astype(f32) * gates[..., None]
        off += cap
        moe = jnp.zeros_like(xm).at[idx.reshape(-1)].add(ye.reshape(-1, D_MODEL))
        y = _rmsnorm(xm + moe, norm_final_g)
        outs.append(y.reshape(xin.shape))
    return tuple(outs)
```

```python
import functools
import math

import jax
import jax.numpy as jnp
from jax import lax
from jax.experimental import pallas as pl
from jax.experimental.pallas import tpu as pltpu

f32 = jnp.float32
bf16 = jnp.bfloat16
i32 = jnp.int32

D_MODEL = 2048
D_HYENA = D_MODEL // 2
HYENA_ORDER = 2
N_DIR = 2
FILTER_EMB = 33
DECAY_FAST_PCT = 0.3
DECAY_SLOW_PCT = 1.5
DECAY_TARGET = 1e-2
D_SGU = D_MODEL // 2
CHUNK = 128
SGU_GROUPS = 8
SGU_GROUP_DIM = D_SGU // SGU_GROUPS
N_EXPERTS = 16
EC_CAPACITY = 2
NORM_EPS = 1e-6
LN_EPS = 1e-5
OFF_SG = 3 * D_HYENA
OFF_GH = OFF_SG + 2 * D_SGU
OFF_GS = OFF_GH + D_MODEL
D_IN = OFF_GS + D_MODEL

VMEM_LIMIT_BYTES = 56 * 1024 * 1024


def _params(*sem):
    return pltpu.CompilerParams(dimension_semantics=sem, vmem_limit_bytes=VMEM_LIMIT_BYTES)


def _inproj_kernel(x_ref, g_ref, w_ref, o_ref, xn_ref):
    @pl.when(pl.program_id(1) == 0)
    def _():
        x = x_ref[...]
        ms = jnp.mean(x * x, axis=-1, keepdims=True)
        xn_ref[...] = (x * lax.rsqrt(ms + NORM_EPS) * g_ref[...]).astype(bf16)

    o_ref[...] = jnp.dot(xn_ref[...], w_ref[...], preferred_element_type=f32).astype(o_ref.dtype)


def in_proj(x, g, w, *, tm=1024, tn=1024):
    n, d = x.shape
    dn = w.shape[1]
    return pl.pallas_call(
        _inproj_kernel,
        grid=(n // tm, dn // tn),
        in_specs=[
            pl.BlockSpec((tm, d), lambda i, j: (i, 0)),
            pl.BlockSpec((1, d), lambda i, j: (0, 0)),
            pl.BlockSpec((d, tn), lambda i, j: (0, j)),
        ],
        out_specs=pl.BlockSpec((tm, tn), lambda i, j: (i, j)),
        out_shape=jax.ShapeDtypeStruct((n, dn), bf16),
        scratch_shapes=[pltpu.VMEM((tm, d), bf16)],
        compiler_params=_params("parallel", "arbitrary"),
        name="in_proj",
    )(x, g, w)


def _merge_kernel(yh_ref, ys_ref, wbh_ref, wbs_ref, gh_ref, gs_ref, o_ref):
    a = jnp.dot(yh_ref[...], wbh_ref[...], preferred_element_type=f32)
    b = jnp.dot(ys_ref[...], wbs_ref[...], preferred_element_type=f32)
    o = jax.nn.sigmoid(gh_ref[...].astype(f32)) * a + jax.nn.sigmoid(gs_ref[...].astype(f32)) * b
    o_ref[...] = o.astype(o_ref.dtype)


def branch_merge(y_hy, y_sg, w_bh, w_bs, p, *, tm=1024, tn=1024):
    n, dh = y_hy.shape
    d = w_bh.shape[1]
    gh0 = OFF_GH // tn
    gs0 = OFF_GS // tn
    return pl.pallas_call(
        _merge_kernel,
        grid=(n // tm, d // tn),
        in_specs=[
            pl.BlockSpec((tm, dh), lambda i, j: (i, 0)),
            pl.BlockSpec((tm, dh), lambda i, j: (i, 0)),
            pl.BlockSpec((dh, tn), lambda i, j: (0, j)),
            pl.BlockSpec((dh, tn), lambda i, j: (0, j)),
            pl.BlockSpec((tm, tn), lambda i, j: (i, gh0 + j)),
            pl.BlockSpec((tm, tn), lambda i, j: (i, gs0 + j)),
        ],
        out_specs=pl.BlockSpec((tm, tn), lambda i, j: (i, j)),
        out_shape=jax.ShapeDtypeStruct((n, d), bf16),
        compiler_params=_params("parallel", "arbitrary"),
        name="branch_merge",
    )(y_hy, y_sg, w_bh, w_bs, p, p)


def _outproj_kernel(m_ref, w_ref, x_ref, o_ref):
    o_ref[...] = x_ref[...] + jnp.dot(m_ref[...], w_ref[...], preferred_element_type=f32)


def out_proj(merged, w, x, *, tm=1024, tn=1024):
    n, d = merged.shape
    dn = w.shape[1]
    return pl.pallas_call(
        _outproj_kernel,
        grid=(n // tm, dn // tn),
        in_specs=[
            pl.BlockSpec((tm, d), lambda i, j: (i, 0)),
            pl.BlockSpec((d, tn), lambda i, j: (0, j)),
            pl.BlockSpec((tm, tn), lambda i, j: (i, j)),
        ],
        out_specs=pl.BlockSpec((tm, tn), lambda i, j: (i, j)),
        out_shape=jax.ShapeDtypeStruct((n, dn), f32),
        compiler_params=_params("parallel", "arbitrary"),
        name="out_proj",
    )(merged, w, x)


ROUTE_TILE = 256
SLOT_CHUNK = 48
LANES = 128
ROW_TILES = D_MODEL // LANES
BF16_ROWS = 16
READ_CHUNK = SLOT_CHUNK + BF16_ROWS


def _rms(x, g):
    return x * lax.rsqrt(jnp.mean(x * x, axis=-1, keepdims=True) + NORM_EPS) * g


def _router_kernel(x_ref, g_ref, wr_ref, o_ref):
    xn = _rms(x_ref[...], g_ref[...])
    logits = lax.dot_general(wr_ref[...], xn, (((1,), (1,)), ((), ())), precision=lax.Precision.HIGHEST,
                             preferred_element_type=f32)[:N_EXPERTS]
    m = jnp.max(logits, axis=0, keepdims=True)
    e = jnp.exp(logits - m)
    o_ref[...] = e / jnp.sum(e, axis=0, keepdims=True)


def router_probs(xm, g, wr_t, *, tm=512):
    n, d = xm.shape
    return pl.pallas_call(
        _router_kernel,
        grid=(n // tm,),
        in_specs=[pl.BlockSpec((tm, d), lambda i: (i, 0)), pl.BlockSpec((1, d), lambda i: (0, 0)),
                  pl.BlockSpec((LANES, d), lambda i: (0, 0))],
        out_specs=pl.BlockSpec((N_EXPERTS, tm), lambda i: (0, i)),
        out_shape=jax.ShapeDtypeStruct((N_EXPERTS, n), f32),
        compiler_params=_params("parallel"),
        name="router_probs",
    )(xm, g, wr_t)


def _select_kernel(p_ref, pos_ref, gate_ref, offs_ref, *, cap):
    p = p_ref[...]
    R = p.shape[0]
    bits = pltpu.bitcast(p, i32)

    def body(i, prefix):
        cand = prefix | lax.shift_left(jnp.int32(1), 30 - i)
        cnt = jnp.sum((bits >= cand).astype(i32))
        return jnp.where(cnt >= cap, cand, prefix)

    thr = lax.fori_loop(0, 31, body, jnp.int32(0))
    gt = bits > thr
    eq = bits == thr
    need = cap - jnp.sum(gt.astype(i32))

    li = lax.broadcasted_iota(i32, (LANES, LANES), 0)
    lj = lax.broadcasted_iota(i32, (LANES, LANES), 1)
    upper = (li <= lj).astype(bf16)
    ri = lax.broadcasted_iota(i32, (R, R), 0)
    rj = lax.broadcasted_iota(i32, (R, R), 1)
    rows_before = (rj < ri).astype(bf16)

    def prefix_counts(mask):
        m = mask.astype(bf16)
        incl = jnp.dot(m, upper, preferred_element_type=f32)
        tot = jnp.broadcast_to(incl[:, LANES - 1:LANES], (R, LANES)).astype(bf16)
        row_off = jnp.dot(rows_before, tot, preferred_element_type=f32)
        return incl - mask.astype(f32) + row_off, row_off

    eq_excl, _ = prefix_counts(eq)
    sel = gt | (eq & (eq_excl < need.astype(f32)))
    pos, row_off = prefix_counts(sel)
    pos_ref[...] = jnp.where(sel, pos.astype(i32), -1)
    gate_ref[...] = jnp.where(sel, p, 0.0)
    offs_ref[...] = row_off[:, 0:1].astype(i32)


def select_tokens(probs3, cap):
    E, R, _ = probs3.shape
    blk = pl.BlockSpec((None, R, LANES), lambda e: (e, 0, 0))
    return pl.pallas_call(
        functools.partial(_select_kernel, cap=cap),
        grid=(E,),
        in_specs=[blk],
        out_specs=[blk, blk, pl.BlockSpec((None, R, 1), lambda e: (e, 0, 0))],
        out_shape=[jax.ShapeDtypeStruct((E, R, LANES), i32), jax.ShapeDtypeStruct((E, R, LANES), f32),
                   jax.ShapeDtypeStruct((E, R, 1), i32)],
        compiler_params=_params("parallel"),
        name="select_tokens",
    )(probs3)


def _onehot_t(pos, start, width):
    j = lax.broadcasted_iota(i32, (width, pos.shape[1]), 0)
    return j == (pos - start)


def _dispatch_kernel(pos0_ref, x_ref, g_ref, pos_ref, gate_ref, xe_ref, ge_ref, xbuf_ref, gbuf_ref, sem_ref,
                     *, cap, n_tiles):
    i = pl.program_id(0)
    E, tm = pos_ref.shape
    slot = i % 2

    xn = _rms(x_ref[...], g_ref[...]).astype(bf16)
    eye = (lax.broadcasted_iota(i32, (E, LANES), 0) == lax.broadcasted_iota(i32, (E, LANES), 1)).astype(f32)
    gt = lax.dot_general(gate_ref[...], eye, (((0,), (0,)), ((), ())), precision=lax.Precision.HIGHEST,
                         preferred_element_type=f32)
    g_hi = gt.astype(bf16)
    g_lo = (gt - g_hi.astype(f32)).astype(bf16)
    pos = pos_ref[...]

    def base(e, tile):
        return pos0_ref[e * (n_tiles + 1) + tile]

    def count(e):
        return base(e, i + 1) - base(e, i)

    def build(c, dst):
        for e in range(E):
            oh = _onehot_t(pos[e:e + 1, :], base(e, i) + c * SLOT_CHUNK, SLOT_CHUNK)
            oh = jnp.where(oh, 1.0, 0.0).astype(bf16)
            rows = pl.ds(e * SLOT_CHUNK, SLOT_CHUNK)
            xr = jnp.dot(oh, xn, preferred_element_type=f32).astype(bf16)
            for s in range(ROW_TILES):
                xbuf_ref[dst, rows, s, :] = xr[:, s * LANES:(s + 1) * LANES]
            gbuf_ref[dst, rows, 0, :] = (jnp.dot(oh, g_hi, preferred_element_type=f32)
                                         + jnp.dot(oh, g_lo, preferred_element_type=f32))

    def copies(e, c, src, tile):
        rows = pl.ds(e * SLOT_CHUNK, SLOT_CHUNK)
        out_rows = pl.ds(base(e, tile) + c * SLOT_CHUNK, SLOT_CHUNK)
        return (pltpu.make_async_copy(xbuf_ref.at[src, rows], xe_ref.at[e, out_rows], sem_ref.at[0, src]),
                pltpu.make_async_copy(gbuf_ref.at[src, rows], ge_ref.at[e, out_rows], sem_ref.at[1, src]))

    @pl.when(i == 0)
    def _():
        xbuf_ref[2] = jnp.zeros(xbuf_ref.shape[1:], bf16)
        gbuf_ref[2] = jnp.zeros(gbuf_ref.shape[1:], f32)
        for e in range(E):
            rows = pl.ds(e * SLOT_CHUNK, SLOT_CHUNK)
            pad = pl.ds(cap, SLOT_CHUNK)
            for cp in (pltpu.make_async_copy(xbuf_ref.at[2, rows], xe_ref.at[e, pad], sem_ref.at[0, 2]),
                       pltpu.make_async_copy(gbuf_ref.at[2, rows], ge_ref.at[e, pad], sem_ref.at[1, 2])):
                cp.start()
                cp.wait()

    build(0, slot)

    @pl.when(i > 0)
    def _():
        for e in range(E):
            for cp in copies(e, 0, 1 - slot, i - 1):
                cp.wait()

    for e in range(E):
        for cp in copies(e, 0, slot, i):
            cp.start()

    nch = jnp.int32(0)
    for e in range(E):
        nch = jnp.maximum(nch, (count(e) + SLOT_CHUNK - 1) // SLOT_CHUNK)

    def overflow(c, carry):
        build(c, 2)
        for e in range(E):
            @pl.when(count(e) > c * SLOT_CHUNK)
            def _():
                for cp in copies(e, c, 2, i):
                    cp.start()
                    cp.wait()
        return carry

    lax.fori_loop(1, nch, overflow, 0)

    @pl.when(i == n_tiles - 1)
    def _():
        for e in range(E):
            for cp in copies(e, 0, slot, i):
                cp.wait()


def dispatch(xm, g, pos, gates, pos0, cap):
    n, d = xm.shape
    E = pos.shape[0]
    n_tiles = n // ROUTE_TILE
    rows = cap + SLOT_CHUNK
    grid_spec = pltpu.PrefetchScalarGridSpec(
        num_scalar_prefetch=1,
        grid=(n_tiles,),
        in_specs=[
            pl.BlockSpec((ROUTE_TILE, d), lambda i, p0: (i, 0)),
            pl.BlockSpec((1, d), lambda i, p0: (0, 0)),
            pl.BlockSpec((E, ROUTE_TILE), lambda i, p0: (0, i)),
            pl.BlockSpec((E, ROUTE_TILE), lambda i, p0: (0, i)),
        ],
        out_specs=[pl.BlockSpec(memory_space=pl.ANY), pl.BlockSpec(memory_space=pl.ANY)],
        scratch_shapes=[pltpu.VMEM((3, E * SLOT_CHUNK, ROW_TILES, LANES), bf16),
                        pltpu.VMEM((3, E * SLOT_CHUNK, 1, LANES), f32),
                        pltpu.SemaphoreType.DMA((2, 3))],
    )
    return pl.pallas_call(
        functools.partial(_dispatch_kernel, cap=cap, n_tiles=n_tiles),
        grid_spec=grid_spec,
        out_shape=[jax.ShapeDtypeStruct((E, rows, ROW_TILES, LANES), bf16),
                   jax.ShapeDtypeStruct((E, rows, 1, LANES), f32)],
        compiler_params=_params("arbitrary"),
        name="dispatch",
    )(pos0, xm, g, pos, gates)


def _expert_kernel(xe_ref, ge_ref, wg_ref, wu_ref, wd_ref, o_ref, xb_ref, acc_ref):
    e = pl.program_id(0)
    f = pl.program_id(2)

    @pl.when(f == 0)
    def _():
        for s in range(ROW_TILES):
            xb_ref[:, s * LANES:(s + 1) * LANES] = xe_ref[:, s, :]

    xb = xb_ref[...]
    hg = jnp.dot(xb, wg_ref[...], preferred_element_type=f32)
    hu = jnp.dot(xb, wu_ref[...], preferred_element_type=f32)
    h = (hg * jax.nn.sigmoid(hg) * hu).astype(bf16)
    part = jnp.dot(h, wd_ref[...], preferred_element_type=f32)

    @pl.when(f == 0)
    def _():
        acc_ref[...] = part

    @pl.when(f != 0)
    def _():
        acc_ref[...] += part

    @pl.when(f == pl.num_programs(2) - 1)
    def _():
        gl = ge_ref[:, 0, :]
        lane = lax.broadcasted_iota(i32, gl.shape, 1)
        gate = jnp.sum(jnp.where(lane == e, gl, 0.0), axis=-1, keepdims=True)
        o_ref[...] = (acc_ref[...] * gate).astype(o_ref.dtype)


def expert_ffn(xe, ge, wg, wu, wd, cap, *, tm=1024, tf=512):
    E = xe.shape[0]
    d = D_MODEL
    dff = wg.shape[2]
    tm = min(tm, cap)
    return pl.pallas_call(
        _expert_kernel,
        grid=(E, cap // tm, dff // tf),
        in_specs=[
            pl.BlockSpec((None, tm, ROW_TILES, LANES), lambda e_, i, f: (e_, i, 0, 0)),
            pl.BlockSpec((None, tm, 1, LANES), lambda e_, i, f: (e_, i, 0, 0)),
            pl.BlockSpec((None, d, tf), lambda e_, i, f: (e_, 0, f)),
            pl.BlockSpec((None, d, tf), lambda e_, i, f: (e_, 0, f)),
            pl.BlockSpec((None, tf, d), lambda e_, i, f: (e_, f, 0)),
        ],
        out_specs=pl.BlockSpec((None, tm, d), lambda e_, i, f: (e_, i, 0)),
        out_shape=jax.ShapeDtypeStruct((E, cap, d), bf16),
        scratch_shapes=[pltpu.VMEM((tm, d), bf16), pltpu.VMEM((tm, d), f32)],
        compiler_params=_params("parallel", "parallel", "arbitrary"),
        name="expert_ffn",
    )(xe, ge, wg, wu, wd)


def _combine_kernel(pos0_ref, x_ref, g_ref, pos_ref, ye_ref, o_ref, buf_ref, sem_ref, *, cap, n_tiles):
    i = pl.program_id(0)
    E, tm = pos_ref.shape
    slot = i % 2

    def base(e, tile):
        return pos0_ref[e * (n_tiles + 1) + tile]

    def count(e):
        return base(e, i + 1) - base(e, i)

    def start_row(e, c, tile):
        st = base(e, tile) + c * SLOT_CHUNK
        st = jnp.minimum((st // BF16_ROWS) * BF16_ROWS, cap - READ_CHUNK)
        return pl.multiple_of(st, BF16_ROWS)

    def copy(e, c, dst, tile):
        return pltpu.make_async_copy(ye_ref.at[e, pl.ds(start_row(e, c, tile), READ_CHUNK)],
                                     buf_ref.at[dst, pl.ds(e * READ_CHUNK, READ_CHUNK)], sem_ref.at[dst])

    @pl.when(i == 0)
    def _():
        for e in range(E):
            copy(e, 0, slot, i).start()

    @pl.when(i + 1 < n_tiles)
    def _():
        for e in range(E):
            copy(e, 0, 1 - slot, i + 1).start()

    pos = pos_ref[...]

    def onehot(c):
        parts = []
        for e in range(E):
            pe = pos[e:e + 1, :]
            local = pe - base(e, i)
            keep = (local >= c * SLOT_CHUNK) & (local < (c + 1) * SLOT_CHUNK) & (pe >= 0)
            oh = _onehot_t(pe, start_row(e, c, i), READ_CHUNK) & keep
            parts.append(jnp.where(oh, 1.0, 0.0).astype(bf16))
        return jnp.concatenate(parts, axis=0)

    for e in range(E):
        copy(e, 0, slot, i).wait()
    acc = lax.dot_general(onehot(0), buf_ref[slot], (((0,), (0,)), ((), ())), preferred_element_type=f32)

    nch = jnp.int32(0)
    for e in range(E):
        nch = jnp.maximum(nch, (count(e) + SLOT_CHUNK - 1) // SLOT_CHUNK)

    def overflow(c, acc):
        for e in range(E):
            cp = copy(e, c, 2, i)
            cp.start()
            cp.wait()
        return acc + lax.dot_general(onehot(c), buf_ref[2], (((0,), (0,)), ((), ())), preferred_element_type=f32)

    acc = lax.fori_loop(1, nch, overflow, acc)
    o_ref[...] = _rms(x_ref[...] + acc, g_ref[...])


def combine(xm, g_final, pos, ye, pos0, cap):
    n, d = xm.shape
    E = pos.shape[0]
    n_tiles = n // ROUTE_TILE
    grid_spec = pltpu.PrefetchScalarGridSpec(
        num_scalar_prefetch=1,
        grid=(n_tiles,),
        in_specs=[
            pl.BlockSpec((ROUTE_TILE, d), lambda i, p0: (i, 0)),
            pl.BlockSpec((1, d), lambda i, p0: (0, 0)),
            pl.BlockSpec((E, ROUTE_TILE), lambda i, p0: (0, i)),
            pl.BlockSpec(memory_space=pl.ANY),
        ],
        out_specs=pl.BlockSpec((ROUTE_TILE, d), lambda i, p0: (i, 0)),
        scratch_shapes=[pltpu.VMEM((3, E * READ_CHUNK, d), bf16), pltpu.SemaphoreType.DMA((3,))],
    )
    return pl.pallas_call(
        functools.partial(_combine_kernel, cap=cap, n_tiles=n_tiles),
        grid_spec=grid_spec,
        out_shape=jax.ShapeDtypeStruct((n, d), f32),
        compiler_params=_params("arbitrary"),
        name="combine",
    )(pos0, xm, g_final, pos, ye)


def moe_block(xm, g_moe, wr_t, wg, wu, wd, g_final):
    n = xm.shape[0]
    cap = EC_CAPACITY * n // N_EXPERTS
    n_tiles = n // ROUTE_TILE
    probs = router_probs(xm, g_moe, wr_t)
    pos3, gate3, offs = select_tokens(probs.reshape(N_EXPERTS, n // LANES, LANES), cap)
    pos = pos3.reshape(N_EXPERTS, n)
    gates = gate3.reshape(N_EXPERTS, n)
    tile_off = offs[:, ::ROUTE_TILE // LANES, 0]
    pos0 = jnp.concatenate([tile_off, jnp.full((N_EXPERTS, 1), cap, i32)], axis=1).reshape(-1)
    xe, ge = dispatch(xm, g_moe, pos, gates, pos0, cap)
    ye = expert_ffn(xe, ge, wg, wu, wd, cap)
    return combine(xm, g_final, pos, ye, pos0, cap)


N2 = 128


def _angles_stage1(H):
    n_total = 2 * N2 * H
    n2 = jnp.arange(N2, dtype=jnp.int32)[:, None, None]
    k1 = jnp.arange(H, dtype=jnp.int32)[None, :, None]
    n1 = jnp.arange(H, dtype=jnp.int32)[None, None, :]
    m = ((2 * k1 + 1) * (N2 * n1 + n2)) % (2 * n_total)
    return m.astype(f32) * (math.pi / n_total)


def _stage1_tables(H):
    th = _angles_stage1(H)
    c, s = jnp.cos(th), jnp.sin(th)
    fwd = jnp.concatenate([c, -s], axis=1).astype(bf16)
    scale = 1.0 / (N2 * H)
    ct, st = jnp.swapaxes(c, 1, 2), jnp.swapaxes(s, 1, 2)
    inv = (jnp.concatenate([ct, -st], axis=2) * scale).astype(bf16)
    return fwd, inv


def _stage2_tables():
    j = jnp.arange(N2, dtype=jnp.int32)
    th = ((j[:, None] * j[None, :]) % N2).astype(f32) * (2.0 * math.pi / N2)
    c, s = jnp.cos(th), jnp.sin(th)
    fwd = jnp.concatenate([jnp.concatenate([c, s], 1), jnp.concatenate([-s, c], 1)], 0)
    inv = jnp.concatenate([jnp.concatenate([c, -s], 1), jnp.concatenate([s, c], 1)], 0)
    return fwd.astype(bf16), inv.astype(bf16)


def _shortconv_kernel(x_ref, w_ref, b_ref, o_ref, s_ref, *, rows):
    L = x_ref.shape[0]
    zeros8 = jnp.zeros((8, x_ref.shape[1]), f32)
    s_ref[0:8, :] = zeros8
    s_ref[L + 8:L + 16, :] = zeros8

    def load(i, c):
        r0 = pl.multiple_of(i * rows, rows)
        s_ref[pl.ds(r0 + 8, rows), :] = x_ref[pl.ds(r0, rows), :].astype(f32)
        return c

    lax.fori_loop(0, L // rows, load, 0)
    w0, w1, w2, b = w_ref[0:1, :], w_ref[1:2, :], w_ref[2:3, :], b_ref[...]

    def conv(i, c):
        r0 = pl.multiple_of(i * rows, rows)
        prev = s_ref[pl.ds(r0 + 7, rows), :]
        cur = s_ref[pl.ds(r0 + 8, rows), :]
        nxt = s_ref[pl.ds(r0 + 9, rows), :]
        o_ref[pl.ds(r0, rows), :] = (prev * w0 + cur * w1 + nxt * w2 + b).astype(o_ref.dtype)
        return c

    lax.fori_loop(0, L // rows, conv, 0)


def short_conv(p3, w, b, col0, ncols, out_dtype, *, rows=256):
    B, L, _ = p3.shape
    cb = col0 // 128
    return pl.pallas_call(
        functools.partial(_shortconv_kernel, rows=rows),
        grid=(B, ncols // 128),
        in_specs=[
            pl.BlockSpec((None, L, 128), lambda b_, j: (b_, 0, cb + j)),
            pl.BlockSpec((3, 128), lambda b_, j: (0, cb + j)),
            pl.BlockSpec((1, 128), lambda b_, j: (0, cb + j)),
        ],
        out_specs=pl.BlockSpec((None, L, 128), lambda b_, j: (b_, 0, j)),
        out_shape=jax.ShapeDtypeStruct((B, L, ncols), out_dtype),
        scratch_shapes=[pltpu.VMEM((L + 16, 128), f32)],
        compiler_params=_params("parallel", "parallel"),
        name="short_conv",
    )(p3, w, b)


def _dft1_kernel(tab_ref, z_ref, o_ref, *, n2b, K):
    g = pl.program_id(2)
    for q in range(n2b):
        zs = z_ref[pl.ds(g * n2b + q, K, stride=N2), :].astype(bf16)
        o_ref[q] = jnp.dot(tab_ref[q], zs, preferred_element_type=f32)


def dft_stage1(z, tab, *, n2b=16):
    B, Lz, C = z.shape
    _, M, K = tab.shape
    assert Lz == K * N2
    return pl.pallas_call(
        functools.partial(_dft1_kernel, n2b=n2b, K=K),
        grid=(B, C // 128, N2 // n2b),
        in_specs=[
            pl.BlockSpec((n2b, M, K), lambda b_, c, g: (g, 0, 0)),
            pl.BlockSpec((None, Lz, 128), lambda b_, c, g: (b_, 0, c)),
        ],
        out_specs=pl.BlockSpec((None, n2b, M, 128), lambda b_, c, g: (b_, g, 0, c)),
        out_shape=jax.ShapeDtypeStruct((B, N2, M, C), f32),
        compiler_params=_params("parallel", "parallel", "arbitrary"),
        name="dft_stage1",
    )(tab, z)


def _cat_bf16(re, im):
    return jnp.concatenate([re, im], axis=0).astype(bf16)


def _mid_kernel(m2_ref, m2i_ref, a_ref, kf_ref, o_ref, *, k1b):
    for q in range(k1b):
        x = jnp.dot(m2_ref[...], _cat_bf16(a_ref[:, 0, q, :], a_ref[:, 1, q, :]), preferred_element_type=f32)
        xr, xi = x[:N2], x[N2:]
        kf = kf_ref[q]
        kr, ki = kf[:N2], kf[N2:]
        y = _cat_bf16(xr * kr - xi * ki, xr * ki + xi * kr)
        bq = jnp.dot(m2i_ref[...], y, preferred_element_type=f32)
        o_ref[:, 0, q, :] = bq[:N2]
        o_ref[:, 1, q, :] = bq[N2:]


def spectral_mid(a, kf, m2, m2i, order, *, k1b=16):
    B, _, _, H, C = a.shape
    k1b = min(k1b, H)
    cpo = C // 128
    return pl.pallas_call(
        functools.partial(_mid_kernel, k1b=k1b),
        grid=(B, cpo, H // k1b),
        in_specs=[
            pl.BlockSpec((2 * N2, 2 * N2), lambda b_, c, k: (0, 0)),
            pl.BlockSpec((2 * N2, 2 * N2), lambda b_, c, k: (0, 0)),
            pl.BlockSpec((None, N2, 2, k1b, 128), lambda b_, c, k: (b_, 0, 0, k, c)),
            pl.BlockSpec((k1b, 2 * N2, 128), lambda b_, c, k: (k, 0, order * cpo + c)),
        ],
        out_specs=pl.BlockSpec((None, N2, 2, k1b, 128), lambda b_, c, k: (b_, 0, 0, k, c)),
        out_shape=jax.ShapeDtypeStruct(a.shape, f32),
        compiler_params=_params("parallel", "parallel", "arbitrary"),
        name="spectral_mid",
    )(m2, m2i, a, kf)


def _idft1_gate_kernel(tab_ref, b_ref, x_ref, v_ref, d_ref, o_ref, *scratch, n2b, H, rows):
    g = pl.program_id(2)
    y_ref = scratch[0] if scratch else o_ref
    for q in range(n2b):
        y = jnp.dot(tab_ref[q], b_ref[q].astype(bf16), preferred_element_type=f32)
        y_ref[pl.ds(g * n2b + q, H, stride=N2), :] = y

    @pl.when(g == pl.num_programs(2) - 1)
    def _():
        d = d_ref[...]

        def gate(i, c):
            r = pl.ds(pl.multiple_of(i * rows, rows), rows)
            o_ref[r, :] = (x_ref[r, :].astype(f32) * (y_ref[r, :] + v_ref[r, :] * d)).astype(o_ref.dtype)
            return c

        lax.fori_loop(0, (H * N2) // rows, gate, 0)


def idft1_gate(bm, tab, xg, xcol0, v, d, out_dtype, *, n2b=16, rows=256):
    B, _, M, C = bm.shape
    H = M // 2
    L = H * N2
    xb = xcol0 // 128
    inplace = out_dtype == f32
    return pl.pallas_call(
        functools.partial(_idft1_gate_kernel, n2b=n2b, H=H, rows=rows),
        grid=(B, C // 128, N2 // n2b),
        in_specs=[
            pl.BlockSpec((n2b, H, M), lambda b_, c, g: (g, 0, 0)),
            pl.BlockSpec((None, n2b, M, 128), lambda b_, c, g: (b_, g, 0, c)),
            pl.BlockSpec((None, L, 128), lambda b_, c, g: (b_, 0, xb + c)),
            pl.BlockSpec((None, L, 128), lambda b_, c, g: (b_, 0, c)),
            pl.BlockSpec((1, 128), lambda b_, c, g: (0, c)),
        ],
        out_specs=pl.BlockSpec((None, L, 128), lambda b_, c, g: (b_, 0, c)),
        out_shape=jax.ShapeDtypeStruct((B, L, C), out_dtype),
        scratch_shapes=[] if inplace else [pltpu.VMEM((L, 128), f32)],
        compiler_params=_params("parallel", "parallel", "arbitrary"),
        name="idft1_gate",
    )(tab, bm, xg, v, d)


def _filter_mlp_kernel(feat_ref, w1_ref, b1_ref, w2_ref, b2_ref, w3_ref, b3_ref, w4_ref, fr_ref, dl_ref,
                       p_ref, q_ref, nrm_ref):
    hp = lax.Precision.HIGHEST
    i = pl.program_id(0)
    feat = feat_ref[...]
    fr = fr_ref[...]
    h = jnp.sin(fr * (jnp.dot(feat, w1_ref[...], precision=hp, preferred_element_type=f32) + b1_ref[...]))
    h = jnp.sin(fr * (jnp.dot(h, w2_ref[...], precision=hp, preferred_element_type=f32) + b2_ref[...]))
    h = jnp.sin(fr * (jnp.dot(h, w3_ref[...], precision=hp, preferred_element_type=f32) + b3_ref[...]))
    win = jnp.exp(-feat[:, 0:1] * dl_ref[...])
    row = lax.broadcasted_iota(jnp.int32, win.shape, 0) + i * feat.shape[0]
    C = D_HYENA
    nrm = []
    for o in range(HYENA_ORDER):
        c0 = o * N_DIR * C
        fwd = jnp.dot(h, w4_ref[:, c0:c0 + C], precision=hp, preferred_element_type=f32) * win
        bwd = jnp.dot(h, w4_ref[:, c0 + C:c0 + 2 * C], precision=hp, preferred_element_type=f32) * win
        bwd = jnp.where(row == 0, 0.0, bwd)
        p_ref[:, o * C:(o + 1) * C] = fwd + bwd
        q_ref[:, o * C:(o + 1) * C] = fwd - bwd
        nrm.append(jnp.sum(jnp.abs(fwd) + jnp.abs(bwd), axis=0, keepdims=True))
    nrm = jnp.concatenate(nrm, axis=1)

    @pl.when(i == 0)
    def _():
        nrm_ref[...] = nrm

    @pl.when(i != 0)
    def _():
        nrm_ref[...] += nrm


def filter_mlp(feat, w1, b1, w2, b2, w3, b3, w4, freq, deltas, *, tm=256):
    L = feat.shape[0]
    CO = HYENA_ORDER * D_HYENA
    full = lambda a: pl.BlockSpec(a.shape, lambda i: (0,) * a.ndim)
    args = (w1, b1, w2, b2, w3, b3, w4, freq, deltas)
    return pl.pallas_call(
        _filter_mlp_kernel,
        grid=(L // tm,),
        in_specs=[pl.BlockSpec((tm, 128), lambda i: (i, 0))] + [full(a) for a in args],
        out_specs=[pl.BlockSpec((tm, CO), lambda i: (i, 0)), pl.BlockSpec((tm, CO), lambda i: (i, 0)),
                   pl.BlockSpec((1, CO), lambda i: (0, 0))],
        out_shape=[jax.ShapeDtypeStruct((L, CO), f32), jax.ShapeDtypeStruct((L, CO), f32),
                   jax.ShapeDtypeStruct((1, CO), f32)],
        compiler_params=_params("arbitrary"),
        name="filter_mlp",
    )(feat, *args)


def _filter_stage2_kernel(m2_ref, ap_ref, aq_ref, nrm_ref, o_ref, *, k1b):
    inv = 1.0 / nrm_ref[...]

    def split_dot(m, a):
        a_hi = a.astype(bf16)
        a_lo = (a - a_hi.astype(f32)).astype(bf16)
        return jnp.dot(m, a_hi, preferred_element_type=f32) + jnp.dot(m, a_lo, preferred_element_type=f32)

    for q in range(k1b):
        ap = jnp.concatenate([ap_ref[:, 0, q, :], ap_ref[:, 1, q, :]], axis=0)
        aq = jnp.concatenate([aq_ref[:, 0, q, :], aq_ref[:, 1, q, :]], axis=0)
        o_ref[q, 0:N2, :] = split_dot(m2_ref[0:N2, :], ap) * inv
        o_ref[q, N2:2 * N2, :] = split_dot(m2_ref[N2:2 * N2, :], aq) * inv


def filter_stage2(apq, m2, nrm, *, k1b=16):
    _, _, _, H, CO = apq.shape
    k1b = min(k1b, H)
    return pl.pallas_call(
        functools.partial(_filter_stage2_kernel, k1b=k1b),
        grid=(CO // 128, H // k1b),
        in_specs=[
            pl.BlockSpec((2 * N2, 2 * N2), lambda c, k: (0, 0)),
            pl.BlockSpec((None, N2, 2, k1b, 128), lambda c, k: (0, 0, 0, k, c)),
            pl.BlockSpec((None, N2, 2, k1b, 128), lambda c, k: (1, 0, 0, k, c)),
            pl.BlockSpec((1, 128), lambda c, k: (0, c)),
        ],
        out_specs=pl.BlockSpec((k1b, 2 * N2, 128), lambda c, k: (k, 0, c)),
        out_shape=jax.ShapeDtypeStruct((H, 2 * N2, CO), f32),
        compiler_params=_params("parallel", "arbitrary"),
        name="filter_stage2",
    )(m2, apq, apq, nrm)


def _pad2(a, rows, cols):
    return jnp.pad(a, ((0, rows - a.shape[0]), (0, cols - a.shape[1])))


def hyena_filter(L, w1, b1, w2, b2, w3, b3, w4, freq, tab_f, m2):
    pos = jnp.arange(L, dtype=f32)[:, None]
    t = jnp.linspace(0.0, 1.0, L, dtype=f32)[:, None]
    bands = (FILTER_EMB - 1) // 2
    z = jnp.linspace(1e-4, bands - 1, bands, dtype=f32)[None, :] * ((2.0 * math.pi / L) * pos)
    feat = _pad2(jnp.concatenate([t, jnp.cos(z), -jnp.sin(z)], axis=-1), L, 128)
    deltas = jnp.abs(jnp.linspace(math.log(DECAY_TARGET) / DECAY_SLOW_PCT,
                                  math.log(DECAY_TARGET) / DECAY_FAST_PCT, D_HYENA, dtype=f32))[None, :]
    row = lambda v: _pad2(v[None, :], 1, 128)
    p, q, nrm = filter_mlp(feat, _pad2(w1, 128, 128), row(b1), _pad2(w2, 128, 128), row(b2),
                           _pad2(w3, 128, 128), row(b3), _pad2(w4, 128, w4.shape[1]), row(freq), deltas)
    apq = dft_stage1(jnp.stack([p, q]), tab_f)
    H = L // N2
    return filter_stage2(apq.reshape(2, N2, 2, H, HYENA_ORDER * D_HYENA), m2, nrm)


def hyena_mixer(p3, conv_w, conv_b, hy_d, kf, tabs):
    tab_f, tab_i, m2, m2i = tabs
    B, L, _ = p3.shape
    C = D_HYENA
    H = L // N2
    cb = conv_b[None, :]
    xg = short_conv(p3, conv_w, cb, 0, 2 * C, bf16)
    z = short_conv(p3, conv_w, cb, 2 * C, C, f32)
    for o in range(HYENA_ORDER):
        a = dft_stage1(z, tab_f).reshape(B, N2, 2, H, C)
        bm = spectral_mid(a, kf, m2, m2i, o).reshape(B, N2, 2 * H, C)
        last = o == HYENA_ORDER - 1
        z = idft1_gate(bm, tab_i, xg, o * C, z, hy_d[o:o + 1], bf16 if last else f32)
    return z


def _gelu(x):
    return 0.5 * x * (1.0 + lax.erf(x * (1.0 / math.sqrt(2.0))))


def _sgu_kernel(pu_ref, pv_ref, g_ref, b_ref, ws_ref, bs_ref, o_ref):
    v = _gelu(pv_ref[...].astype(f32))
    mu = jnp.mean(v, axis=-1, keepdims=True)
    vc = v - mu
    var = jnp.mean(vc * vc, axis=-1, keepdims=True)
    vn = (vc * lax.rsqrt(var + LN_EPS) * g_ref[...] + b_ref[...]).astype(bf16)
    for c in range(pv_ref.shape[0] // CHUNK):
        rows = slice(c * CHUNK, (c + 1) * CHUNK)
        for g in range(SGU_GROUPS):
            cols = slice(g * SGU_GROUP_DIM, (g + 1) * SGU_GROUP_DIM)
            s = jnp.dot(ws_ref[g], vn[rows, cols], preferred_element_type=f32) + bs_ref[:, g:g + 1]
            u = _gelu(pu_ref[rows, cols].astype(f32))
            o_ref[rows, cols] = (u * s).astype(o_ref.dtype)


def sgu(p, ln_g, ln_b, w_s, b_s_t, *, tm=512):
    n = p.shape[0]
    ub = OFF_SG // D_SGU
    return pl.pallas_call(
        _sgu_kernel,
        grid=(n // tm,),
        in_specs=[
            pl.BlockSpec((tm, D_SGU), lambda i: (i, ub)),
            pl.BlockSpec((tm, D_SGU), lambda i: (i, ub + 1)),
            pl.BlockSpec((1, D_SGU), lambda i: (0, 0)),
            pl.BlockSpec((1, D_SGU), lambda i: (0, 0)),
            pl.BlockSpec((SGU_GROUPS, CHUNK, CHUNK), lambda i: (0, 0, 0)),
            pl.BlockSpec((CHUNK, SGU_GROUPS), lambda i: (0, 0)),
        ],
        out_specs=pl.BlockSpec((tm, D_SGU), lambda i: (i, 0)),
        out_shape=jax.ShapeDtypeStruct((n, D_SGU), bf16),
        compiler_params=_params("parallel"),
        name="sgu",
    )(p, p, ln_g, ln_b, w_s, b_s_t)


def kernel(x_prompt, x_sample, norm_mix_g, w_in, hy_conv_w, hy_conv_b, flt_w1, flt_b1, flt_w2, flt_b2,
           flt_w3, flt_b3, flt_w4, flt_sin_freq, hy_d, sg_ln_g, sg_ln_b, sg_w, sg_b, w_branch_hy,
           w_branch_sg, w_out, norm_moe_g, w_router, w_gate, w_up, w_down, norm_final_g):
    w_in_b = w_in[0].astype(bf16)
    w_bh_b = w_branch_hy[0].astype(bf16)
    w_bs_b = w_branch_sg[0].astype(bf16)
    w_out_b = w_out[0].astype(bf16)
    wg_b = w_gate[0].astype(bf16)
    wu_b = w_up[0].astype(bf16)
    wd_b = w_down[0].astype(bf16)
    g_mix = norm_mix_g[0].reshape(1, D_MODEL)
    sg_w_b = sg_w[0].astype(bf16)
    sg_b_t = sg_b[0].T

    m2, m2i = _stage2_tables()

    def mixer(x):
        B, L, D = x.shape
        n = B * L
        xt = x.reshape(n, D)
        p = in_proj(xt, g_mix, w_in_b)
        p3 = p.reshape(B, L, D_IN)
        tab_f, tab_i = _stage1_tables(L // N2)
        kf = hyena_filter(L, flt_w1[0], flt_b1[0], flt_w2[0], flt_b2[0], flt_w3[0], flt_b3[0],
                          flt_w4[0], flt_sin_freq[0], tab_f, m2)
        y_hy = hyena_mixer(p3, hy_conv_w[0], hy_conv_b[0], hy_d[0], kf,
                           (tab_f, tab_i, m2, m2i)).reshape(n, D_HYENA)
        y_sg = sgu(p, sg_ln_g, sg_ln_b, sg_w_b, sg_b_t)
        merged = branch_merge(y_hy, y_sg, w_bh_b, w_bs_b, p)
        return out_proj(merged, w_out_b, xt)

    g_moe = norm_moe_g[0].reshape(1, D_MODEL)
    g_fin = norm_final_g.reshape(1, D_MODEL)
    wr_t = jnp.pad(w_router[0].T, ((0, LANES - N_EXPERTS), (0, 0)))
    outs = []
    for x in (x_prompt, x_sample):
        y = moe_block(mixer(x), g_moe, wr_t, wg_b, wu_b, wd_b, g_fin)
        outs.append(y.reshape(x.shape))
    return tuple(outs)
```

```python
import functools
import math

import jax
import jax.numpy as jnp
from jax import lax
from jax.experimental import pallas as pl
from jax.experimental.pallas import tpu as pltpu

f32 = jnp.float32
bf16 = jnp.bfloat16
i32 = jnp.int32

D_MODEL = 2048
D_HYENA = D_MODEL // 2
HYENA_ORDER = 2
N_DIR = 2
FILTER_EMB = 33
DECAY_FAST_PCT = 0.3
DECAY_SLOW_PCT = 1.5
DECAY_TARGET = 1e-2
D_SGU = D_MODEL // 2
CHUNK = 128
SGU_GROUPS = 8
SGU_GROUP_DIM = D_SGU // SGU_GROUPS
N_EXPERTS = 16
EC_CAPACITY = 2
NORM_EPS = 1e-6
LN_EPS = 1e-5
OFF_SG = 3 * D_HYENA
OFF_GH = OFF_SG + 2 * D_SGU
OFF_GS = OFF_GH + D_MODEL
D_IN = OFF_GS + D_MODEL

VMEM_LIMIT_BYTES = 56 * 1024 * 1024


def _params(*sem):
    return pltpu.CompilerParams(dimension_semantics=sem, vmem_limit_bytes=VMEM_LIMIT_BYTES)


def _inproj_kernel(x_ref, g_ref, w_ref, o_ref, xn_ref):
    @pl.when(pl.program_id(1) == 0)
    def _():
        x = x_ref[...]
        ms = jnp.mean(x * x, axis=-1, keepdims=True)
        xn_ref[...] = (x * lax.rsqrt(ms + NORM_EPS) * g_ref[...]).astype(bf16)

    o_ref[...] = jnp.dot(xn_ref[...], w_ref[...], preferred_element_type=f32).astype(o_ref.dtype)


def in_proj(x, g, w, *, tm=1024, tn=1024):
    n, d = x.shape
    dn = w.shape[1]
    return pl.pallas_call(
        _inproj_kernel,
        grid=(n // tm, dn // tn),
        in_specs=[
            pl.BlockSpec((tm, d), lambda i, j: (i, 0)),
            pl.BlockSpec((1, d), lambda i, j: (0, 0)),
            pl.BlockSpec((d, tn), lambda i, j: (0, j)),
        ],
        out_specs=pl.BlockSpec((tm, tn), lambda i, j: (i, j)),
        out_shape=jax.ShapeDtypeStruct((n, dn), bf16),
        scratch_shapes=[pltpu.VMEM((tm, d), bf16)],
        compiler_params=_params("parallel", "arbitrary"),
        name="in_proj",
    )(x, g, w)


def _merge_kernel(yh_ref, ys_ref, wbh_ref, wbs_ref, gh_ref, gs_ref, o_ref):
    a = jnp.dot(yh_ref[...], wbh_ref[...], preferred_element_type=f32)
    b = jnp.dot(ys_ref[...], wbs_ref[...], preferred_element_type=f32)
    o = jax.nn.sigmoid(gh_ref[...].astype(f32)) * a + jax.nn.sigmoid(gs_ref[...].astype(f32)) * b
    o_ref[...] = o.astype(o_ref.dtype)


def branch_merge(y_hy, y_sg, w_bh, w_bs, p, *, tm=1024, tn=1024):
    n, dh = y_hy.shape
    d = w_bh.shape[1]
    gh0 = OFF_GH // tn
    gs0 = OFF_GS // tn
    return pl.pallas_call(
        _merge_kernel,
        grid=(n // tm, d // tn),
        in_specs=[
            pl.BlockSpec((tm, dh), lambda i, j: (i, 0)),
            pl.BlockSpec((tm, dh), lambda i, j: (i, 0)),
            pl.BlockSpec((dh, tn), lambda i, j: (0, j)),
            pl.BlockSpec((dh, tn), lambda i, j: (0, j)),
            pl.BlockSpec((tm, tn), lambda i, j: (i, gh0 + j)),
            pl.BlockSpec((tm, tn), lambda i, j: (i, gs0 + j)),
        ],
        out_specs=pl.BlockSpec((tm, tn), lambda i, j: (i, j)),
        out_shape=jax.ShapeDtypeStruct((n, d), bf16),
        compiler_params=_params("parallel", "arbitrary"),
        name="branch_merge",
    )(y_hy, y_sg, w_bh, w_bs, p, p)


def _outproj_kernel(m_ref, w_ref, x_ref, o_ref):
    o_ref[...] = x_ref[...] + jnp.dot(m_ref[...], w_ref[...], preferred_element_type=f32)


def out_proj(merged, w, x, *, tm=1024, tn=1024):
    n, d = merged.shape
    dn = w.shape[1]
    return pl.pallas_call(
        _outproj_kernel,
        grid=(n // tm, dn // tn),
        in_specs=[
            pl.BlockSpec((tm, d), lambda i, j: (i, 0)),
            pl.BlockSpec((d, tn), lambda i, j: (0, j)),
            pl.BlockSpec((tm, tn), lambda i, j: (i, j)),
        ],
        out_specs=pl.BlockSpec((tm, tn), lambda i, j: (i, j)),
        out_shape=jax.ShapeDtypeStruct((n, dn), f32),
        compiler_params=_params("parallel", "arbitrary"),
        name="out_proj",
    )(merged, w, x)


ROUTE_TILE = 256
SLOT_CHUNK = 48
LANES = 128
ROW_TILES = D_MODEL // LANES
BF16_ROWS = 16
READ_CHUNK = SLOT_CHUNK + BF16_ROWS


def _rms(x, g):
    return x * lax.rsqrt(jnp.mean(x * x, axis=-1, keepdims=True) + NORM_EPS) * g


def _router_kernel(x_ref, g_ref, wr_ref, o_ref):
    xn = _rms(x_ref[...], g_ref[...])
    logits = lax.dot_general(wr_ref[...], xn, (((1,), (1,)), ((), ())), precision=lax.Precision.HIGHEST,
                             preferred_element_type=f32)[:N_EXPERTS]
    m = jnp.max(logits, axis=0, keepdims=True)
    e = jnp.exp(logits - m)
    o_ref[...] = e / jnp.sum(e, axis=0, keepdims=True)


def router_probs(xm, g, wr_t, *, tm=512):
    n, d = xm.shape
    return pl.pallas_call(
        _router_kernel,
        grid=(n // tm,),
        in_specs=[pl.BlockSpec((tm, d), lambda i: (i, 0)), pl.BlockSpec((1, d), lambda i: (0, 0)),
                  pl.BlockSpec((LANES, d), lambda i: (0, 0))],
        out_specs=pl.BlockSpec((N_EXPERTS, tm), lambda i: (0, i)),
        out_shape=jax.ShapeDtypeStruct((N_EXPERTS, n), f32),
        compiler_params=_params("parallel"),
        name="router_probs",
    )(xm, g, wr_t)


def _select_kernel(p_ref, pos_ref, gate_ref, offs_ref, *, cap):
    p = p_ref[...]
    R = p.shape[0]
    bits = pltpu.bitcast(p, i32)

    def body(i, prefix):
        cand = prefix | lax.shift_left(jnp.int32(1), 30 - i)
        cnt = jnp.sum((bits >= cand).astype(i32))
        return jnp.where(cnt >= cap, cand, prefix)

    thr = lax.fori_loop(0, 31, body, jnp.int32(0))
    gt = bits > thr
    eq = bits == thr
    need = cap - jnp.sum(gt.astype(i32))

    li = lax.broadcasted_iota(i32, (LANES, LANES), 0)
    lj = lax.broadcasted_iota(i32, (LANES, LANES), 1)
    upper = (li <= lj).astype(bf16)
    ri = lax.broadcasted_iota(i32, (R, R), 0)
    rj = lax.broadcasted_iota(i32, (R, R), 1)
    rows_before = (rj < ri).astype(bf16)

    def prefix_counts(mask):
        m = mask.astype(bf16)
        incl = jnp.dot(m, upper, preferred_element_type=f32)
        tot = jnp.broadcast_to(incl[:, LANES - 1:LANES], (R, LANES)).astype(bf16)
        row_off = jnp.dot(rows_before, tot, preferred_element_type=f32)
        return incl - mask.astype(f32) + row_off, row_off

    eq_excl, _ = prefix_counts(eq)
    sel = gt | (eq & (eq_excl < need.astype(f32)))
    pos, row_off = prefix_counts(sel)
    pos_ref[...] = jnp.where(sel, pos.astype(i32), -1)
    gate_ref[...] = jnp.where(sel, p, 0.0)
    offs_ref[...] = row_off[:, 0:1].astype(i32)


def select_tokens(probs3, cap):
    E, R, _ = probs3.shape
    blk = pl.BlockSpec((None, R, LANES), lambda e: (e, 0, 0))
    return pl.pallas_call(
        functools.partial(_select_kernel, cap=cap),
        grid=(E,),
        in_specs=[blk],
        out_specs=[blk, blk, pl.BlockSpec((None, R, 1), lambda e: (e, 0, 0))],
        out_shape=[jax.ShapeDtypeStruct((E, R, LANES), i32), jax.ShapeDtypeStruct((E, R, LANES), f32),
                   jax.ShapeDtypeStruct((E, R, 1), i32)],
        compiler_params=_params("parallel"),
        name="select_tokens",
    )(probs3)


def _onehot_t(pos, start, width):
    j = lax.broadcasted_iota(i32, (width, pos.shape[1]), 0)
    return j == (pos - start)


def _dispatch_kernel(pos0_ref, x_ref, g_ref, pos_ref, gate_ref, xe_ref, ge_ref, xbuf_ref, gbuf_ref, sem_ref,
                     *, cap, n_tiles):
    i = pl.program_id(0)
    E, tm = pos_ref.shape
    slot = i % 2

    xn = _rms(x_ref[...], g_ref[...]).astype(bf16)
    eye = (lax.broadcasted_iota(i32, (E, LANES), 0) == lax.broadcasted_iota(i32, (E, LANES), 1)).astype(f32)
    gt = lax.dot_general(gate_ref[...], eye, (((0,), (0,)), ((), ())), precision=lax.Precision.HIGHEST,
                         preferred_element_type=f32)
    g_hi = gt.astype(bf16)
    g_lo = (gt - g_hi.astype(f32)).astype(bf16)
    pos = pos_ref[...]

    def base(e, tile):
        return pos0_ref[e * (n_tiles + 1) + tile]

    def count(e):
        return base(e, i + 1) - base(e, i)

    def build(c, dst):
        oh = jnp.concatenate([_onehot_t(pos[e:e + 1, :], base(e, i) + c * SLOT_CHUNK, SLOT_CHUNK)
                              for e in range(E)], axis=0)
        oh = jnp.where(oh, 1.0, 0.0).astype(bf16)
        xr = jnp.dot(oh, xn, preferred_element_type=f32).astype(bf16)
        tiles = jnp.stack([xr[:, s * LANES:(s + 1) * LANES] for s in range(ROW_TILES)], axis=0)
        xbuf_ref[dst] = pltpu.einshape("srl->rsl", tiles)
        gbuf_ref[dst, :, 0, :] = (jnp.dot(oh, g_hi, preferred_element_type=f32)
                                  + jnp.dot(oh, g_lo, preferred_element_type=f32))

    def copies(e, c, src, tile):
        rows = pl.ds(e * SLOT_CHUNK, SLOT_CHUNK)
        out_rows = pl.ds(base(e, tile) + c * SLOT_CHUNK, SLOT_CHUNK)
        return (pltpu.make_async_copy(xbuf_ref.at[src, rows], xe_ref.at[e, out_rows], sem_ref.at[0, src]),
                pltpu.make_async_copy(gbuf_ref.at[src, rows], ge_ref.at[e, out_rows], sem_ref.at[1, src]))

    @pl.when(i == 0)
    def _():
        xbuf_ref[2] = jnp.zeros(xbuf_ref.shape[1:], bf16)
        gbuf_ref[2] = jnp.zeros(gbuf_ref.shape[1:], f32)
        for e in range(E):
            rows = pl.ds(e * SLOT_CHUNK, SLOT_CHUNK)
            pad = pl.ds(cap, SLOT_CHUNK)
            for cp in (pltpu.make_async_copy(xbuf_ref.at[2, rows], xe_ref.at[e, pad], sem_ref.at[0, 2]),
                       pltpu.make_async_copy(gbuf_ref.at[2, rows], ge_ref.at[e, pad], sem_ref.at[1, 2])):
                cp.start()
                cp.wait()

    build(0, slot)

    @pl.when(i > 0)
    def _():
        for e in range(E):
            for cp in copies(e, 0, 1 - slot, i - 1):
                cp.wait()

    for e in range(E):
        for cp in copies(e, 0, slot, i):
            cp.start()

    nch = jnp.int32(0)
    for e in range(E):
        nch = jnp.maximum(nch, (count(e) + SLOT_CHUNK - 1) // SLOT_CHUNK)

    def overflow(c, carry):
        build(c, 2)
        for e in range(E):
            @pl.when(count(e) > c * SLOT_CHUNK)
            def _():
                for cp in copies(e, c, 2, i):
                    cp.start()
                    cp.wait()
        return carry

    lax.fori_loop(1, nch, overflow, 0)

    @pl.when(i == n_tiles - 1)
    def _():
        for e in range(E):
            for cp in copies(e, 0, slot, i):
                cp.wait()


def dispatch(xm, g, pos, gates, pos0, cap):
    n, d = xm.shape
    E = pos.shape[0]
    n_tiles = n // ROUTE_TILE
    rows = cap + SLOT_CHUNK
    grid_spec = pltpu.PrefetchScalarGridSpec(
        num_scalar_prefetch=1,
        grid=(n_tiles,),
        in_specs=[
            pl.BlockSpec((ROUTE_TILE, d), lambda i, p0: (i, 0)),
            pl.BlockSpec((1, d), lambda i, p0: (0, 0)),
            pl.BlockSpec((E, ROUTE_TILE), lambda i, p0: (0, i)),
            pl.BlockSpec((E, ROUTE_TILE), lambda i, p0: (0, i)),
        ],
        out_specs=[pl.BlockSpec(memory_space=pl.ANY), pl.BlockSpec(memory_space=pl.ANY)],
        scratch_shapes=[pltpu.VMEM((3, E * SLOT_CHUNK, ROW_TILES, LANES), bf16),
                        pltpu.VMEM((3, E * SLOT_CHUNK, 1, LANES), f32),
                        pltpu.SemaphoreType.DMA((2, 3))],
    )
    return pl.pallas_call(
        functools.partial(_dispatch_kernel, cap=cap, n_tiles=n_tiles),
        grid_spec=grid_spec,
        out_shape=[jax.ShapeDtypeStruct((E, rows, ROW_TILES, LANES), bf16),
                   jax.ShapeDtypeStruct((E, rows, 1, LANES), f32)],
        compiler_params=_params("arbitrary"),
        name="dispatch",
    )(pos0, xm, g, pos, gates)


def _expert_kernel(xe_ref, ge_ref, wg_ref, wu_ref, wd_ref, o_ref, xb_ref, acc_ref):
    e = pl.program_id(0)
    f = pl.program_id(2)

    @pl.when(f == 0)
    def _():
        tiles = pltpu.einshape("rsl->srl", xe_ref[...])
        for s in range(ROW_TILES):
            xb_ref[:, s * LANES:(s + 1) * LANES] = tiles[s]

    xb = xb_ref[...]
    hg = jnp.dot(xb, wg_ref[...], preferred_element_type=f32)
    hu = jnp.dot(xb, wu_ref[...], preferred_element_type=f32)
    h = (hg * jax.nn.sigmoid(hg) * hu).astype(bf16)
    part = jnp.dot(h, wd_ref[...], preferred_element_type=f32)

    @pl.when(f == 0)
    def _():
        acc_ref[...] = part

    @pl.when(f != 0)
    def _():
        acc_ref[...] += part

    @pl.when(f == pl.num_programs(2) - 1)
    def _():
        gl = ge_ref[:, 0, :]
        lane = lax.broadcasted_iota(i32, gl.shape, 1)
        gate = jnp.sum(jnp.where(lane == e, gl, 0.0), axis=-1, keepdims=True)
        o_ref[...] = (acc_ref[...] * gate).astype(o_ref.dtype)


def expert_ffn(xe, ge, wg, wu, wd, cap, *, tm=1024, tf=512):
    E = xe.shape[0]
    d = D_MODEL
    dff = wg.shape[2]
    tm = min(tm, cap)
    return pl.pallas_call(
        _expert_kernel,
        grid=(E, cap // tm, dff // tf),
        in_specs=[
            pl.BlockSpec((None, tm, ROW_TILES, LANES), lambda e_, i, f: (e_, i, 0, 0)),
            pl.BlockSpec((None, tm, 1, LANES), lambda e_, i, f: (e_, i, 0, 0)),
            pl.BlockSpec((None, d, tf), lambda e_, i, f: (e_, 0, f)),
            pl.BlockSpec((None, d, tf), lambda e_, i, f: (e_, 0, f)),
            pl.BlockSpec((None, tf, d), lambda e_, i, f: (e_, f, 0)),
        ],
        out_specs=pl.BlockSpec((None, tm, d), lambda e_, i, f: (e_, i, 0)),
        out_shape=jax.ShapeDtypeStruct((E, cap, d), bf16),
        scratch_shapes=[pltpu.VMEM((tm, d), bf16), pltpu.VMEM((tm, d), f32)],
        compiler_params=_params("parallel", "parallel", "arbitrary"),
        name="expert_ffn",
    )(xe, ge, wg, wu, wd)


def _combine_kernel(pos0_ref, x_ref, g_ref, pos_ref, ye_ref, o_ref, buf_ref, sem_ref, *, cap, n_tiles):
    i = pl.program_id(0)
    E, tm = pos_ref.shape
    slot = i % 2

    def base(e, tile):
        return pos0_ref[e * (n_tiles + 1) + tile]

    def count(e):
        return base(e, i + 1) - base(e, i)

    def start_row(e, c, tile):
        st = base(e, tile) + c * SLOT_CHUNK
        st = jnp.minimum((st // BF16_ROWS) * BF16_ROWS, cap - READ_CHUNK)
        return pl.multiple_of(st, BF16_ROWS)

    def copy(e, c, dst, tile):
        return pltpu.make_async_copy(ye_ref.at[e, pl.ds(start_row(e, c, tile), READ_CHUNK)],
                                     buf_ref.at[dst, pl.ds(e * READ_CHUNK, READ_CHUNK)], sem_ref.at[dst])

    @pl.when(i == 0)
    def _():
        for e in range(E):
            copy(e, 0, slot, i).start()

    @pl.when(i + 1 < n_tiles)
    def _():
        for e in range(E):
            copy(e, 0, 1 - slot, i + 1).start()

    pos = pos_ref[...]

    def onehot(c):
        parts = []
        for e in range(E):
            pe = pos[e:e + 1, :]
            local = pe - base(e, i)
            keep = (local >= c * SLOT_CHUNK) & (local < (c + 1) * SLOT_CHUNK) & (pe >= 0)
            oh = _onehot_t(pe, start_row(e, c, i), READ_CHUNK) & keep
            parts.append(jnp.where(oh, 1.0, 0.0).astype(bf16))
        return jnp.concatenate(parts, axis=0)

    for e in range(E):
        copy(e, 0, slot, i).wait()
    acc = lax.dot_general(onehot(0), buf_ref[slot], (((0,), (0,)), ((), ())), preferred_element_type=f32)

    nch = jnp.int32(0)
    for e in range(E):
        nch = jnp.maximum(nch, (count(e) + SLOT_CHUNK - 1) // SLOT_CHUNK)

    def overflow(c, acc):
        for e in range(E):
            cp = copy(e, c, 2, i)
            cp.start()
            cp.wait()
        return acc + lax.dot_general(onehot(c), buf_ref[2], (((0,), (0,)), ((), ())), preferred_element_type=f32)

    acc = lax.fori_loop(1, nch, overflow, acc)
    o_ref[...] = _rms(x_ref[...] + acc, g_ref[...])


def combine(xm, g_final, pos, ye, pos0, cap):
    n, d = xm.shape
    E = pos.shape[0]
    n_tiles = n // ROUTE_TILE
    grid_spec = pltpu.PrefetchScalarGridSpec(
        num_scalar_prefetch=1,
        grid=(n_tiles,),
        in_specs=[
            pl.BlockSpec((ROUTE_TILE, d), lambda i, p0: (i, 0)),
            pl.BlockSpec((1, d), lambda i, p0: (0, 0)),
            pl.BlockSpec((E, ROUTE_TILE), lambda i, p0: (0, i)),
            pl.BlockSpec(memory_space=pl.ANY),
        ],
        out_specs=pl.BlockSpec((ROUTE_TILE, d), lambda i, p0: (i, 0)),
        scratch_shapes=[pltpu.VMEM((3, E * READ_CHUNK, d), bf16), pltpu.SemaphoreType.DMA((3,))],
    )
    return pl.pallas_call(
        functools.partial(_combine_kernel, cap=cap, n_tiles=n_tiles),
        grid_spec=grid_spec,
        out_shape=jax.ShapeDtypeStruct((n, d), f32),
        compiler_params=_params("arbitrary"),
        name="combine",
    )(pos0, xm, g_final, pos, ye)


def moe_block(xm, g_moe, wr_t, wg, wu, wd, g_final):
    n = xm.shape[0]
    cap = EC_CAPACITY * n // N_EXPERTS
    n_tiles = n // ROUTE_TILE
    probs = router_probs(xm, g_moe, wr_t)
    pos3, gate3, offs = select_tokens(probs.reshape(N_EXPERTS, n // LANES, LANES), cap)
    pos = pos3.reshape(N_EXPERTS, n)
    gates = gate3.reshape(N_EXPERTS, n)
    tile_off = offs[:, ::ROUTE_TILE // LANES, 0]
    pos0 = jnp.concatenate([tile_off, jnp.full((N_EXPERTS, 1), cap, i32)], axis=1).reshape(-1)
    xe, ge = dispatch(xm, g_moe, pos, gates, pos0, cap)
    ye = expert_ffn(xe, ge, wg, wu, wd, cap)
    return combine(xm, g_final, pos, ye, pos0, cap)


N2 = 128


def _angles_stage1(H):
    n_total = 2 * N2 * H
    n2 = jnp.arange(N2, dtype=jnp.int32)[:, None, None]
    k1 = jnp.arange(H, dtype=jnp.int32)[None, :, None]
    n1 = jnp.arange(H, dtype=jnp.int32)[None, None, :]
    m = ((2 * k1 + 1) * (N2 * n1 + n2)) % (2 * n_total)
    return m.astype(f32) * (math.pi / n_total)


def _stage1_tables(H):
    th = _angles_stage1(H)
    c, s = jnp.cos(th), jnp.sin(th)
    fwd = jnp.concatenate([c, -s], axis=1).astype(bf16)
    scale = 1.0 / (N2 * H)
    ct, st = jnp.swapaxes(c, 1, 2), jnp.swapaxes(s, 1, 2)
    inv = (jnp.concatenate([ct, -st], axis=2) * scale).astype(bf16)
    return fwd, inv


def _stage2_tables():
    j = jnp.arange(N2, dtype=jnp.int32)
    th = ((j[:, None] * j[None, :]) % N2).astype(f32) * (2.0 * math.pi / N2)
    c, s = jnp.cos(th), jnp.sin(th)
    fwd = jnp.concatenate([jnp.concatenate([c, s], 1), jnp.concatenate([-s, c], 1)], 0)
    inv = jnp.concatenate([jnp.concatenate([c, -s], 1), jnp.concatenate([s, c], 1)], 0)
    return fwd.astype(bf16), inv.astype(bf16)


def _shortconv_kernel(x_ref, w_ref, b_ref, o_ref, s_ref, *, rows):
    L = x_ref.shape[0]
    zeros8 = jnp.zeros((8, x_ref.shape[1]), f32)
    s_ref[0:8, :] = zeros8
    s_ref[L + 8:L + 16, :] = zeros8

    def load(i, c):
        r0 = pl.multiple_of(i * rows, rows)
        s_ref[pl.ds(r0 + 8, rows), :] = x_ref[pl.ds(r0, rows), :].astype(f32)
        return c

    lax.fori_loop(0, L // rows, load, 0)
    w0, w1, w2, b = w_ref[0:1, :], w_ref[1:2, :], w_ref[2:3, :], b_ref[...]

    def conv(i, c):
        r0 = pl.multiple_of(i * rows, rows)
        prev = s_ref[pl.ds(r0 + 7, rows), :]
        cur = s_ref[pl.ds(r0 + 8, rows), :]
        nxt = s_ref[pl.ds(r0 + 9, rows), :]
        o_ref[pl.ds(r0, rows), :] = (prev * w0 + cur * w1 + nxt * w2 + b).astype(o_ref.dtype)
        return c

    lax.fori_loop(0, L // rows, conv, 0)


def short_conv(p3, w, b, col0, ncols, out_dtype, *, rows=256):
    B, L, _ = p3.shape
    cb = col0 // 128
    return pl.pallas_call(
        functools.partial(_shortconv_kernel, rows=rows),
        grid=(B, ncols // 128),
        in_specs=[
            pl.BlockSpec((None, L, 128), lambda b_, j: (b_, 0, cb + j)),
            pl.BlockSpec((3, 128), lambda b_, j: (0, cb + j)),
            pl.BlockSpec((1, 128), lambda b_, j: (0, cb + j)),
        ],
        out_specs=pl.BlockSpec((None, L, 128), lambda b_, j: (b_, 0, j)),
        out_shape=jax.ShapeDtypeStruct((B, L, ncols), out_dtype),
        scratch_shapes=[pltpu.VMEM((L + 16, 128), f32)],
        compiler_params=_params("parallel", "parallel"),
        name="short_conv",
    )(p3, w, b)


def _dft1_kernel(tab_ref, z_ref, o_ref, *, n2b, K):
    g = pl.program_id(2)
    for q in range(n2b):
        zs = z_ref[pl.ds(g * n2b + q, K, stride=N2), :].astype(bf16)
        o_ref[q] = jnp.dot(tab_ref[q], zs, preferred_element_type=f32).astype(o_ref.dtype)


def dft_stage1(z, tab, out_dtype, *, n2b=16):
    B, Lz, C = z.shape
    _, M, K = tab.shape
    assert Lz == K * N2
    return pl.pallas_call(
        functools.partial(_dft1_kernel, n2b=n2b, K=K),
        grid=(B, C // 128, N2 // n2b),
        in_specs=[
            pl.BlockSpec((n2b, M, K), lambda b_, c, g: (g, 0, 0)),
            pl.BlockSpec((None, Lz, 128), lambda b_, c, g: (b_, 0, c)),
        ],
        out_specs=pl.BlockSpec((None, n2b, M, 128), lambda b_, c, g: (b_, g, 0, c)),
        out_shape=jax.ShapeDtypeStruct((B, N2, M, C), out_dtype),
        compiler_params=_params("parallel", "parallel", "arbitrary"),
        name="dft_stage1",
    )(tab, z)


def _mid_kernel(m2_ref, m2i_ref, a_ref, kf_ref, o_ref, *, k1b):
    are = pltpu.einshape("nkc->knc", a_ref[:, 0])
    aim = pltpu.einshape("nkc->knc", a_ref[:, 1])
    bre, bim = [], []
    for q in range(k1b):
        x = jnp.dot(m2_ref[...], jnp.concatenate([are[q], aim[q]], axis=0), preferred_element_type=f32)
        xr, xi = x[:N2], x[N2:]
        kf = kf_ref[q]
        kr, ki = kf[:N2], kf[N2:]
        y = jnp.concatenate([xr * kr - xi * ki, xr * ki + xi * kr], axis=0).astype(bf16)
        bq = jnp.dot(m2i_ref[...], y, preferred_element_type=f32).astype(bf16)
        bre.append(bq[:N2])
        bim.append(bq[N2:])
    o_ref[:, 0] = pltpu.einshape("knc->nkc", jnp.stack(bre, axis=0))
    o_ref[:, 1] = pltpu.einshape("knc->nkc", jnp.stack(bim, axis=0))


def spectral_mid(a, kf, m2, m2i, order, *, k1b=16, cb=256):
    B, _, _, H, C = a.shape
    k1b = min(k1b, H)
    cpo = C // cb
    return pl.pallas_call(
        functools.partial(_mid_kernel, k1b=k1b),
        grid=(B, cpo, H // k1b),
        in_specs=[
            pl.BlockSpec((2 * N2, 2 * N2), lambda b_, c, k: (0, 0)),
            pl.BlockSpec((2 * N2, 2 * N2), lambda b_, c, k: (0, 0)),
            pl.BlockSpec((None, N2, 2, k1b, cb), lambda b_, c, k: (b_, 0, 0, k, c)),
            pl.BlockSpec((k1b, 2 * N2, cb), lambda b_, c, k: (k, 0, order * cpo + c)),
        ],
        out_specs=pl.BlockSpec((None, N2, 2, k1b, cb), lambda b_, c, k: (b_, 0, 0, k, c)),
        out_shape=jax.ShapeDtypeStruct(a.shape, bf16),
        compiler_params=_params("parallel", "parallel", "arbitrary"),
        name="spectral_mid",
    )(m2, m2i, a, kf)


def _idft1_gate_kernel(tab_ref, b_ref, x_ref, v_ref, d_ref, o_ref, *scratch, n2b, H, rows):
    g = pl.program_id(2)
    y_ref = scratch[0] if scratch else o_ref
    for q in range(n2b):
        y = jnp.dot(tab_ref[q], b_ref[q], preferred_element_type=f32)
        y_ref[pl.ds(g * n2b + q, H, stride=N2), :] = y

    @pl.when(g == pl.num_programs(2) - 1)
    def _():
        d = d_ref[...]

        def gate(i, c):
            r = pl.ds(pl.multiple_of(i * rows, rows), rows)
            o_ref[r, :] = (x_ref[r, :].astype(f32) * (y_ref[r, :] + v_ref[r, :] * d)).astype(o_ref.dtype)
            return c

        lax.fori_loop(0, (H * N2) // rows, gate, 0)


def idft1_gate(bm, tab, xg, xcol0, v, d, out_dtype, *, n2b=16, rows=256):
    B, _, M, C = bm.shape
    H = M // 2
    L = H * N2
    xb = xcol0 // 128
    inplace = out_dtype == f32
    return pl.pallas_call(
        functools.partial(_idft1_gate_kernel, n2b=n2b, H=H, rows=rows),
        grid=(B, C // 128, N2 // n2b),
        in_specs=[
            pl.BlockSpec((n2b, H, M), lambda b_, c, g: (g, 0, 0)),
            pl.BlockSpec((None, n2b, M, 128), lambda b_, c, g: (b_, g, 0, c)),
            pl.BlockSpec((None, L, 128), lambda b_, c, g: (b_, 0, xb + c)),
            pl.BlockSpec((None, L, 128), lambda b_, c, g: (b_, 0, c)),
            pl.BlockSpec((1, 128), lambda b_, c, g: (0, c)),
        ],
        out_specs=pl.BlockSpec((None, L, 128), lambda b_, c, g: (b_, 0, c)),
        out_shape=jax.ShapeDtypeStruct((B, L, C), out_dtype),
        scratch_shapes=[] if inplace else [pltpu.VMEM((L, 128), f32)],
        compiler_params=_params("parallel", "parallel", "arbitrary"),
        name="idft1_gate",
    )(tab, bm, xg, v, d)


def _filter_mlp_kernel(feat_ref, w1_ref, b1_ref, w2_ref, b2_ref, w3_ref, b3_ref, w4_ref, fr_ref, dl_ref,
                       p_ref, q_ref, nrm_ref):
    hp = lax.Precision.HIGHEST
    i = pl.program_id(0)
    feat = feat_ref[...]
    fr = fr_ref[...]
    h = jnp.sin(fr * (jnp.dot(feat, w1_ref[...], precision=hp, preferred_element_type=f32) + b1_ref[...]))
    h = jnp.sin(fr * (jnp.dot(h, w2_ref[...], precision=hp, preferred_element_type=f32) + b2_ref[...]))
    h = jnp.sin(fr * (jnp.dot(h, w3_ref[...], precision=hp, preferred_element_type=f32) + b3_ref[...]))
    win = jnp.exp(-feat[:, 0:1] * dl_ref[...])
    row = lax.broadcasted_iota(jnp.int32, win.shape, 0) + i * feat.shape[0]
    C = D_HYENA
    nrm = []
    for o in range(HYENA_ORDER):
        c0 = o * N_DIR * C
        fwd = jnp.dot(h, w4_ref[:, c0:c0 + C], precision=hp, preferred_element_type=f32) * win
        bwd = jnp.dot(h, w4_ref[:, c0 + C:c0 + 2 * C], precision=hp, preferred_element_type=f32) * win
        bwd = jnp.where(row == 0, 0.0, bwd)
        p_ref[:, o * C:(o + 1) * C] = fwd + bwd
        q_ref[:, o * C:(o + 1) * C] = fwd - bwd
        nrm.append(jnp.sum(jnp.abs(fwd) + jnp.abs(bwd), axis=0, keepdims=True))
    nrm = jnp.concatenate(nrm, axis=1)

    @pl.when(i == 0)
    def _():
        nrm_ref[...] = nrm

    @pl.when(i != 0)
    def _():
        nrm_ref[...] += nrm


def filter_mlp(feat, w1, b1, w2, b2, w3, b3, w4, freq, deltas, *, tm=256):
    L = feat.shape[0]
    CO = HYENA_ORDER * D_HYENA
    full = lambda a: pl.BlockSpec(a.shape, lambda i: (0,) * a.ndim)
    args = (w1, b1, w2, b2, w3, b3, w4, freq, deltas)
    return pl.pallas_call(
        _filter_mlp_kernel,
        grid=(L // tm,),
        in_specs=[pl.BlockSpec((tm, 128), lambda i: (i, 0))] + [full(a) for a in args],
        out_specs=[pl.BlockSpec((tm, CO), lambda i: (i, 0)), pl.BlockSpec((tm, CO), lambda i: (i, 0)),
                   pl.BlockSpec((1, CO), lambda i: (0, 0))],
        out_shape=[jax.ShapeDtypeStruct((L, CO), f32), jax.ShapeDtypeStruct((L, CO), f32),
                   jax.ShapeDtypeStruct((1, CO), f32)],
        compiler_params=_params("arbitrary"),
        name="filter_mlp",
    )(feat, *args)


def _filter_stage2_kernel(m2_ref, ap_ref, aq_ref, nrm_ref, o_ref, *, k1b):
    inv = 1.0 / nrm_ref[...]

    def split_dot(m, a):
        a_hi = a.astype(bf16)
        a_lo = (a - a_hi.astype(f32)).astype(bf16)
        return jnp.dot(m, a_hi, preferred_element_type=f32) + jnp.dot(m, a_lo, preferred_element_type=f32)

    pr, pi = pltpu.einshape("nkc->knc", ap_ref[:, 0]), pltpu.einshape("nkc->knc", ap_ref[:, 1])
    qr, qi = pltpu.einshape("nkc->knc", aq_ref[:, 0]), pltpu.einshape("nkc->knc", aq_ref[:, 1])
    for q in range(k1b):
        ap = jnp.concatenate([pr[q], pi[q]], axis=0)
        aq = jnp.concatenate([qr[q], qi[q]], axis=0)
        o_ref[q, 0:N2, :] = split_dot(m2_ref[0:N2, :], ap) * inv
        o_ref[q, N2:2 * N2, :] = split_dot(m2_ref[N2:2 * N2, :], aq) * inv


def filter_stage2(apq, m2, nrm, *, k1b=16):
    _, _, _, H, CO = apq.shape
    k1b = min(k1b, H)
    return pl.pallas_call(
        functools.partial(_filter_stage2_kernel, k1b=k1b),
        grid=(CO // 128, H // k1b),
        in_specs=[
            pl.BlockSpec((2 * N2, 2 * N2), lambda c, k: (0, 0)),
            pl.BlockSpec((None, N2, 2, k1b, 128), lambda c, k: (0, 0, 0, k, c)),
            pl.BlockSpec((None, N2, 2, k1b, 128), lambda c, k: (1, 0, 0, k, c)),
            pl.BlockSpec((1, 128), lambda c, k: (0, c)),
        ],
        out_specs=pl.BlockSpec((k1b, 2 * N2, 128), lambda c, k: (k, 0, c)),
        out_shape=jax.ShapeDtypeStruct((H, 2 * N2, CO), f32),
        compiler_params=_params("parallel", "arbitrary"),
        name="filter_stage2",
    )(m2, apq, apq, nrm)


def _pad2(a, rows, cols):
    return jnp.pad(a, ((0, rows - a.shape[0]), (0, cols - a.shape[1])))


def hyena_filter(L, w1, b1, w2, b2, w3, b3, w4, freq, tab_f, m2):
    pos = jnp.arange(L, dtype=f32)[:, None]
    t = jnp.linspace(0.0, 1.0, L, dtype=f32)[:, None]
    bands = (FILTER_EMB - 1) // 2
    z = jnp.linspace(1e-4, bands - 1, bands, dtype=f32)[None, :] * ((2.0 * math.pi / L) * pos)
    feat = _pad2(jnp.concatenate([t, jnp.cos(z), -jnp.sin(z)], axis=-1), L, 128)
    deltas = jnp.abs(jnp.linspace(math.log(DECAY_TARGET) / DECAY_SLOW_PCT,
                                  math.log(DECAY_TARGET) / DECAY_FAST_PCT, D_HYENA, dtype=f32))[None, :]
    row = lambda v: _pad2(v[None, :], 1, 128)
    p, q, nrm = filter_mlp(feat, _pad2(w1, 128, 128), row(b1), _pad2(w2, 128, 128), row(b2),
                           _pad2(w3, 128, 128), row(b3), _pad2(w4, 128, w4.shape[1]), row(freq), deltas)
    apq = dft_stage1(jnp.stack([p, q]), tab_f, f32)
    H = L // N2
    return filter_stage2(apq.reshape(2, N2, 2, H, HYENA_ORDER * D_HYENA), m2, nrm)


def hyena_mixer(p3, conv_w, conv_b, hy_d, kf, tabs):
    tab_f, tab_i, m2, m2i = tabs
    B, L, _ = p3.shape
    C = D_HYENA
    H = L // N2
    cb = conv_b[None, :]
    xg = short_conv(p3, conv_w, cb, 0, 2 * C, bf16)
    z = short_conv(p3, conv_w, cb, 2 * C, C, f32)
    for o in range(HYENA_ORDER):
        a = dft_stage1(z, tab_f, bf16).reshape(B, N2, 2, H, C)
        bm = spectral_mid(a, kf, m2, m2i, o).reshape(B, N2, 2 * H, C)
        last = o == HYENA_ORDER - 1
        z = idft1_gate(bm, tab_i, xg, o * C, z, hy_d[o:o + 1], bf16 if last else f32)
    return z


def _gelu(x):
    return 0.5 * x * (1.0 + lax.erf(x * (1.0 / math.sqrt(2.0))))


def _sgu_kernel(pu_ref, pv_ref, g_ref, b_ref, ws_ref, bs_ref, o_ref):
    v = _gelu(pv_ref[...].astype(f32))
    mu = jnp.mean(v, axis=-1, keepdims=True)
    vc = v - mu
    var = jnp.mean(vc * vc, axis=-1, keepdims=True)
    vn = (vc * lax.rsqrt(var + LN_EPS) * g_ref[...] + b_ref[...]).astype(bf16)
    for c in range(pv_ref.shape[0] // CHUNK):
        rows = slice(c * CHUNK, (c + 1) * CHUNK)
        for g in range(SGU_GROUPS):
            cols = slice(g * SGU_GROUP_DIM, (g + 1) * SGU_GROUP_DIM)
            s = jnp.dot(ws_ref[g], vn[rows, cols], preferred_element_type=f32) + bs_ref[:, g:g + 1]
            u = _gelu(pu_ref[rows, cols].astype(f32))
            o_ref[rows, cols] = (u * s).astype(o_ref.dtype)


def sgu(p, ln_g, ln_b, w_s, b_s_t, *, tm=512):
    n = p.shape[0]
    ub = OFF_SG // D_SGU
    return pl.pallas_call(
        _sgu_kernel,
        grid=(n // tm,),
        in_specs=[
            pl.BlockSpec((tm, D_SGU), lambda i: (i, ub)),
            pl.BlockSpec((tm, D_SGU), lambda i: (i, ub + 1)),
            pl.BlockSpec((1, D_SGU), lambda i: (0, 0)),
            pl.BlockSpec((1, D_SGU), lambda i: (0, 0)),
            pl.BlockSpec((SGU_GROUPS, CHUNK, CHUNK), lambda i: (0, 0, 0)),
            pl.BlockSpec((CHUNK, SGU_GROUPS), lambda i: (0, 0)),
        ],
        out_specs=pl.BlockSpec((tm, D_SGU), lambda i: (i, 0)),
        out_shape=jax.ShapeDtypeStruct((n, D_SGU), bf16),
        compiler_params=_params("parallel"),
        name="sgu",
    )(p, p, ln_g, ln_b, w_s, b_s_t)


def kernel(x_prompt, x_sample, norm_mix_g, w_in, hy_conv_w, hy_conv_b, flt_w1, flt_b1, flt_w2, flt_b2,
           flt_w3, flt_b3, flt_w4, flt_sin_freq, hy_d, sg_ln_g, sg_ln_b, sg_w, sg_b, w_branch_hy,
           w_branch_sg, w_out, norm_moe_g, w_router, w_gate, w_up, w_down, norm_final_g):
    w_in_b = w_in[0].astype(bf16)
    w_bh_b = w_branch_hy[0].astype(bf16)
    w_bs_b = w_branch_sg[0].astype(bf16)
    w_out_b = w_out[0].astype(bf16)
    wg_b = w_gate[0].astype(bf16)
    wu_b = w_up[0].astype(bf16)
    wd_b = w_down[0].astype(bf16)
    g_mix = norm_mix_g[0].reshape(1, D_MODEL)
    sg_w_b = sg_w[0].astype(bf16)
    sg_b_t = sg_b[0].T

    m2, m2i = _stage2_tables()

    def mixer(x):
        B, L, D = x.shape
        n = B * L
        xt = x.reshape(n, D)
        p = in_proj(xt, g_mix, w_in_b)
        p3 = p.reshape(B, L, D_IN)
        tab_f, tab_i = _stage1_tables(L // N2)
        kf = hyena_filter(L, flt_w1[0], flt_b1[0], flt_w2[0], flt_b2[0], flt_w3[0], flt_b3[0],
                          flt_w4[0], flt_sin_freq[0], tab_f, m2)
        y_hy = hyena_mixer(p3, hy_conv_w[0], hy_conv_b[0], hy_d[0], kf,
                           (tab_f, tab_i, m2, m2i)).reshape(n, D_HYENA)
        y_sg = sgu(p, sg_ln_g, sg_ln_b, sg_w_b, sg_b_t)
        merged = branch_merge(y_hy, y_sg, w_bh_b, w_bs_b, p)
        return out_proj(merged, w_out_b, xt)

    g_moe = norm_moe_g[0].reshape(1, D_MODEL)
    g_fin = norm_final_g.reshape(1, D_MODEL)
    wr_t = jnp.pad(w_router[0].T, ((0, LANES - N_EXPERTS), (0, 0)))
    outs = []
    for x in (x_prompt, x_sample):
        y = moe_block(mixer(x), g_moe, wr_t, wg_b, wu_b, wd_b, g_fin)
        outs.append(y.reshape(x.shape))
    return tuple(outs)
```

```python
import functools
import math

import jax
import jax.numpy as jnp
from jax import lax
from jax.experimental import pallas as pl
from jax.experimental.pallas import tpu as pltpu

f32 = jnp.float32
bf16 = jnp.bfloat16
i32 = jnp.int32

D_MODEL = 2048
D_HYENA = D_MODEL // 2
HYENA_ORDER = 2
N_DIR = 2
FILTER_EMB = 33
DECAY_FAST_PCT = 0.3
DECAY_SLOW_PCT = 1.5
DECAY_TARGET = 1e-2
D_SGU = D_MODEL // 2
CHUNK = 128
SGU_GROUPS = 8
SGU_GROUP_DIM = D_SGU // SGU_GROUPS
N_EXPERTS = 16
EC_CAPACITY = 2
NORM_EPS = 1e-6
LN_EPS = 1e-5
OFF_SG = 3 * D_HYENA
OFF_GH = OFF_SG + 2 * D_SGU
OFF_GS = OFF_GH + D_MODEL
D_IN = OFF_GS + D_MODEL

VMEM_LIMIT_BYTES = 56 * 1024 * 1024


def _params(*sem):
    return pltpu.CompilerParams(dimension_semantics=sem, vmem_limit_bytes=VMEM_LIMIT_BYTES)


def _inproj_kernel(x_ref, g_ref, w_ref, o_ref, xn_ref):
    @pl.when(pl.program_id(1) == 0)
    def _():
        x = x_ref[...]
        ms = jnp.mean(x * x, axis=-1, keepdims=True)
        xn_ref[...] = (x * lax.rsqrt(ms + NORM_EPS) * g_ref[...]).astype(bf16)

    o_ref[...] = jnp.dot(xn_ref[...], w_ref[...], preferred_element_type=f32).astype(o_ref.dtype)


def in_proj(x, g, w, *, tm=1024, tn=1024):
    n, d = x.shape
    dn = w.shape[1]
    return pl.pallas_call(
        _inproj_kernel,
        grid=(n // tm, dn // tn),
        in_specs=[
            pl.BlockSpec((tm, d), lambda i, j: (i, 0)),
            pl.BlockSpec((1, d), lambda i, j: (0, 0)),
            pl.BlockSpec((d, tn), lambda i, j: (0, j)),
        ],
        out_specs=pl.BlockSpec((tm, tn), lambda i, j: (i, j)),
        out_shape=jax.ShapeDtypeStruct((n, dn), bf16),
        scratch_shapes=[pltpu.VMEM((tm, d), bf16)],
        compiler_params=_params("parallel", "arbitrary"),
        name="in_proj",
    )(x, g, w)


def _merge_kernel(yh_ref, ys_ref, wbh_ref, wbs_ref, gh_ref, gs_ref, o_ref):
    a = jnp.dot(yh_ref[...], wbh_ref[...], preferred_element_type=f32)
    b = jnp.dot(ys_ref[...], wbs_ref[...], preferred_element_type=f32)
    o = jax.nn.sigmoid(gh_ref[...].astype(f32)) * a + jax.nn.sigmoid(gs_ref[...].astype(f32)) * b
    o_ref[...] = o.astype(o_ref.dtype)


def branch_merge(y_hy, y_sg, w_bh, w_bs, p, *, tm=1024, tn=1024):
    n, dh = y_hy.shape
    d = w_bh.shape[1]
    gh0 = OFF_GH // tn
    gs0 = OFF_GS // tn
    return pl.pallas_call(
        _merge_kernel,
        grid=(n // tm, d // tn),
        in_specs=[
            pl.BlockSpec((tm, dh), lambda i, j: (i, 0)),
            pl.BlockSpec((tm, dh), lambda i, j: (i, 0)),
            pl.BlockSpec((dh, tn), lambda i, j: (0, j)),
            pl.BlockSpec((dh, tn), lambda i, j: (0, j)),
            pl.BlockSpec((tm, tn), lambda i, j: (i, gh0 + j)),
            pl.BlockSpec((tm, tn), lambda i, j: (i, gs0 + j)),
        ],
        out_specs=pl.BlockSpec((tm, tn), lambda i, j: (i, j)),
        out_shape=jax.ShapeDtypeStruct((n, d), bf16),
        compiler_params=_params("parallel", "arbitrary"),
        name="branch_merge",
    )(y_hy, y_sg, w_bh, w_bs, p, p)


def _outproj_kernel(m_ref, w_ref, x_ref, o_ref):
    o_ref[...] = x_ref[...] + jnp.dot(m_ref[...], w_ref[...], preferred_element_type=f32)


def out_proj(merged, w, x, *, tm=1024, tn=1024):
    n, d = merged.shape
    dn = w.shape[1]
    return pl.pallas_call(
        _outproj_kernel,
        grid=(n // tm, dn // tn),
        in_specs=[
            pl.BlockSpec((tm, d), lambda i, j: (i, 0)),
            pl.BlockSpec((d, tn), lambda i, j: (0, j)),
            pl.BlockSpec((tm, tn), lambda i, j: (i, j)),
        ],
        out_specs=pl.BlockSpec((tm, tn), lambda i, j: (i, j)),
        out_shape=jax.ShapeDtypeStruct((n, dn), f32),
        compiler_params=_params("parallel", "arbitrary"),
        name="out_proj",
    )(merged, w, x)


ROUTE_TILE = 256
SLOT_CHUNK = 48
LANES = 128
ROW_TILES = D_MODEL // LANES
BF16_ROWS = 16
READ_CHUNK = SLOT_CHUNK + BF16_ROWS


def _rms(x, g):
    return x * lax.rsqrt(jnp.mean(x * x, axis=-1, keepdims=True) + NORM_EPS) * g


def _router_kernel(x_ref, g_ref, wr_ref, o_ref):
    xn = _rms(x_ref[...], g_ref[...])
    logits = lax.dot_general(wr_ref[...], xn, (((1,), (1,)), ((), ())), precision=lax.Precision.HIGHEST,
                             preferred_element_type=f32)
    m = jnp.max(logits, axis=0, keepdims=True)
    e = jnp.exp(logits - m)
    o_ref[...] = e / jnp.sum(e, axis=0, keepdims=True)


def router_probs(xm, g, wr_t, *, tm=512):
    n, d = xm.shape
    return pl.pallas_call(
        _router_kernel,
        grid=(n // tm,),
        in_specs=[pl.BlockSpec((tm, d), lambda i: (i, 0)), pl.BlockSpec((1, d), lambda i: (0, 0)),
                  pl.BlockSpec((N_EXPERTS, d), lambda i: (0, 0))],
        out_specs=pl.BlockSpec((N_EXPERTS, tm), lambda i: (0, i)),
        out_shape=jax.ShapeDtypeStruct((N_EXPERTS, n), f32),
        compiler_params=_params("parallel"),
        name="router_probs",
    )(xm, g, wr_t)


def _select_kernel(p_ref, pos_ref, gate_ref, offs_ref, *, cap):
    p = p_ref[...]
    R = p.shape[0]
    bits = pltpu.bitcast(p, i32)

    def body(i, prefix):
        cand = prefix | lax.shift_left(jnp.int32(1), 30 - i)
        cnt = jnp.sum((bits >= cand).astype(i32))
        return jnp.where(cnt >= cap, cand, prefix)

    thr = lax.fori_loop(0, 31, body, jnp.int32(0))
    gt = bits > thr
    eq = bits == thr
    need = cap - jnp.sum(gt.astype(i32))

    li = lax.broadcasted_iota(i32, (LANES, LANES), 0)
    lj = lax.broadcasted_iota(i32, (LANES, LANES), 1)
    upper = (li <= lj).astype(bf16)
    ri = lax.broadcasted_iota(i32, (R, R), 0)
    rj = lax.broadcasted_iota(i32, (R, R), 1)
    rows_before = (rj < ri).astype(bf16)

    def prefix_counts(mask):
        m = mask.astype(bf16)
        incl = jnp.dot(m, upper, preferred_element_type=f32)
        tot = jnp.broadcast_to(incl[:, LANES - 1:LANES], (R, LANES)).astype(bf16)
        row_off = jnp.dot(rows_before, tot, preferred_element_type=f32)
        return incl - mask.astype(f32) + row_off, row_off

    eq_excl, _ = prefix_counts(eq)
    sel = gt | (eq & (eq_excl < need.astype(f32)))
    pos, row_off = prefix_counts(sel)
    pos_ref[...] = jnp.where(sel, pos.astype(i32), -1)
    gate_ref[...] = jnp.where(sel, p, 0.0)
    offs_ref[...] = row_off[:, 0:1].astype(i32)


def select_tokens(probs3, cap):
    E, R, _ = probs3.shape
    blk = pl.BlockSpec((None, R, LANES), lambda e: (e, 0, 0))
    return pl.pallas_call(
        functools.partial(_select_kernel, cap=cap),
        grid=(E,),
        in_specs=[blk],
        out_specs=[blk, blk, pl.BlockSpec((None, R, 1), lambda e: (e, 0, 0))],
        out_shape=[jax.ShapeDtypeStruct((E, R, LANES), i32), jax.ShapeDtypeStruct((E, R, LANES), f32),
                   jax.ShapeDtypeStruct((E, R, 1), i32)],
        compiler_params=_params("parallel"),
        name="select_tokens",
    )(probs3)


def _onehot_t(pos, start, width):
    j = lax.broadcasted_iota(i32, (width, pos.shape[1]), 0)
    return j == (pos - start)


def _dispatch_kernel(pos0_ref, x_ref, g_ref, pos_ref, gate_ref, xe_ref, ge_ref, xbuf_ref, gbuf_ref, sem_ref,
                     *, cap, n_tiles):
    i = pl.program_id(0)
    E, tm = pos_ref.shape
    slot = i % 2

    xn = _rms(x_ref[...], g_ref[...]).astype(bf16)
    eye = (lax.broadcasted_iota(i32, (E, LANES), 0) == lax.broadcasted_iota(i32, (E, LANES), 1)).astype(f32)
    gt = lax.dot_general(gate_ref[...], eye, (((0,), (0,)), ((), ())), precision=lax.Precision.HIGHEST,
                         preferred_element_type=f32)
    g_hi = gt.astype(bf16)
    g_lo = (gt - g_hi.astype(f32)).astype(bf16)
    pos = pos_ref[...]

    def base(e, tile):
        return pos0_ref[e * (n_tiles + 1) + tile]

    def count(e):
        return base(e, i + 1) - base(e, i)

    def build(c, dst):
        oh = jnp.concatenate([_onehot_t(pos[e:e + 1, :], base(e, i) + c * SLOT_CHUNK, SLOT_CHUNK)
                              for e in range(E)], axis=0)
        oh = jnp.where(oh, 1.0, 0.0).astype(bf16)
        xr = jnp.dot(oh, xn, preferred_element_type=f32).astype(bf16)
        tiles = jnp.stack([xr[:, s * LANES:(s + 1) * LANES] for s in range(ROW_TILES)], axis=0)
        xbuf_ref[dst] = pltpu.einshape("srl->rsl", tiles)
        gbuf_ref[dst, :, 0, :] = (jnp.dot(oh, g_hi, preferred_element_type=f32)
                                  + jnp.dot(oh, g_lo, preferred_element_type=f32))

    def copies(e, c, src, tile):
        rows = pl.ds(e * SLOT_CHUNK, SLOT_CHUNK)
        out_rows = pl.ds(base(e, tile) + c * SLOT_CHUNK, SLOT_CHUNK)
        return (pltpu.make_async_copy(xbuf_ref.at[src, rows], xe_ref.at[e, out_rows], sem_ref.at[0, src]),
                pltpu.make_async_copy(gbuf_ref.at[src, rows], ge_ref.at[e, out_rows], sem_ref.at[1, src]))

    @pl.when(i == 0)
    def _():
        xbuf_ref[2] = jnp.zeros(xbuf_ref.shape[1:], bf16)
        gbuf_ref[2] = jnp.zeros(gbuf_ref.shape[1:], f32)
        for e in range(E):
            rows = pl.ds(e * SLOT_CHUNK, SLOT_CHUNK)
            pad = pl.ds(cap, SLOT_CHUNK)
            for cp in (pltpu.make_async_copy(xbuf_ref.at[2, rows], xe_ref.at[e, pad], sem_ref.at[0, 2]),
                       pltpu.make_async_copy(gbuf_ref.at[2, rows], ge_ref.at[e, pad], sem_ref.at[1, 2])):
                cp.start()
                cp.wait()

    build(0, slot)

    @pl.when(i > 0)
    def _():
        for e in range(E):
            for cp in copies(e, 0, 1 - slot, i - 1):
                cp.wait()

    for e in range(E):
        for cp in copies(e, 0, slot, i):
            cp.start()

    nch = jnp.int32(0)
    for e in range(E):
        nch = jnp.maximum(nch, (count(e) + SLOT_CHUNK - 1) // SLOT_CHUNK)

    def overflow(c, carry):
        build(c, 2)
        for e in range(E):
            @pl.when(count(e) > c * SLOT_CHUNK)
            def _():
                for cp in copies(e, c, 2, i):
                    cp.start()
                    cp.wait()
        return carry

    lax.fori_loop(1, nch, overflow, 0)

    @pl.when(i == n_tiles - 1)
    def _():
        for e in range(E):
            for cp in copies(e, 0, slot, i):
                cp.wait()


def dispatch(xm, g, pos, gates, pos0, cap):
    n, d = xm.shape
    E = pos.shape[0]
    n_tiles = n // ROUTE_TILE
    rows = cap + SLOT_CHUNK
    grid_spec = pltpu.PrefetchScalarGridSpec(
        num_scalar_prefetch=1,
        grid=(n_tiles,),
        in_specs=[
            pl.BlockSpec((ROUTE_TILE, d), lambda i, p0: (i, 0)),
            pl.BlockSpec((1, d), lambda i, p0: (0, 0)),
            pl.BlockSpec((E, ROUTE_TILE), lambda i, p0: (0, i)),
            pl.BlockSpec((E, ROUTE_TILE), lambda i, p0: (0, i)),
        ],
        out_specs=[pl.BlockSpec(memory_space=pl.ANY), pl.BlockSpec(memory_space=pl.ANY)],
        scratch_shapes=[pltpu.VMEM((3, E * SLOT_CHUNK, ROW_TILES, LANES), bf16),
                        pltpu.VMEM((3, E * SLOT_CHUNK, 1, LANES), f32),
                        pltpu.SemaphoreType.DMA((2, 3))],
    )
    return pl.pallas_call(
        functools.partial(_dispatch_kernel, cap=cap, n_tiles=n_tiles),
        grid_spec=grid_spec,
        out_shape=[jax.ShapeDtypeStruct((E, rows, ROW_TILES, LANES), bf16),
                   jax.ShapeDtypeStruct((E, rows, 1, LANES), f32)],
        compiler_params=_params("arbitrary"),
        name="dispatch",
    )(pos0, xm, g, pos, gates)


def _expert_kernel(xe_ref, ge_ref, wg_ref, wu_ref, wd_ref, o_ref, xb_ref, acc_ref):
    e = pl.program_id(0)
    f = pl.program_id(2)

    @pl.when(f == 0)
    def _():
        tiles = pltpu.einshape("rsl->srl", xe_ref[...])
        for s in range(ROW_TILES):
            xb_ref[:, s * LANES:(s + 1) * LANES] = tiles[s]

    xb = xb_ref[...]
    hg = jnp.dot(xb, wg_ref[...], preferred_element_type=f32)
    hu = jnp.dot(xb, wu_ref[...], preferred_element_type=f32)
    h = (hg * jax.nn.sigmoid(hg) * hu).astype(bf16)
    part = jnp.dot(h, wd_ref[...], preferred_element_type=f32)

    @pl.when(f == 0)
    def _():
        acc_ref[...] = part

    @pl.when(f != 0)
    def _():
        acc_ref[...] += part

    @pl.when(f == pl.num_programs(2) - 1)
    def _():
        gl = ge_ref[:, 0, :]
        lane = lax.broadcasted_iota(i32, gl.shape, 1)
        gate = jnp.sum(jnp.where(lane == e, gl, 0.0), axis=-1, keepdims=True)
        o_ref[...] = (acc_ref[...] * gate).astype(o_ref.dtype)


def expert_ffn(xe, ge, wg, wu, wd, cap, *, tm=1024, tf=512):
    E = xe.shape[0]
    d = D_MODEL
    dff = wg.shape[2]
    tm = min(tm, cap)
    return pl.pallas_call(
        _expert_kernel,
        grid=(E, cap // tm, dff // tf),
        in_specs=[
            pl.BlockSpec((None, tm, ROW_TILES, LANES), lambda e_, i, f: (e_, i, 0, 0)),
            pl.BlockSpec((None, tm, 1, LANES), lambda e_, i, f: (e_, i, 0, 0)),
            pl.BlockSpec((None, d, tf), lambda e_, i, f: (e_, 0, f)),
            pl.BlockSpec((None, d, tf), lambda e_, i, f: (e_, 0, f)),
            pl.BlockSpec((None, tf, d), lambda e_, i, f: (e_, f, 0)),
        ],
        out_specs=pl.BlockSpec((None, tm, d), lambda e_, i, f: (e_, i, 0)),
        out_shape=jax.ShapeDtypeStruct((E, cap, d), bf16),
        scratch_shapes=[pltpu.VMEM((tm, d), bf16), pltpu.VMEM((tm, d), f32)],
        compiler_params=_params("parallel", "parallel", "arbitrary"),
        name="expert_ffn",
    )(xe, ge, wg, wu, wd)


def _combine_kernel(pos0_ref, x_ref, g_ref, pos_ref, ye_ref, o_ref, buf_ref, sem_ref, *, cap, n_tiles):
    i = pl.program_id(0)
    E, tm = pos_ref.shape
    slot = i % 2

    def base(e, tile):
        return pos0_ref[e * (n_tiles + 1) + tile]

    def count(e):
        return base(e, i + 1) - base(e, i)

    def start_row(e, c, tile):
        st = base(e, tile) + c * SLOT_CHUNK
        st = jnp.minimum((st // BF16_ROWS) * BF16_ROWS, cap - READ_CHUNK)
        return pl.multiple_of(st, BF16_ROWS)

    def copy(e, c, dst, tile):
        return pltpu.make_async_copy(ye_ref.at[e, pl.ds(start_row(e, c, tile), READ_CHUNK)],
                                     buf_ref.at[dst, pl.ds(e * READ_CHUNK, READ_CHUNK)], sem_ref.at[dst])

    @pl.when(i == 0)
    def _():
        for e in range(E):
            copy(e, 0, slot, i).start()

    @pl.when(i + 1 < n_tiles)
    def _():
        for e in range(E):
            copy(e, 0, 1 - slot, i + 1).start()

    pos = pos_ref[...]

    def onehot(c):
        parts = []
        for e in range(E):
            pe = pos[e:e + 1, :]
            local = pe - base(e, i)
            keep = (local >= c * SLOT_CHUNK) & (local < (c + 1) * SLOT_CHUNK) & (pe >= 0)
            oh = _onehot_t(pe, start_row(e, c, i), READ_CHUNK) & keep
            parts.append(jnp.where(oh, 1.0, 0.0).astype(bf16))
        return jnp.concatenate(parts, axis=0)

    for e in range(E):
        copy(e, 0, slot, i).wait()
    acc = lax.dot_general(onehot(0), buf_ref[slot], (((0,), (0,)), ((), ())), preferred_element_type=f32)

    nch = jnp.int32(0)
    for e in range(E):
        nch = jnp.maximum(nch, (count(e) + SLOT_CHUNK - 1) // SLOT_CHUNK)

    def overflow(c, acc):
        for e in range(E):
            cp = copy(e, c, 2, i)
            cp.start()
            cp.wait()
        return acc + lax.dot_general(onehot(c), buf_ref[2], (((0,), (0,)), ((), ())), preferred_element_type=f32)

    acc = lax.fori_loop(1, nch, overflow, acc)
    o_ref[...] = _rms(x_ref[...] + acc, g_ref[...])


def combine(xm, g_final, pos, ye, pos0, cap):
    n, d = xm.shape
    E = pos.shape[0]
    n_tiles = n // ROUTE_TILE
    grid_spec = pltpu.PrefetchScalarGridSpec(
        num_scalar_prefetch=1,
        grid=(n_tiles,),
        in_specs=[
            pl.BlockSpec((ROUTE_TILE, d), lambda i, p0: (i, 0)),
            pl.BlockSpec((1, d), lambda i, p0: (0, 0)),
            pl.BlockSpec((E, ROUTE_TILE), lambda i, p0: (0, i)),
            pl.BlockSpec(memory_space=pl.ANY),
        ],
        out_specs=pl.BlockSpec((ROUTE_TILE, d), lambda i, p0: (i, 0)),
        scratch_shapes=[pltpu.VMEM((3, E * READ_CHUNK, d), bf16), pltpu.SemaphoreType.DMA((3,))],
    )
    return pl.pallas_call(
        functools.partial(_combine_kernel, cap=cap, n_tiles=n_tiles),
        grid_spec=grid_spec,
        out_shape=jax.ShapeDtypeStruct((n, d), f32),
        compiler_params=_params("arbitrary"),
        name="combine",
    )(pos0, xm, g_final, pos, ye)


def moe_block(xm, g_moe, wr_t, wg, wu, wd, g_final):
    n = xm.shape[0]
    cap = EC_CAPACITY * n // N_EXPERTS
    n_tiles = n // ROUTE_TILE
    probs = router_probs(xm, g_moe, wr_t)
    pos3, gate3, offs = select_tokens(probs.reshape(N_EXPERTS, n // LANES, LANES), cap)
    pos = pos3.reshape(N_EXPERTS, n)
    gates = gate3.reshape(N_EXPERTS, n)
    tile_off = offs[:, ::ROUTE_TILE // LANES, 0]
    pos0 = jnp.concatenate([tile_off, jnp.full((N_EXPERTS, 1), cap, i32)], axis=1).reshape(-1)
    xe, ge = dispatch(xm, g_moe, pos, gates, pos0, cap)
    ye = expert_ffn(xe, ge, wg, wu, wd, cap)
    return combine(xm, g_final, pos, ye, pos0, cap)


N2 = 128


def _angles_stage1(H):
    n_total = 2 * N2 * H
    n2 = jnp.arange(N2, dtype=jnp.int32)[:, None, None]
    k1 = jnp.arange(H, dtype=jnp.int32)[None, :, None]
    n1 = jnp.arange(H, dtype=jnp.int32)[None, None, :]
    m = ((2 * k1 + 1) * (N2 * n1 + n2)) % (2 * n_total)
    return m.astype(f32) * (math.pi / n_total)


def _stage1_tables(H):
    th = _angles_stage1(H)
    c, s = jnp.cos(th), jnp.sin(th)
    fwd = jnp.concatenate([c, -s], axis=1).astype(bf16)
    scale = 1.0 / (N2 * H)
    ct, st = jnp.swapaxes(c, 1, 2), jnp.swapaxes(s, 1, 2)
    inv = (jnp.concatenate([ct, -st], axis=2) * scale).astype(bf16)
    return fwd, inv


def _stage2_tables():
    j = jnp.arange(N2, dtype=jnp.int32)
    th = ((j[:, None] * j[None, :]) % N2).astype(f32) * (2.0 * math.pi / N2)
    c, s = jnp.cos(th), jnp.sin(th)
    fwd = jnp.concatenate([jnp.concatenate([c, s], 1), jnp.concatenate([-s, c], 1)], 0)
    inv = jnp.concatenate([jnp.concatenate([c, -s], 1), jnp.concatenate([s, c], 1)], 0)
    return fwd.astype(bf16), inv.astype(bf16)


def _shortconv_kernel(x_ref, w_ref, b_ref, o_ref, s_ref, *, rows):
    L = x_ref.shape[0]
    zeros8 = jnp.zeros((8, x_ref.shape[1]), f32)
    s_ref[0:8, :] = zeros8
    s_ref[L + 8:L + 16, :] = zeros8

    def load(i, c):
        r0 = pl.multiple_of(i * rows, rows)
        s_ref[pl.ds(r0 + 8, rows), :] = x_ref[pl.ds(r0, rows), :].astype(f32)
        return c

    lax.fori_loop(0, L // rows, load, 0)
    w0, w1, w2, b = w_ref[0:1, :], w_ref[1:2, :], w_ref[2:3, :], b_ref[...]

    def conv(i, c):
        r0 = pl.multiple_of(i * rows, rows)
        prev = s_ref[pl.ds(r0 + 7, rows), :]
        cur = s_ref[pl.ds(r0 + 8, rows), :]
        nxt = s_ref[pl.ds(r0 + 9, rows), :]
        o_ref[pl.ds(r0, rows), :] = (prev * w0 + cur * w1 + nxt * w2 + b).astype(o_ref.dtype)
        return c

    lax.fori_loop(0, L // rows, conv, 0)


def short_conv(p3, w, b, col0, ncols, out_dtype, *, rows=256):
    B, L, _ = p3.shape
    cb = col0 // 128
    return pl.pallas_call(
        functools.partial(_shortconv_kernel, rows=rows),
        grid=(B, ncols // 128),
        in_specs=[
            pl.BlockSpec((None, L, 128), lambda b_, j: (b_, 0, cb + j)),
            pl.BlockSpec((3, 128), lambda b_, j: (0, cb + j)),
            pl.BlockSpec((1, 128), lambda b_, j: (0, cb + j)),
        ],
        out_specs=pl.BlockSpec((None, L, 128), lambda b_, j: (b_, 0, j)),
        out_shape=jax.ShapeDtypeStruct((B, L, ncols), out_dtype),
        scratch_shapes=[pltpu.VMEM((L + 16, 128), f32)],
        compiler_params=_params("parallel", "parallel"),
        name="short_conv",
    )(p3, w, b)


def _dft1_kernel(tab_ref, z_ref, o_ref, *, n2b, K):
    g = pl.program_id(2)
    rows = [z_ref[pl.ds(pl.multiple_of(n1 * N2 + g * n2b, n2b), n2b), :] for n1 in range(K)]
    zs = pltpu.einshape("kqc->qkc", jnp.stack(rows, axis=0)).astype(bf16)
    for q in range(n2b):
        o_ref[q] = jnp.dot(tab_ref[q], zs[q], preferred_element_type=f32).astype(o_ref.dtype)


def dft_stage1(z, tab, out_dtype, *, n2b=16):
    B, Lz, C = z.shape
    _, M, K = tab.shape
    assert Lz == K * N2
    return pl.pallas_call(
        functools.partial(_dft1_kernel, n2b=n2b, K=K),
        grid=(B, C // 128, N2 // n2b),
        in_specs=[
            pl.BlockSpec((n2b, M, K), lambda b_, c, g: (g, 0, 0)),
            pl.BlockSpec((None, Lz, 128), lambda b_, c, g: (b_, 0, c)),
        ],
        out_specs=pl.BlockSpec((None, n2b, M, 128), lambda b_, c, g: (b_, g, 0, c)),
        out_shape=jax.ShapeDtypeStruct((B, N2, M, C), out_dtype),
        compiler_params=_params("parallel", "parallel", "arbitrary"),
        name="dft_stage1",
    )(tab, z)


def _mid_kernel(m2_ref, m2i_ref, a_ref, kf_ref, o_ref, *, k1b):
    are = pltpu.einshape("nkc->knc", a_ref[:, 0])
    aim = pltpu.einshape("nkc->knc", a_ref[:, 1])
    bre, bim = [], []
    for q in range(k1b):
        x = jnp.dot(m2_ref[...], jnp.concatenate([are[q], aim[q]], axis=0), preferred_element_type=f32)
        xr, xi = x[:N2], x[N2:]
        kf = kf_ref[q]
        kr, ki = kf[:N2], kf[N2:]
        y = jnp.concatenate([xr * kr - xi * ki, xr * ki + xi * kr], axis=0).astype(bf16)
        bq = jnp.dot(m2i_ref[...], y, preferred_element_type=f32).astype(bf16)
        bre.append(bq[:N2])
        bim.append(bq[N2:])
    o_ref[:, 0] = pltpu.einshape("knc->nkc", jnp.stack(bre, axis=0))
    o_ref[:, 1] = pltpu.einshape("knc->nkc", jnp.stack(bim, axis=0))


def spectral_mid(a, kf, m2, m2i, order, *, k1b=16, cb=256):
    B, _, _, H, C = a.shape
    k1b = min(k1b, H)
    cpo = C // cb
    return pl.pallas_call(
        functools.partial(_mid_kernel, k1b=k1b),
        grid=(B, cpo, H // k1b),
        in_specs=[
            pl.BlockSpec((2 * N2, 2 * N2), lambda b_, c, k: (0, 0)),
            pl.BlockSpec((2 * N2, 2 * N2), lambda b_, c, k: (0, 0)),
            pl.BlockSpec((None, N2, 2, k1b, cb), lambda b_, c, k: (b_, 0, 0, k, c)),
            pl.BlockSpec((k1b, 2 * N2, cb), lambda b_, c, k: (k, 0, order * cpo + c)),
        ],
        out_specs=pl.BlockSpec((None, N2, 2, k1b, cb), lambda b_, c, k: (b_, 0, 0, k, c)),
        out_shape=jax.ShapeDtypeStruct(a.shape, bf16),
        compiler_params=_params("parallel", "parallel", "arbitrary"),
        name="spectral_mid",
    )(m2, m2i, a, kf)


def _idft1_gate_kernel(tab_ref, b_ref, x_ref, v_ref, d_ref, o_ref, *scratch, n2b, H, rows):
    g = pl.program_id(2)
    y_ref = scratch[0] if scratch else o_ref
    ys = [jnp.dot(tab_ref[q], b_ref[q], preferred_element_type=f32) for q in range(n2b)]
    yt = pltpu.einshape("qhc->hqc", jnp.stack(ys, axis=0))
    for n1 in range(H):
        y_ref[pl.ds(pl.multiple_of(n1 * N2 + g * n2b, n2b), n2b), :] = yt[n1]

    @pl.when(g == pl.num_programs(2) - 1)
    def _():
        d = d_ref[...]

        def gate(i, c):
            r = pl.ds(pl.multiple_of(i * rows, rows), rows)
            o_ref[r, :] = (x_ref[r, :].astype(f32) * (y_ref[r, :] + v_ref[r, :] * d)).astype(o_ref.dtype)
            return c

        lax.fori_loop(0, (H * N2) // rows, gate, 0)


def idft1_gate(bm, tab, xg, xcol0, v, d, out_dtype, *, n2b=16, rows=256):
    B, _, M, C = bm.shape
    H = M // 2
    L = H * N2
    xb = xcol0 // 128
    inplace = out_dtype == f32
    return pl.pallas_call(
        functools.partial(_idft1_gate_kernel, n2b=n2b, H=H, rows=rows),
        grid=(B, C // 128, N2 // n2b),
        in_specs=[
            pl.BlockSpec((n2b, H, M), lambda b_, c, g: (g, 0, 0)),
            pl.BlockSpec((None, n2b, M, 128), lambda b_, c, g: (b_, g, 0, c)),
            pl.BlockSpec((None, L, 128), lambda b_, c, g: (b_, 0, xb + c)),
            pl.BlockSpec((None, L, 128), lambda b_, c, g: (b_, 0, c)),
            pl.BlockSpec((1, 128), lambda b_, c, g: (0, c)),
        ],
        out_specs=pl.BlockSpec((None, L, 128), lambda b_, c, g: (b_, 0, c)),
        out_shape=jax.ShapeDtypeStruct((B, L, C), out_dtype),
        scratch_shapes=[] if inplace else [pltpu.VMEM((L, 128), f32)],
        compiler_params=_params("parallel", "parallel", "arbitrary"),
        name="idft1_gate",
    )(tab, bm, xg, v, d)


def _filter_mlp_kernel(feat_ref, w1_ref, b1_ref, w2_ref, b2_ref, w3_ref, b3_ref, w4a_ref, w4b_ref, fr_ref, dl_ref,
                       pq_ref, nrm_ref):
    hp = lax.Precision.HIGHEST
    i = pl.program_id(0)
    feat = feat_ref[...]
    fr = fr_ref[...]
    h = jnp.sin(fr * (jnp.dot(feat, w1_ref[...], precision=hp, preferred_element_type=f32) + b1_ref[...]))
    h = jnp.sin(fr * (jnp.dot(h, w2_ref[...], precision=hp, preferred_element_type=f32) + b2_ref[...]))
    h = jnp.sin(fr * (jnp.dot(h, w3_ref[...], precision=hp, preferred_element_type=f32) + b3_ref[...]))
    h_hi = h.astype(bf16)
    h_lo = (h - h_hi.astype(f32)).astype(bf16)
    h_cat = jnp.concatenate([h_hi, h_lo], axis=1)

    def proj(cols):
        return (jnp.dot(h_cat, w4a_ref[:, cols], preferred_element_type=f32)
                + jnp.dot(h_hi, w4b_ref[:, cols], preferred_element_type=f32))

    win = jnp.exp(-feat[:, 0:1] * dl_ref[...])
    row = lax.broadcasted_iota(jnp.int32, win.shape, 0) + i * feat.shape[0]
    C = D_HYENA
    nrm = []
    for o in range(HYENA_ORDER):
        c0 = o * N_DIR * C
        fwd = proj(slice(c0, c0 + C)) * win
        bwd = proj(slice(c0 + C, c0 + 2 * C)) * win
        bwd = jnp.where(row == 0, 0.0, bwd)
        pq_ref[0, :, o * C:(o + 1) * C] = fwd + bwd
        pq_ref[1, :, o * C:(o + 1) * C] = fwd - bwd
        nrm.append(jnp.sum(jnp.abs(fwd) + jnp.abs(bwd), axis=0, keepdims=True))
    nrm = jnp.concatenate(nrm, axis=1)

    @pl.when(i == 0)
    def _():
        nrm_ref[...] = nrm

    @pl.when(i != 0)
    def _():
        nrm_ref[...] += nrm


def filter_mlp(feat, w1, b1, w2, b2, w3, b3, w4a, w4b, freq, deltas, *, tm=512):
    L = feat.shape[0]
    CO = HYENA_ORDER * D_HYENA
    full = lambda a: pl.BlockSpec(a.shape, lambda i: (0,) * a.ndim)
    args = (w1, b1, w2, b2, w3, b3, w4a, w4b, freq, deltas)
    return pl.pallas_call(
        _filter_mlp_kernel,
        grid=(L // tm,),
        in_specs=[pl.BlockSpec((tm, 128), lambda i: (i, 0))] + [full(a) for a in args],
        out_specs=[pl.BlockSpec((2, tm, CO), lambda i: (0, i, 0)), pl.BlockSpec((1, CO), lambda i: (0, 0))],
        out_shape=[jax.ShapeDtypeStruct((2, L, CO), f32), jax.ShapeDtypeStruct((1, CO), f32)],
        compiler_params=_params("arbitrary"),
        name="filter_mlp",
    )(feat, *args)


def _filter_stage2_kernel(m2_ref, ap_ref, aq_ref, nrm_ref, o_ref, *, k1b):
    inv = 1.0 / nrm_ref[...]
    pr, pi = pltpu.einshape("nkc->knc", ap_ref[:, 0]), pltpu.einshape("nkc->knc", ap_ref[:, 1])
    qr, qi = pltpu.einshape("nkc->knc", aq_ref[:, 0]), pltpu.einshape("nkc->knc", aq_ref[:, 1])
    for q in range(k1b):
        ap = jnp.concatenate([pr[q], pi[q]], axis=0)
        aq = jnp.concatenate([qr[q], qi[q]], axis=0)
        o_ref[q, 0:N2, :] = jnp.dot(m2_ref[0:N2, :], ap, preferred_element_type=f32) * inv
        o_ref[q, N2:2 * N2, :] = jnp.dot(m2_ref[N2:2 * N2, :], aq, preferred_element_type=f32) * inv


def filter_stage2(apq, m2, nrm, *, k1b=16, cb=256):
    _, _, _, H, CO = apq.shape
    k1b = min(k1b, H)
    return pl.pallas_call(
        functools.partial(_filter_stage2_kernel, k1b=k1b),
        grid=(CO // cb, H // k1b),
        in_specs=[
            pl.BlockSpec((2 * N2, 2 * N2), lambda c, k: (0, 0)),
            pl.BlockSpec((None, N2, 2, k1b, cb), lambda c, k: (0, 0, 0, k, c)),
            pl.BlockSpec((None, N2, 2, k1b, cb), lambda c, k: (1, 0, 0, k, c)),
            pl.BlockSpec((1, cb), lambda c, k: (0, c)),
        ],
        out_specs=pl.BlockSpec((k1b, 2 * N2, cb), lambda c, k: (k, 0, c)),
        out_shape=jax.ShapeDtypeStruct((H, 2 * N2, CO), f32),
        compiler_params=_params("parallel", "arbitrary"),
        name="filter_stage2",
    )(m2, apq, apq, nrm)


def _pad2(a, rows, cols):
    return jnp.pad(a, ((0, rows - a.shape[0]), (0, cols - a.shape[1])))


def hyena_filter(L, w1, b1, w2, b2, w3, b3, w4, freq, tab_f, m2):
    pos = jnp.arange(L, dtype=f32)[:, None]
    t = jnp.linspace(0.0, 1.0, L, dtype=f32)[:, None]
    bands = (FILTER_EMB - 1) // 2
    z = jnp.linspace(1e-4, bands - 1, bands, dtype=f32)[None, :] * ((2.0 * math.pi / L) * pos)
    feat = _pad2(jnp.concatenate([t, jnp.cos(z), -jnp.sin(z)], axis=-1), L, 128)
    deltas = jnp.abs(jnp.linspace(math.log(DECAY_TARGET) / DECAY_SLOW_PCT,
                                  math.log(DECAY_TARGET) / DECAY_FAST_PCT, D_HYENA, dtype=f32))[None, :]
    row = lambda v: _pad2(v[None, :], 1, 128)
    w4p = _pad2(w4, 128, w4.shape[1])
    w4_hi = w4p.astype(bf16)
    w4_lo = (w4p - w4_hi.astype(f32)).astype(bf16)
    pq, nrm = filter_mlp(feat, _pad2(w1, 128, 128), row(b1), _pad2(w2, 128, 128), row(b2), _pad2(w3, 128, 128),
                         row(b3), jnp.concatenate([w4_hi, w4_hi], axis=0), w4_lo, row(freq), deltas)
    apq = dft_stage1(pq, tab_f, bf16)
    H = L // N2
    return filter_stage2(apq.reshape(2, N2, 2, H, HYENA_ORDER * D_HYENA), m2, nrm)


def hyena_mixer(p3, conv_w, conv_b, hy_d, kf, tabs):
    tab_f, tab_i, m2, m2i = tabs
    B, L, _ = p3.shape
    C = D_HYENA
    H = L // N2
    cb = conv_b[None, :]
    xg = short_conv(p3, conv_w, cb, 0, 2 * C, bf16)
    z = short_conv(p3, conv_w, cb, 2 * C, C, f32)
    for o in range(HYENA_ORDER):
        a = dft_stage1(z, tab_f, bf16).reshape(B, N2, 2, H, C)
        bm = spectral_mid(a, kf, m2, m2i, o).reshape(B, N2, 2 * H, C)
        last = o == HYENA_ORDER - 1
        z = idft1_gate(bm, tab_i, xg, o * C, z, hy_d[o:o + 1], bf16 if last else f32)
    return z


def _gelu(x):
    return 0.5 * x * (1.0 + lax.erf(x * (1.0 / math.sqrt(2.0))))


def _sgu_kernel(pu_ref, pv_ref, g_ref, b_ref, ws_ref, bs_ref, o_ref):
    v = _gelu(pv_ref[...].astype(f32))
    mu = jnp.mean(v, axis=-1, keepdims=True)
    vc = v - mu
    var = jnp.mean(vc * vc, axis=-1, keepdims=True)
    vn = (vc * lax.rsqrt(var + LN_EPS) * g_ref[...] + b_ref[...]).astype(bf16)
    for c in range(pv_ref.shape[0] // CHUNK):
        rows = slice(c * CHUNK, (c + 1) * CHUNK)
        for g in range(SGU_GROUPS):
            cols = slice(g * SGU_GROUP_DIM, (g + 1) * SGU_GROUP_DIM)
            s = jnp.dot(ws_ref[g], vn[rows, cols], preferred_element_type=f32) + bs_ref[:, g:g + 1]
            u = _gelu(pu_ref[rows, cols].astype(f32))
            o_ref[rows, cols] = (u * s).astype(o_ref.dtype)


def sgu(p, ln_g, ln_b, w_s, b_s_t, *, tm=512):
    n = p.shape[0]
    ub = OFF_SG // D_SGU
    return pl.pallas_call(
        _sgu_kernel,
        grid=(n // tm,),
        in_specs=[
            pl.BlockSpec((tm, D_SGU), lambda i: (i, ub)),
            pl.BlockSpec((tm, D_SGU), lambda i: (i, ub + 1)),
            pl.BlockSpec((1, D_SGU), lambda i: (0, 0)),
            pl.BlockSpec((1, D_SGU), lambda i: (0, 0)),
            pl.BlockSpec((SGU_GROUPS, CHUNK, CHUNK), lambda i: (0, 0, 0)),
            pl.BlockSpec((CHUNK, SGU_GROUPS), lambda i: (0, 0)),
        ],
        out_specs=pl.BlockSpec((tm, D_SGU), lambda i: (i, 0)),
        out_shape=jax.ShapeDtypeStruct((n, D_SGU), bf16),
        compiler_params=_params("parallel"),
        name="sgu",
    )(p, p, ln_g, ln_b, w_s, b_s_t)


def kernel(x_prompt, x_sample, norm_mix_g, w_in, hy_conv_w, hy_conv_b, flt_w1, flt_b1, flt_w2, flt_b2,
           flt_w3, flt_b3, flt_w4, flt_sin_freq, hy_d, sg_ln_g, sg_ln_b, sg_w, sg_b, w_branch_hy,
           w_branch_sg, w_out, norm_moe_g, w_router, w_gate, w_up, w_down, norm_final_g):
    w_in_b = w_in[0].astype(bf16)
    w_bh_b = w_branch_hy[0].astype(bf16)
    w_bs_b = w_branch_sg[0].astype(bf16)
    w_out_b = w_out[0].astype(bf16)
    wg_b = w_gate[0].astype(bf16)
    wu_b = w_up[0].astype(bf16)
    wd_b = w_down[0].astype(bf16)
    g_mix = norm_mix_g[0].reshape(1, D_MODEL)
    sg_w_b = sg_w[0].astype(bf16)
    sg_b_t = sg_b[0].T

    m2, m2i = _stage2_tables()

    def mixer(x):
        B, L, D = x.shape
        n = B * L
        xt = x.reshape(n, D)
        p = in_proj(xt, g_mix, w_in_b)
        p3 = p.reshape(B, L, D_IN)
        tab_f, tab_i = _stage1_tables(L // N2)
        kf = hyena_filter(L, flt_w1[0], flt_b1[0], flt_w2[0], flt_b2[0], flt_w3[0], flt_b3[0],
                          flt_w4[0], flt_sin_freq[0], tab_f, m2)
        y_hy = hyena_mixer(p3, hy_conv_w[0], hy_conv_b[0], hy_d[0], kf,
                           (tab_f, tab_i, m2, m2i)).reshape(n, D_HYENA)
        y_sg = sgu(p, sg_ln_g, sg_ln_b, sg_w_b, sg_b_t)
        merged = branch_merge(y_hy, y_sg, w_bh_b, w_bs_b, p)
        return out_proj(merged, w_out_b, xt)

    g_moe = norm_moe_g[0].reshape(1, D_MODEL)
    g_fin = norm_final_g.reshape(1, D_MODEL)
    wr_t = w_router[0].T
    outs = []
    for x in (x_prompt, x_sample):
        y = moe_block(mixer(x), g_moe, wr_t, wg_b, wu_b, wd_b, g_fin)
        outs.append(y.reshape(x.shape))
    return tuple(outs)
```

```python
import functools
import math

import jax
import jax.numpy as jnp
from jax import lax
from jax.experimental import pallas as pl
from jax.experimental.pallas import tpu as pltpu

f32 = jnp.float32
bf16 = jnp.bfloat16
i32 = jnp.int32

D_MODEL = 2048
D_HYENA = D_MODEL // 2
HYENA_ORDER = 2
N_DIR = 2
FILTER_EMB = 33
DECAY_FAST_PCT = 0.3
DECAY_SLOW_PCT = 1.5
DECAY_TARGET = 1e-2
D_SGU = D_MODEL // 2
CHUNK = 128
SGU_GROUPS = 8
SGU_GROUP_DIM = D_SGU // SGU_GROUPS
N_EXPERTS = 16
EC_CAPACITY = 2
NORM_EPS = 1e-6
LN_EPS = 1e-5
OFF_SG = 3 * D_HYENA
OFF_GH = OFF_SG + 2 * D_SGU
OFF_GS = OFF_GH + D_MODEL
D_IN = OFF_GS + D_MODEL

VMEM_LIMIT_BYTES = 56 * 1024 * 1024


def _params(*sem):
    return pltpu.CompilerParams(dimension_semantics=sem, vmem_limit_bytes=VMEM_LIMIT_BYTES)


def _inproj_kernel(x_ref, g_ref, w_ref, o_ref, xn_ref):
    @pl.when(pl.program_id(1) == 0)
    def _():
        x = x_ref[...]
        ms = jnp.mean(x * x, axis=-1, keepdims=True)
        xn_ref[...] = (x * lax.rsqrt(ms + NORM_EPS) * g_ref[...]).astype(bf16)

    o_ref[...] = jnp.dot(xn_ref[...], w_ref[...], preferred_element_type=f32).astype(o_ref.dtype)


def in_proj(x, g, w, *, tm=1024, tn=2304):
    n, d = x.shape
    dn = w.shape[1]
    return pl.pallas_call(
        _inproj_kernel,
        grid=(n // tm, dn // tn),
        in_specs=[
            pl.BlockSpec((tm, d), lambda i, j: (i, 0)),
            pl.BlockSpec((1, d), lambda i, j: (0, 0)),
            pl.BlockSpec((d, tn), lambda i, j: (0, j)),
        ],
        out_specs=pl.BlockSpec((tm, tn), lambda i, j: (i, j)),
        out_shape=jax.ShapeDtypeStruct((n, dn), bf16),
        scratch_shapes=[pltpu.VMEM((tm, d), bf16)],
        compiler_params=_params("parallel", "arbitrary"),
        name="in_proj",
    )(x, g, w)


def _merge_kernel(yh_ref, ys_ref, wbh_ref, wbs_ref, gh_ref, gs_ref, o_ref):
    a = jnp.dot(yh_ref[...], wbh_ref[...], preferred_element_type=f32)
    b = jnp.dot(ys_ref[...], wbs_ref[...], preferred_element_type=f32)
    o = jax.nn.sigmoid(gh_ref[...].astype(f32)) * a + jax.nn.sigmoid(gs_ref[...].astype(f32)) * b
    o_ref[...] = o.astype(o_ref.dtype)


def branch_merge(y_hy, y_sg, w_bh, w_bs, p, *, tm=1024, tn=1024):
    n, dh = y_hy.shape
    d = w_bh.shape[1]
    gh0 = OFF_GH // tn
    gs0 = OFF_GS // tn
    return pl.pallas_call(
        _merge_kernel,
        grid=(n // tm, d // tn),
        in_specs=[
            pl.BlockSpec((tm, dh), lambda i, j: (i, 0)),
            pl.BlockSpec((tm, dh), lambda i, j: (i, 0)),
            pl.BlockSpec((dh, tn), lambda i, j: (0, j)),
            pl.BlockSpec((dh, tn), lambda i, j: (0, j)),
            pl.BlockSpec((tm, tn), lambda i, j: (i, gh0 + j)),
            pl.BlockSpec((tm, tn), lambda i, j: (i, gs0 + j)),
        ],
        out_specs=pl.BlockSpec((tm, tn), lambda i, j: (i, j)),
        out_shape=jax.ShapeDtypeStruct((n, d), bf16),
        compiler_params=_params("parallel", "arbitrary"),
        name="branch_merge",
    )(y_hy, y_sg, w_bh, w_bs, p, p)


def _outproj_kernel(m_ref, w_ref, x_ref, o_ref):
    o_ref[...] = x_ref[...] + jnp.dot(m_ref[...], w_ref[...], preferred_element_type=f32)


def out_proj(merged, w, x, *, tm=1024, tn=1024):
    n, d = merged.shape
    dn = w.shape[1]
    return pl.pallas_call(
        _outproj_kernel,
        grid=(n // tm, dn // tn),
        in_specs=[
            pl.BlockSpec((tm, d), lambda i, j: (i, 0)),
            pl.BlockSpec((d, tn), lambda i, j: (0, j)),
            pl.BlockSpec((tm, tn), lambda i, j: (i, j)),
        ],
        out_specs=pl.BlockSpec((tm, tn), lambda i, j: (i, j)),
        out_shape=jax.ShapeDtypeStruct((n, dn), f32),
        compiler_params=_params("parallel", "arbitrary"),
        name="out_proj",
    )(merged, w, x)


ROUTE_TILE = 256
SLOT_CHUNK = 48
LANES = 128
ROW_TILES = D_MODEL // LANES
BF16_ROWS = 16
READ_CHUNK = SLOT_CHUNK + BF16_ROWS


def _rms(x, g):
    return x * lax.rsqrt(jnp.mean(x * x, axis=-1, keepdims=True) + NORM_EPS) * g


def _router_kernel(x_ref, g_ref, wr_ref, o_ref):
    xn = _rms(x_ref[...], g_ref[...])
    logits = lax.dot_general(wr_ref[...], xn, (((1,), (1,)), ((), ())), precision=lax.Precision.HIGHEST,
                             preferred_element_type=f32)
    m = jnp.max(logits, axis=0, keepdims=True)
    e = jnp.exp(logits - m)
    o_ref[...] = e / jnp.sum(e, axis=0, keepdims=True)


def router_probs(xm, g, wr_t, *, tm=512):
    n, d = xm.shape
    return pl.pallas_call(
        _router_kernel,
        grid=(n // tm,),
        in_specs=[pl.BlockSpec((tm, d), lambda i: (i, 0)), pl.BlockSpec((1, d), lambda i: (0, 0)),
                  pl.BlockSpec((N_EXPERTS, d), lambda i: (0, 0))],
        out_specs=pl.BlockSpec((N_EXPERTS, tm), lambda i: (0, i)),
        out_shape=jax.ShapeDtypeStruct((N_EXPERTS, n), f32),
        compiler_params=_params("parallel"),
        name="router_probs",
    )(xm, g, wr_t)


def _select_kernel(p_ref, pos_ref, gate_ref, offs_ref, *, cap):
    p = p_ref[...]
    R = p.shape[0]
    bits = pltpu.bitcast(p, i32)

    def body(i, prefix):
        cand = prefix | lax.shift_left(jnp.int32(1), 30 - i)
        cnt = jnp.sum((bits >= cand).astype(i32))
        return jnp.where(cnt >= cap, cand, prefix)

    thr = lax.fori_loop(0, 31, body, jnp.int32(0))
    gt = bits > thr
    eq = bits == thr
    need = cap - jnp.sum(gt.astype(i32))

    li = lax.broadcasted_iota(i32, (LANES, LANES), 0)
    lj = lax.broadcasted_iota(i32, (LANES, LANES), 1)
    upper = (li <= lj).astype(bf16)
    ri = lax.broadcasted_iota(i32, (R, R), 0)
    rj = lax.broadcasted_iota(i32, (R, R), 1)
    rows_before = (rj < ri).astype(bf16)

    def prefix_counts(mask):
        m = mask.astype(bf16)
        incl = jnp.dot(m, upper, preferred_element_type=f32)
        tot = jnp.broadcast_to(incl[:, LANES - 1:LANES], (R, LANES)).astype(bf16)
        row_off = jnp.dot(rows_before, tot, preferred_element_type=f32)
        return incl - mask.astype(f32) + row_off, row_off

    eq_excl, _ = prefix_counts(eq)
    sel = gt | (eq & (eq_excl < need.astype(f32)))
    pos, row_off = prefix_counts(sel)
    pos_ref[...] = jnp.where(sel, pos.astype(i32), -1)
    gate_ref[...] = jnp.where(sel, p, 0.0)
    offs_ref[...] = row_off[:, 0:1].astype(i32)


def select_tokens(probs3, cap):
    E, R, _ = probs3.shape
    blk = pl.BlockSpec((None, R, LANES), lambda e: (e, 0, 0))
    return pl.pallas_call(
        functools.partial(_select_kernel, cap=cap),
        grid=(E,),
        in_specs=[blk],
        out_specs=[blk, blk, pl.BlockSpec((None, R, 1), lambda e: (e, 0, 0))],
        out_shape=[jax.ShapeDtypeStruct((E, R, LANES), i32), jax.ShapeDtypeStruct((E, R, LANES), f32),
                   jax.ShapeDtypeStruct((E, R, 1), i32)],
        compiler_params=_params("parallel"),
        name="select_tokens",
    )(probs3)


def _onehot_t(pos, start, width):
    j = lax.broadcasted_iota(i32, (width, pos.shape[1]), 0)
    return j == (pos - start)


def _dispatch_kernel(pos0_ref, x_ref, g_ref, pos_ref, gate_ref, xe_ref, ge_ref, xbuf_ref, gbuf_ref, sem_ref,
                     *, cap, n_tiles):
    i = pl.program_id(0)
    E, tm = pos_ref.shape
    slot = i % 2

    xn = _rms(x_ref[...], g_ref[...]).astype(bf16)
    eye = (lax.broadcasted_iota(i32, (E, LANES), 0) == lax.broadcasted_iota(i32, (E, LANES), 1)).astype(f32)
    gt = lax.dot_general(gate_ref[...], eye, (((0,), (0,)), ((), ())), precision=lax.Precision.HIGHEST,
                         preferred_element_type=f32)
    g_hi = gt.astype(bf16)
    g_lo = (gt - g_hi.astype(f32)).astype(bf16)
    pos = pos_ref[...]

    def base(e, tile):
        return pos0_ref[e * (n_tiles + 1) + tile]

    def count(e):
        return base(e, i + 1) - base(e, i)

    def build(c, dst):
        oh = jnp.concatenate([_onehot_t(pos[e:e + 1, :], base(e, i) + c * SLOT_CHUNK, SLOT_CHUNK)
                              for e in range(E)], axis=0)
        oh = jnp.where(oh, 1.0, 0.0).astype(bf16)
        xr = jnp.dot(oh, xn, preferred_element_type=f32).astype(bf16)
        tiles = jnp.stack([xr[:, s * LANES:(s + 1) * LANES] for s in range(ROW_TILES)], axis=0)
        xbuf_ref[dst] = pltpu.einshape("srl->rsl", tiles)
        gbuf_ref[dst, :, 0, :] = (jnp.dot(oh, g_hi, preferred_element_type=f32)
                                  + jnp.dot(oh, g_lo, preferred_element_type=f32))

    def copies(e, c, src, tile):
        rows = pl.ds(e * SLOT_CHUNK, SLOT_CHUNK)
        out_rows = pl.ds(base(e, tile) + c * SLOT_CHUNK, SLOT_CHUNK)
        return (pltpu.make_async_copy(xbuf_ref.at[src, rows], xe_ref.at[e, out_rows], sem_ref.at[0, src]),
                pltpu.make_async_copy(gbuf_ref.at[src, rows], ge_ref.at[e, out_rows], sem_ref.at[1, src]))

    @pl.when(i == 0)
    def _():
        xbuf_ref[2] = jnp.zeros(xbuf_ref.shape[1:], bf16)
        gbuf_ref[2] = jnp.zeros(gbuf_ref.shape[1:], f32)
        for e in range(E):
            rows = pl.ds(e * SLOT_CHUNK, SLOT_CHUNK)
            pad = pl.ds(cap, SLOT_CHUNK)
            for cp in (pltpu.make_async_copy(xbuf_ref.at[2, rows], xe_ref.at[e, pad], sem_ref.at[0, 2]),
                       pltpu.make_async_copy(gbuf_ref.at[2, rows], ge_ref.at[e, pad], sem_ref.at[1, 2])):
                cp.start()
                cp.wait()

    build(0, slot)

    @pl.when(i > 0)
    def _():
        for e in range(E):
            for cp in copies(e, 0, 1 - slot, i - 1):
                cp.wait()

    for e in range(E):
        for cp in copies(e, 0, slot, i):
            cp.start()

    nch = jnp.int32(0)
    for e in range(E):
        nch = jnp.maximum(nch, (count(e) + SLOT_CHUNK - 1) // SLOT_CHUNK)

    def overflow(c, carry):
        build(c, 2)
        for e in range(E):
            @pl.when(count(e) > c * SLOT_CHUNK)
            def _():
                for cp in copies(e, c, 2, i):
                    cp.start()
                    cp.wait()
        return carry

    lax.fori_loop(1, nch, overflow, 0)

    @pl.when(i == n_tiles - 1)
    def _():
        for e in range(E):
            for cp in copies(e, 0, slot, i):
                cp.wait()


def dispatch(xm, g, pos, gates, pos0, cap):
    n, d = xm.shape
    E = pos.shape[0]
    n_tiles = n // ROUTE_TILE
    rows = cap + SLOT_CHUNK
    grid_spec = pltpu.PrefetchScalarGridSpec(
        num_scalar_prefetch=1,
        grid=(n_tiles,),
        in_specs=[
            pl.BlockSpec((ROUTE_TILE, d), lambda i, p0: (i, 0)),
            pl.BlockSpec((1, d), lambda i, p0: (0, 0)),
            pl.BlockSpec((E, ROUTE_TILE), lambda i, p0: (0, i)),
            pl.BlockSpec((E, ROUTE_TILE), lambda i, p0: (0, i)),
        ],
        out_specs=[pl.BlockSpec(memory_space=pl.ANY), pl.BlockSpec(memory_space=pl.ANY)],
        scratch_shapes=[pltpu.VMEM((3, E * SLOT_CHUNK, ROW_TILES, LANES), bf16),
                        pltpu.VMEM((3, E * SLOT_CHUNK, 1, LANES), f32),
                        pltpu.SemaphoreType.DMA((2, 3))],
    )
    return pl.pallas_call(
        functools.partial(_dispatch_kernel, cap=cap, n_tiles=n_tiles),
        grid_spec=grid_spec,
        out_shape=[jax.ShapeDtypeStruct((E, rows, ROW_TILES, LANES), bf16),
                   jax.ShapeDtypeStruct((E, rows, 1, LANES), f32)],
        compiler_params=_params("arbitrary"),
        name="dispatch",
    )(pos0, xm, g, pos, gates)


def _expert_kernel(xe_ref, ge_ref, wg_ref, wu_ref, wd_ref, o_ref, xb_ref, acc_ref):
    e = pl.program_id(0)
    f = pl.program_id(2)

    @pl.when(f == 0)
    def _():
        tiles = pltpu.einshape("rsl->srl", xe_ref[...])
        for s in range(ROW_TILES):
            xb_ref[:, s * LANES:(s + 1) * LANES] = tiles[s]

    xb = xb_ref[...]
    hg = jnp.dot(xb, wg_ref[...], preferred_element_type=f32)
    hu = jnp.dot(xb, wu_ref[...], preferred_element_type=f32)
    h = (hg * jax.nn.sigmoid(hg) * hu).astype(bf16)
    part = jnp.dot(h, wd_ref[...], preferred_element_type=f32)

    @pl.when(f == 0)
    def _():
        acc_ref[...] = part

    @pl.when(f != 0)
    def _():
        acc_ref[...] += part

    @pl.when(f == pl.num_programs(2) - 1)
    def _():
        gl = ge_ref[:, 0, :]
        lane = lax.broadcasted_iota(i32, gl.shape, 1)
        gate = jnp.sum(jnp.where(lane == e, gl, 0.0), axis=-1, keepdims=True)
        o_ref[...] = (acc_ref[...] * gate).astype(o_ref.dtype)


def expert_ffn(xe, ge, wg, wu, wd, cap, *, tm=1024, tf=512):
    E = xe.shape[0]
    d = D_MODEL
    dff = wg.shape[2]
    tm = min(tm, cap)
    return pl.pallas_call(
        _expert_kernel,
        grid=(E, cap // tm, dff // tf),
        in_specs=[
            pl.BlockSpec((None, tm, ROW_TILES, LANES), lambda e_, i, f: (e_, i, 0, 0)),
            pl.BlockSpec((None, tm, 1, LANES), lambda e_, i, f: (e_, i, 0, 0)),
            pl.BlockSpec((None, d, tf), lambda e_, i, f: (e_, 0, f)),
            pl.BlockSpec((None, d, tf), lambda e_, i, f: (e_, 0, f)),
            pl.BlockSpec((None, tf, d), lambda e_, i, f: (e_, f, 0)),
        ],
        out_specs=pl.BlockSpec((None, tm, d), lambda e_, i, f: (e_, i, 0)),
        out_shape=jax.ShapeDtypeStruct((E, cap, d), bf16),
        scratch_shapes=[pltpu.VMEM((tm, d), bf16), pltpu.VMEM((tm, d), f32)],
        compiler_params=_params("parallel", "parallel", "arbitrary"),
        name="expert_ffn",
    )(xe, ge, wg, wu, wd)


def _combine_kernel(pos0_ref, x_ref, g_ref, pos_ref, ye_ref, o_ref, buf_ref, sem_ref, *, cap, n_tiles):
    i = pl.program_id(0)
    E, tm = pos_ref.shape
    slot = i % 2

    def base(e, tile):
        return pos0_ref[e * (n_tiles + 1) + tile]

    def count(e):
        return base(e, i + 1) - base(e, i)

    def start_row(e, c, tile):
        st = base(e, tile) + c * SLOT_CHUNK
        st = jnp.minimum((st // BF16_ROWS) * BF16_ROWS, cap - READ_CHUNK)
        return pl.multiple_of(st, BF16_ROWS)

    def copy(e, c, dst, tile):
        return pltpu.make_async_copy(ye_ref.at[e, pl.ds(start_row(e, c, tile), READ_CHUNK)],
                                     buf_ref.at[dst, pl.ds(e * READ_CHUNK, READ_CHUNK)], sem_ref.at[dst])

    @pl.when(i == 0)
    def _():
        for e in range(E):
            copy(e, 0, slot, i).start()

    @pl.when(i + 1 < n_tiles)
    def _():
        for e in range(E):
            copy(e, 0, 1 - slot, i + 1).start()

    pos = pos_ref[...]

    def onehot(c):
        parts = []
        for e in range(E):
            pe = pos[e:e + 1, :]
            local = pe - base(e, i)
            keep = (local >= c * SLOT_CHUNK) & (local < (c + 1) * SLOT_CHUNK) & (pe >= 0)
            oh = _onehot_t(pe, start_row(e, c, i), READ_CHUNK) & keep
            parts.append(jnp.where(oh, 1.0, 0.0).astype(bf16))
        return jnp.concatenate(parts, axis=0)

    for e in range(E):
        copy(e, 0, slot, i).wait()
    acc = lax.dot_general(onehot(0), buf_ref[slot], (((0,), (0,)), ((), ())), preferred_element_type=f32)

    nch = jnp.int32(0)
    for e in range(E):
        nch = jnp.maximum(nch, (count(e) + SLOT_CHUNK - 1) // SLOT_CHUNK)

    def overflow(c, acc):
        for e in range(E):
            cp = copy(e, c, 2, i)
            cp.start()
            cp.wait()
        return acc + lax.dot_general(onehot(c), buf_ref[2], (((0,), (0,)), ((), ())), preferred_element_type=f32)

    acc = lax.fori_loop(1, nch, overflow, acc)
    o_ref[...] = _rms(x_ref[...] + acc, g_ref[...])


def combine(xm, g_final, pos, ye, pos0, cap):
    n, d = xm.shape
    E = pos.shape[0]
    n_tiles = n // ROUTE_TILE
    grid_spec = pltpu.PrefetchScalarGridSpec(
        num_scalar_prefetch=1,
        grid=(n_tiles,),
        in_specs=[
            pl.BlockSpec((ROUTE_TILE, d), lambda i, p0: (i, 0)),
            pl.BlockSpec((1, d), lambda i, p0: (0, 0)),
            pl.BlockSpec((E, ROUTE_TILE), lambda i, p0: (0, i)),
            pl.BlockSpec(memory_space=pl.ANY),
        ],
        out_specs=pl.BlockSpec((ROUTE_TILE, d), lambda i, p0: (i, 0)),
        scratch_shapes=[pltpu.VMEM((3, E * READ_CHUNK, d), bf16), pltpu.SemaphoreType.DMA((3,))],
    )
    return pl.pallas_call(
        functools.partial(_combine_kernel, cap=cap, n_tiles=n_tiles),
        grid_spec=grid_spec,
        out_shape=jax.ShapeDtypeStruct((n, d), f32),
        compiler_params=_params("arbitrary"),
        name="combine",
    )(pos0, xm, g_final, pos, ye)


def moe_block(xm, g_moe, wr_t, wg, wu, wd, g_final):
    n = xm.shape[0]
    cap = EC_CAPACITY * n // N_EXPERTS
    n_tiles = n // ROUTE_TILE
    probs = router_probs(xm, g_moe, wr_t)
    pos3, gate3, offs = select_tokens(probs.reshape(N_EXPERTS, n // LANES, LANES), cap)
    pos = pos3.reshape(N_EXPERTS, n)
    gates = gate3.reshape(N_EXPERTS, n)
    tile_off = offs[:, ::ROUTE_TILE // LANES, 0]
    pos0 = jnp.concatenate([tile_off, jnp.full((N_EXPERTS, 1), cap, i32)], axis=1).reshape(-1)
    xe, ge = dispatch(xm, g_moe, pos, gates, pos0, cap)
    ye = expert_ffn(xe, ge, wg, wu, wd, cap)
    return combine(xm, g_final, pos, ye, pos0, cap)


N2 = 128


def _stage1_tables(H):
    n_total = 2 * N2 * H
    k1 = jnp.arange(H, dtype=jnp.int32)
    idx = jnp.arange(max(H, N2), dtype=jnp.int32)

    def phase(m):
        a = (m % (2 * n_total)).astype(f32) * (math.pi / n_total)
        return jnp.cos(a), jnp.sin(a)

    c1, s1 = phase((2 * k1 + 1)[:, None] * (N2 * idx[:H])[None, :])
    c2, s2 = phase((2 * k1 + 1)[None, :] * idx[:N2, None])
    c = c2[:, :, None] * c1[None] - s2[:, :, None] * s1[None]
    s_ = s2[:, :, None] * c1[None] + c2[:, :, None] * s1[None]
    fwd = jnp.concatenate([c, -s_], axis=1).astype(bf16)
    scale = 1.0 / (N2 * H)
    ct, st = jnp.swapaxes(c, 1, 2), jnp.swapaxes(s_, 1, 2)
    inv = (jnp.concatenate([ct, -st], axis=2) * scale).astype(bf16)
    return fwd, inv


def _stage2_tables():
    j = jnp.arange(N2, dtype=jnp.int32)
    th = ((j[:, None] * j[None, :]) % N2).astype(f32) * (2.0 * math.pi / N2)
    c, s = jnp.cos(th), jnp.sin(th)
    fwd = jnp.concatenate([jnp.concatenate([c, s], 1), jnp.concatenate([-s, c], 1)], 0)
    inv = jnp.concatenate([jnp.concatenate([c, -s], 1), jnp.concatenate([s, c], 1)], 0)
    return fwd.astype(bf16), inv.astype(bf16)


def _shortconv_kernel(x_ref, w_ref, b_ref, o_ref, s_ref, *, rows):
    L = x_ref.shape[0]
    zeros8 = jnp.zeros((8, x_ref.shape[1]), f32)
    s_ref[0:8, :] = zeros8
    s_ref[L + 8:L + 16, :] = zeros8

    def load(i, c):
        r0 = pl.multiple_of(i * rows, rows)
        s_ref[pl.ds(r0 + 8, rows), :] = x_ref[pl.ds(r0, rows), :].astype(f32)
        return c

    lax.fori_loop(0, L // rows, load, 0)
    w0, w1, w2, b = w_ref[0:1, :], w_ref[1:2, :], w_ref[2:3, :], b_ref[...]

    def conv(i, c):
        r0 = pl.multiple_of(i * rows, rows)
        prev = s_ref[pl.ds(r0 + 7, rows), :]
        cur = s_ref[pl.ds(r0 + 8, rows), :]
        nxt = s_ref[pl.ds(r0 + 9, rows), :]
        o_ref[pl.ds(r0, rows), :] = (prev * w0 + cur * w1 + nxt * w2 + b).astype(o_ref.dtype)
        return c

    lax.fori_loop(0, L // rows, conv, 0)


def short_conv(p3, w, b, col0, ncols, out_dtype, *, rows=256):
    B, L, _ = p3.shape
    cb = col0 // 128
    return pl.pallas_call(
        functools.partial(_shortconv_kernel, rows=rows),
        grid=(B, ncols // 128),
        in_specs=[
            pl.BlockSpec((None, L, 128), lambda b_, j: (b_, 0, cb + j)),
            pl.BlockSpec((3, 128), lambda b_, j: (0, cb + j)),
            pl.BlockSpec((1, 128), lambda b_, j: (0, cb + j)),
        ],
        out_specs=pl.BlockSpec((None, L, 128), lambda b_, j: (b_, 0, j)),
        out_shape=jax.ShapeDtypeStruct((B, L, ncols), out_dtype),
        scratch_shapes=[pltpu.VMEM((L + 16, 128), f32)],
        compiler_params=_params("parallel", "parallel"),
        name="short_conv",
    )(p3, w, b)


def _dft1_kernel(tab_ref, z_ref, o_ref, *, n2b, K):
    g = pl.program_id(2)
    rows = [z_ref[pl.ds(pl.multiple_of(n1 * N2 + g * n2b, n2b), n2b), :] for n1 in range(K)]
    zs = pltpu.einshape("kqc->qkc", jnp.stack(rows, axis=0)).astype(bf16)
    for q in range(n2b):
        o_ref[q] = jnp.dot(tab_ref[g * n2b + q], zs[q], preferred_element_type=f32).astype(o_ref.dtype)


def dft_stage1(z, tab, *, n2b=16):
    B, Lz, C = z.shape
    _, M, K = tab.shape
    assert Lz == K * N2
    return pl.pallas_call(
        functools.partial(_dft1_kernel, n2b=n2b, K=K),
        grid=(B, C // 128, N2 // n2b),
        in_specs=[
            pl.BlockSpec((N2, M, K), lambda b_, c, g: (0, 0, 0)),
            pl.BlockSpec((None, Lz, 128), lambda b_, c, g: (b_, 0, c)),
        ],
        out_specs=pl.BlockSpec((None, n2b, M, 128), lambda b_, c, g: (b_, g, 0, c)),
        out_shape=jax.ShapeDtypeStruct((B, N2, M, C), bf16),
        compiler_params=_params("parallel", "parallel", "arbitrary"),
        name="dft_stage1",
    )(tab, z)


def _mid_kernel(m2_ref, m2i_ref, a_ref, kf_ref, o_ref, *, k1b):
    are = pltpu.einshape("nkc->knc", a_ref[:, 0])
    aim = pltpu.einshape("nkc->knc", a_ref[:, 1])
    bre, bim = [], []
    for q in range(k1b):
        x = jnp.dot(m2_ref[...], jnp.concatenate([are[q], aim[q]], axis=0), preferred_element_type=f32)
        xr, xi = x[:N2], x[N2:]
        kf = kf_ref[q].astype(f32)
        kr, ki = kf[:N2], kf[N2:]
        y = jnp.concatenate([xr * kr - xi * ki, xr * ki + xi * kr], axis=0).astype(bf16)
        bq = jnp.dot(m2i_ref[...], y, preferred_element_type=f32).astype(bf16)
        bre.append(bq[:N2])
        bim.append(bq[N2:])
    o_ref[:, 0] = pltpu.einshape("knc->nkc", jnp.stack(bre, axis=0))
    o_ref[:, 1] = pltpu.einshape("knc->nkc", jnp.stack(bim, axis=0))


def spectral_mid(a, kf, m2, m2i, order, *, k1b=16, cb=256):
    B, _, _, H, C = a.shape
    k1b = min(k1b, H)
    cpo = C // cb
    return pl.pallas_call(
        functools.partial(_mid_kernel, k1b=k1b),
        grid=(B, cpo, H // k1b),
        in_specs=[
            pl.BlockSpec((2 * N2, 2 * N2), lambda b_, c, k: (0, 0)),
            pl.BlockSpec((2 * N2, 2 * N2), lambda b_, c, k: (0, 0)),
            pl.BlockSpec((None, N2, 2, k1b, cb), lambda b_, c, k: (b_, 0, 0, k, c)),
            pl.BlockSpec((k1b, 2 * N2, cb), lambda b_, c, k: (k, 0, order * cpo + c)),
        ],
        out_specs=pl.BlockSpec((None, N2, 2, k1b, cb), lambda b_, c, k: (b_, 0, 0, k, c)),
        out_shape=jax.ShapeDtypeStruct(a.shape, bf16),
        compiler_params=_params("parallel", "parallel", "arbitrary"),
        name="spectral_mid",
    )(m2, m2i, a, kf)


def _idft1_gate_kernel(tab_ref, b_ref, x_ref, v_ref, d_ref, o_ref, y_ref, *, n2b, H, rows):
    g = pl.program_id(2)
    ys = [jnp.dot(tab_ref[q], b_ref[q], preferred_element_type=f32) for q in range(n2b)]
    yt = pltpu.einshape("qhc->hqc", jnp.stack(ys, axis=0))
    for n1 in range(H):
        y_ref[pl.ds(pl.multiple_of(n1 * N2 + g * n2b, n2b), n2b), :] = yt[n1]

    @pl.when(g == pl.num_programs(2) - 1)
    def _():
        d = d_ref[...]

        def gate(i, c):
            r = pl.ds(pl.multiple_of(i * rows, rows), rows)
            o_ref[r, :] = (x_ref[r, :].astype(f32) * (y_ref[r, :] + v_ref[r, :].astype(f32) * d)).astype(o_ref.dtype)
            return c

        lax.fori_loop(0, (H * N2) // rows, gate, 0)


def idft1_gate(bm, tab, xg, xcol0, v, d, *, n2b=16, rows=256):
    B, _, M, C = bm.shape
    H = M // 2
    L = H * N2
    xb = xcol0 // 128
    return pl.pallas_call(
        functools.partial(_idft1_gate_kernel, n2b=n2b, H=H, rows=rows),
        grid=(B, C // 128, N2 // n2b),
        in_specs=[
            pl.BlockSpec((n2b, H, M), lambda b_, c, g: (g, 0, 0)),
            pl.BlockSpec((None, n2b, M, 128), lambda b_, c, g: (b_, g, 0, c)),
            pl.BlockSpec((None, L, 128), lambda b_, c, g: (b_, 0, xb + c)),
            pl.BlockSpec((None, L, 128), lambda b_, c, g: (b_, 0, c)),
            pl.BlockSpec((1, 128), lambda b_, c, g: (0, c)),
        ],
        out_specs=pl.BlockSpec((None, L, 128), lambda b_, c, g: (b_, 0, c)),
        out_shape=jax.ShapeDtypeStruct((B, L, C), bf16),
        scratch_shapes=[pltpu.VMEM((L, 128), f32)],
        compiler_params=_params("parallel", "parallel", "arbitrary"),
        name="idft1_gate",
    )(tab, bm, xg, v, d)


def _filter_mlp_kernel(feat_ref, w1_ref, b1_ref, w2_ref, b2_ref, w3_ref, b3_ref, w4a_ref, w4b_ref, fr_ref, dl_ref,
                       pq_ref, nrm_ref):
    hp = lax.Precision.HIGHEST
    i = pl.program_id(0)
    feat = feat_ref[...]
    fr = fr_ref[...]
    h = jnp.sin(fr * (jnp.dot(feat, w1_ref[...], precision=hp, preferred_element_type=f32) + b1_ref[...]))
    h = jnp.sin(fr * (jnp.dot(h, w2_ref[...], precision=hp, preferred_element_type=f32) + b2_ref[...]))
    h = jnp.sin(fr * (jnp.dot(h, w3_ref[...], precision=hp, preferred_element_type=f32) + b3_ref[...]))
    h_hi = h.astype(bf16)
    h_lo = (h - h_hi.astype(f32)).astype(bf16)
    h_cat = jnp.concatenate([h_hi, h_lo], axis=1)

    def proj(cols):
        return (jnp.dot(h_cat, w4a_ref[:, cols], preferred_element_type=f32)
                + jnp.dot(h_hi, w4b_ref[:, cols], preferred_element_type=f32))

    win = jnp.exp(-feat[:, 0:1] * dl_ref[...])
    row = lax.broadcasted_iota(jnp.int32, win.shape, 0) + i * feat.shape[0]
    C = D_HYENA
    nrm = []
    for o in range(HYENA_ORDER):
        c0 = o * N_DIR * C
        fwd = proj(slice(c0, c0 + C)) * win
        bwd = proj(slice(c0 + C, c0 + 2 * C)) * win
        bwd = jnp.where(row == 0, 0.0, bwd)
        pq_ref[0, :, o * C:(o + 1) * C] = fwd + bwd
        pq_ref[1, :, o * C:(o + 1) * C] = fwd - bwd
        nrm.append(jnp.sum(jnp.abs(fwd) + jnp.abs(bwd), axis=0, keepdims=True))
    nrm = jnp.concatenate(nrm, axis=1)

    @pl.when(i == 0)
    def _():
        nrm_ref[...] = nrm

    @pl.when(i != 0)
    def _():
        nrm_ref[...] += nrm


def filter_mlp(feat, w1, b1, w2, b2, w3, b3, w4a, w4b, freq, deltas, *, tm=512):
    L = feat.shape[0]
    CO = HYENA_ORDER * D_HYENA
    full = lambda a: pl.BlockSpec(a.shape, lambda i: (0,) * a.ndim)
    args = (w1, b1, w2, b2, w3, b3, w4a, w4b, freq, deltas)
    return pl.pallas_call(
        _filter_mlp_kernel,
        grid=(L // tm,),
        in_specs=[pl.BlockSpec((tm, 128), lambda i: (i, 0))] + [full(a) for a in args],
        out_specs=[pl.BlockSpec((2, tm, CO), lambda i: (0, i, 0)), pl.BlockSpec((1, CO), lambda i: (0, 0))],
        out_shape=[jax.ShapeDtypeStruct((2, L, CO), f32), jax.ShapeDtypeStruct((1, CO), f32)],
        compiler_params=_params("arbitrary"),
        name="filter_mlp",
    )(feat, *args)


def _filter_stage2_kernel(m2_ref, ap_ref, aq_ref, nrm_ref, o_ref, *, k1b):
    inv = 1.0 / nrm_ref[...]
    pr, pi = pltpu.einshape("nkc->knc", ap_ref[:, 0]), pltpu.einshape("nkc->knc", ap_ref[:, 1])
    qr, qi = pltpu.einshape("nkc->knc", aq_ref[:, 0]), pltpu.einshape("nkc->knc", aq_ref[:, 1])
    for q in range(k1b):
        ap = jnp.concatenate([pr[q], pi[q]], axis=0)
        aq = jnp.concatenate([qr[q], qi[q]], axis=0)
        re = jnp.dot(m2_ref[0:N2, :], ap, preferred_element_type=f32) * inv
        im = jnp.dot(m2_ref[N2:2 * N2, :], aq, preferred_element_type=f32) * inv
        o_ref[q] = jnp.concatenate([re, im], axis=0).astype(o_ref.dtype)


def filter_stage2(apq, m2, nrm, *, k1b=16, cb=256):
    _, _, _, H, CO = apq.shape
    k1b = min(k1b, H)
    return pl.pallas_call(
        functools.partial(_filter_stage2_kernel, k1b=k1b),
        grid=(CO // cb, H // k1b),
        in_specs=[
            pl.BlockSpec((2 * N2, 2 * N2), lambda c, k: (0, 0)),
            pl.BlockSpec((None, N2, 2, k1b, cb), lambda c, k: (0, 0, 0, k, c)),
            pl.BlockSpec((None, N2, 2, k1b, cb), lambda c, k: (1, 0, 0, k, c)),
            pl.BlockSpec((1, cb), lambda c, k: (0, c)),
        ],
        out_specs=pl.BlockSpec((k1b, 2 * N2, cb), lambda c, k: (k, 0, c)),
        out_shape=jax.ShapeDtypeStruct((H, 2 * N2, CO), bf16),
        compiler_params=_params("parallel", "arbitrary"),
        name="filter_stage2",
    )(m2, apq, apq, nrm)


def _pad2(a, rows, cols):
    return jnp.pad(a, ((0, rows - a.shape[0]), (0, cols - a.shape[1])))


def hyena_filter(L, w1, b1, w2, b2, w3, b3, w4, freq, tab_f, m2):
    pos = jnp.arange(L, dtype=f32)[:, None]
    t = jnp.linspace(0.0, 1.0, L, dtype=f32)[:, None]
    bands = (FILTER_EMB - 1) // 2
    z = jnp.linspace(1e-4, bands - 1, bands, dtype=f32)[None, :] * ((2.0 * math.pi / L) * pos)
    feat = _pad2(jnp.concatenate([t, jnp.cos(z), -jnp.sin(z)], axis=-1), L, 128)
    deltas = jnp.abs(jnp.linspace(math.log(DECAY_TARGET) / DECAY_SLOW_PCT,
                                  math.log(DECAY_TARGET) / DECAY_FAST_PCT, D_HYENA, dtype=f32))[None, :]
    row = lambda v: _pad2(v[None, :], 1, 128)
    w4p = _pad2(w4, 128, w4.shape[1])
    w4_hi = w4p.astype(bf16)
    w4_lo = (w4p - w4_hi.astype(f32)).astype(bf16)
    pq, nrm = filter_mlp(feat, _pad2(w1, 128, 128), row(b1), _pad2(w2, 128, 128), row(b2), _pad2(w3, 128, 128),
                         row(b3), jnp.concatenate([w4_hi, w4_hi], axis=0), w4_lo, row(freq), deltas)
    apq = dft_stage1(pq, tab_f)
    H = L // N2
    return filter_stage2(apq.reshape(2, N2, 2, H, HYENA_ORDER * D_HYENA), m2, nrm)


def hyena_mixer(p3, conv_w, conv_b, hy_d, kf, tabs):
    tab_f, tab_i, m2, m2i = tabs
    B, L, _ = p3.shape
    C = D_HYENA
    H = L // N2
    cb = conv_b[None, :]
    xg = short_conv(p3, conv_w, cb, 0, 2 * C, bf16)
    z = short_conv(p3, conv_w, cb, 2 * C, C, bf16)
    for o in range(HYENA_ORDER):
        a = dft_stage1(z, tab_f).reshape(B, N2, 2, H, C)
        bm = spectral_mid(a, kf, m2, m2i, o).reshape(B, N2, 2 * H, C)
        z = idft1_gate(bm, tab_i, xg, o * C, z, hy_d[o:o + 1])
    return z


def _gelu(x):
    return 0.5 * x * (1.0 + lax.erf(x * (1.0 / math.sqrt(2.0))))


def _sgu_kernel(pu_ref, pv_ref, g_ref, b_ref, ws_ref, bs_ref, o_ref):
    v = _gelu(pv_ref[...].astype(f32))
    mu = jnp.mean(v, axis=-1, keepdims=True)
    vc = v - mu
    var = jnp.mean(vc * vc, axis=-1, keepdims=True)
    vn = (vc * lax.rsqrt(var + LN_EPS) * g_ref[...] + b_ref[...]).astype(bf16)
    for c in range(pv_ref.shape[0] // CHUNK):
        rows = slice(c * CHUNK, (c + 1) * CHUNK)
        for g in range(SGU_GROUPS):
            cols = slice(g * SGU_GROUP_DIM, (g + 1) * SGU_GROUP_DIM)
            s = jnp.dot(ws_ref[g], vn[rows, cols], preferred_element_type=f32) + bs_ref[:, g:g + 1]
            u = _gelu(pu_ref[rows, cols].astype(f32))
            o_ref[rows, cols] = (u * s).astype(o_ref.dtype)


def sgu(p, ln_g, ln_b, w_s, b_s_t, *, tm=512):
    n = p.shape[0]
    ub = OFF_SG // D_SGU
    return pl.pallas_call(
        _sgu_kernel,
        grid=(n // tm,),
        in_specs=[
            pl.BlockSpec((tm, D_SGU), lambda i: (i, ub)),
            pl.BlockSpec((tm, D_SGU), lambda i: (i, ub + 1)),
            pl.BlockSpec((1, D_SGU), lambda i: (0, 0)),
            pl.BlockSpec((1, D_SGU), lambda i: (0, 0)),
            pl.BlockSpec((SGU_GROUPS, CHUNK, CHUNK), lambda i: (0, 0, 0)),
            pl.BlockSpec((CHUNK, SGU_GROUPS), lambda i: (0, 0)),
        ],
        out_specs=pl.BlockSpec((tm, D_SGU), lambda i: (i, 0)),
        out_shape=jax.ShapeDtypeStruct((n, D_SGU), bf16),
        compiler_params=_params("parallel"),
        name="sgu",
    )(p, p, ln_g, ln_b, w_s, b_s_t)


def kernel(x_prompt, x_sample, norm_mix_g, w_in, hy_conv_w, hy_conv_b, flt_w1, flt_b1, flt_w2, flt_b2,
           flt_w3, flt_b3, flt_w4, flt_sin_freq, hy_d, sg_ln_g, sg_ln_b, sg_w, sg_b, w_branch_hy,
           w_branch_sg, w_out, norm_moe_g, w_router, w_gate, w_up, w_down, norm_final_g):
    w_in_b = w_in[0].astype(bf16)
    w_bh_b = w_branch_hy[0].astype(bf16)
    w_bs_b = w_branch_sg[0].astype(bf16)
    w_out_b = w_out[0].astype(bf16)
    wg_b = w_gate[0].astype(bf16)
    wu_b = w_up[0].astype(bf16)
    wd_b = w_down[0].astype(bf16)
    g_mix = norm_mix_g[0].reshape(1, D_MODEL)
    sg_w_b = sg_w[0].astype(bf16)
    sg_b_t = sg_b[0].T

    m2, m2i = _stage2_tables()

    def mixer(x):
        B, L, D = x.shape
        n = B * L
        xt = x.reshape(n, D)
        p = in_proj(xt, g_mix, w_in_b)
        p3 = p.reshape(B, L, D_IN)
        tab_f, tab_i = _stage1_tables(L // N2)
        kf = hyena_filter(L, flt_w1[0], flt_b1[0], flt_w2[0], flt_b2[0], flt_w3[0], flt_b3[0],
                          flt_w4[0], flt_sin_freq[0], tab_f, m2)
        y_hy = hyena_mixer(p3, hy_conv_w[0], hy_conv_b[0], hy_d[0], kf,
                           (tab_f, tab_i, m2, m2i)).reshape(n, D_HYENA)
        y_sg = sgu(p, sg_ln_g, sg_ln_b, sg_w_b, sg_b_t)
        merged = branch_merge(y_hy, y_sg, w_bh_b, w_bs_b, p)
        return out_proj(merged, w_out_b, xt)

    g_moe = norm_moe_g[0].reshape(1, D_MODEL)
    g_fin = norm_final_g.reshape(1, D_MODEL)
    wr_t = w_router[0].T
    outs = []
    for x in (x_prompt, x_sample):
        y = moe_block(mixer(x), g_moe, wr_t, wg_b, wu_b, wd_b, g_fin)
        outs.append(y.reshape(x.shape))
    return tuple(outs)
```

```python
import functools
import math

import jax
import jax.numpy as jnp
from jax import lax
from jax.experimental import pallas as pl
from jax.experimental.pallas import tpu as pltpu

f32 = jnp.float32
bf16 = jnp.bfloat16
i32 = jnp.int32

D_MODEL = 2048
D_HYENA = D_MODEL // 2
HYENA_ORDER = 2
N_DIR = 2
FILTER_EMB = 33
DECAY_FAST_PCT = 0.3
DECAY_SLOW_PCT = 1.5
DECAY_TARGET = 1e-2
D_SGU = D_MODEL // 2
CHUNK = 128
SGU_GROUPS = 8
SGU_GROUP_DIM = D_SGU // SGU_GROUPS
N_EXPERTS = 16
EC_CAPACITY = 2
NORM_EPS = 1e-6
LN_EPS = 1e-5
OFF_SG = 3 * D_HYENA
OFF_GH = OFF_SG + 2 * D_SGU
OFF_GS = OFF_GH + D_MODEL
D_IN = OFF_GS + D_MODEL

VMEM_LIMIT_BYTES = 56 * 1024 * 1024


def _params(*sem):
    return pltpu.CompilerParams(dimension_semantics=sem, vmem_limit_bytes=VMEM_LIMIT_BYTES)


def _inproj_kernel(x_ref, g_ref, w_ref, o_ref, xn_ref):
    @pl.when(pl.program_id(1) == 0)
    def _():
        x = x_ref[...]
        ms = jnp.mean(x * x, axis=-1, keepdims=True)
        xn_ref[...] = (x * lax.rsqrt(ms + NORM_EPS) * g_ref[...]).astype(bf16)

    o_ref[...] = jnp.dot(xn_ref[...], w_ref[...], preferred_element_type=f32).astype(o_ref.dtype)


def in_proj(x, g, w, *, tm=1024, tn=2304):
    n, d = x.shape
    dn = w.shape[1]
    return pl.pallas_call(
        _inproj_kernel,
        grid=(n // tm, dn // tn),
        in_specs=[
            pl.BlockSpec((tm, d), lambda i, j: (i, 0)),
            pl.BlockSpec((1, d), lambda i, j: (0, 0)),
            pl.BlockSpec((d, tn), lambda i, j: (0, j)),
        ],
        out_specs=pl.BlockSpec((tm, tn), lambda i, j: (i, j)),
        out_shape=jax.ShapeDtypeStruct((n, dn), bf16),
        scratch_shapes=[pltpu.VMEM((tm, d), bf16)],
        compiler_params=_params("parallel", "arbitrary"),
        name="in_proj",
    )(x, g, w)


def _merge_kernel(yh_ref, ys_ref, wbh_ref, wbs_ref, gh_ref, gs_ref, o_ref):
    a = jnp.dot(yh_ref[...], wbh_ref[...], preferred_element_type=f32)
    b = jnp.dot(ys_ref[...], wbs_ref[...], preferred_element_type=f32)
    o = jax.nn.sigmoid(gh_ref[...].astype(f32)) * a + jax.nn.sigmoid(gs_ref[...].astype(f32)) * b
    o_ref[...] = o.astype(o_ref.dtype)


def branch_merge(y_hy, y_sg, w_bh, w_bs, p, *, tm=1024, tn=1024):
    n, dh = y_hy.shape
    d = w_bh.shape[1]
    gh0 = OFF_GH // tn
    gs0 = OFF_GS // tn
    return pl.pallas_call(
        _merge_kernel,
        grid=(n // tm, d // tn),
        in_specs=[
            pl.BlockSpec((tm, dh), lambda i, j: (i, 0)),
            pl.BlockSpec((tm, dh), lambda i, j: (i, 0)),
            pl.BlockSpec((dh, tn), lambda i, j: (0, j)),
            pl.BlockSpec((dh, tn), lambda i, j: (0, j)),
            pl.BlockSpec((tm, tn), lambda i, j: (i, gh0 + j)),
            pl.BlockSpec((tm, tn), lambda i, j: (i, gs0 + j)),
        ],
        out_specs=pl.BlockSpec((tm, tn), lambda i, j: (i, j)),
        out_shape=jax.ShapeDtypeStruct((n, d), bf16),
        compiler_params=_params("parallel", "arbitrary"),
        name="branch_merge",
    )(y_hy, y_sg, w_bh, w_bs, p, p)


def _outproj_kernel(m_ref, w_ref, x_ref, o_ref):
    o_ref[...] = x_ref[...] + jnp.dot(m_ref[...], w_ref[...], preferred_element_type=f32)


def out_proj(merged, w, x, *, tm=1024, tn=1024):
    n, d = merged.shape
    dn = w.shape[1]
    return pl.pallas_call(
        _outproj_kernel,
        grid=(n // tm, dn // tn),
        in_specs=[
            pl.BlockSpec((tm, d), lambda i, j: (i, 0)),
            pl.BlockSpec((d, tn), lambda i, j: (0, j)),
            pl.BlockSpec((tm, tn), lambda i, j: (i, j)),
        ],
        out_specs=pl.BlockSpec((tm, tn), lambda i, j: (i, j)),
        out_shape=jax.ShapeDtypeStruct((n, dn), f32),
        compiler_params=_params("parallel", "arbitrary"),
        name="out_proj",
    )(merged, w, x)


ROUTE_TILE = 256
SLOT_CHUNK = 48
LANES = 128
ROW_TILES = D_MODEL // LANES
BF16_ROWS = 16
READ_CHUNK = SLOT_CHUNK + BF16_ROWS


def _rms(x, g):
    return x * lax.rsqrt(jnp.mean(x * x, axis=-1, keepdims=True) + NORM_EPS) * g


def _router_kernel(x_ref, g_ref, wr_ref, o_ref):
    xn = _rms(x_ref[...], g_ref[...])
    logits = lax.dot_general(wr_ref[...], xn, (((1,), (1,)), ((), ())), precision=lax.Precision.HIGHEST,
                             preferred_element_type=f32)
    m = jnp.max(logits, axis=0, keepdims=True)
    e = jnp.exp(logits - m)
    o_ref[...] = e / jnp.sum(e, axis=0, keepdims=True)


def router_probs(xm, g, wr_t, *, tm=512):
    n, d = xm.shape
    return pl.pallas_call(
        _router_kernel,
        grid=(n // tm,),
        in_specs=[pl.BlockSpec((tm, d), lambda i: (i, 0)), pl.BlockSpec((1, d), lambda i: (0, 0)),
                  pl.BlockSpec((N_EXPERTS, d), lambda i: (0, 0))],
        out_specs=pl.BlockSpec((N_EXPERTS, tm), lambda i: (0, i)),
        out_shape=jax.ShapeDtypeStruct((N_EXPERTS, n), f32),
        compiler_params=_params("parallel"),
        name="router_probs",
    )(xm, g, wr_t)


def _select_kernel(p_ref, pos_ref, gate_ref, offs_ref, *, cap):
    p = p_ref[...]
    R = p.shape[0]
    bits = pltpu.bitcast(p, i32)

    def body(i, prefix):
        cand = prefix | lax.shift_left(jnp.int32(1), 30 - i)
        cnt = jnp.sum((bits >= cand).astype(i32))
        return jnp.where(cnt >= cap, cand, prefix)

    thr = lax.fori_loop(0, 31, body, jnp.int32(0))
    gt = bits > thr
    eq = bits == thr
    need = cap - jnp.sum(gt.astype(i32))

    li = lax.broadcasted_iota(i32, (LANES, LANES), 0)
    lj = lax.broadcasted_iota(i32, (LANES, LANES), 1)
    upper = (li <= lj).astype(bf16)
    ri = lax.broadcasted_iota(i32, (R, R), 0)
    rj = lax.broadcasted_iota(i32, (R, R), 1)
    rows_before = (rj < ri).astype(bf16)

    def prefix_counts(mask):
        m = mask.astype(bf16)
        incl = jnp.dot(m, upper, preferred_element_type=f32)
        tot = jnp.broadcast_to(incl[:, LANES - 1:LANES], (R, LANES)).astype(bf16)
        row_off = jnp.dot(rows_before, tot, preferred_element_type=f32)
        return incl - mask.astype(f32) + row_off, row_off

    eq_excl, _ = prefix_counts(eq)
    sel = gt | (eq & (eq_excl < need.astype(f32)))
    pos, row_off = prefix_counts(sel)
    pos_ref[...] = jnp.where(sel, pos.astype(i32), -1)
    gate_ref[...] = jnp.where(sel, p, 0.0)
    offs_ref[...] = row_off[:, 0:1].astype(i32)


def select_tokens(probs3, cap):
    E, R, _ = probs3.shape
    blk = pl.BlockSpec((None, R, LANES), lambda e: (e, 0, 0))
    return pl.pallas_call(
        functools.partial(_select_kernel, cap=cap),
        grid=(E,),
        in_specs=[blk],
        out_specs=[blk, blk, pl.BlockSpec((None, R, 1), lambda e: (e, 0, 0))],
        out_shape=[jax.ShapeDtypeStruct((E, R, LANES), i32), jax.ShapeDtypeStruct((E, R, LANES), f32),
                   jax.ShapeDtypeStruct((E, R, 1), i32)],
        compiler_params=_params("parallel"),
        name="select_tokens",
    )(probs3)


def _onehot_t(pos, start, width):
    j = lax.broadcasted_iota(i32, (width, pos.shape[1]), 0)
    return j == (pos - start)


def _dispatch_kernel(pos0_ref, x_ref, g_ref, pos_ref, gate_ref, xe_ref, ge_ref, xbuf_ref, gbuf_ref, sem_ref,
                     *, cap, n_tiles):
    i = pl.program_id(0)
    E, tm = pos_ref.shape
    slot = i % 2

    xn = _rms(x_ref[...], g_ref[...]).astype(bf16)
    eye = (lax.broadcasted_iota(i32, (E, LANES), 0) == lax.broadcasted_iota(i32, (E, LANES), 1)).astype(f32)
    gt = lax.dot_general(gate_ref[...], eye, (((0,), (0,)), ((), ())), precision=lax.Precision.HIGHEST,
                         preferred_element_type=f32)
    g_hi = gt.astype(bf16)
    g_lo = (gt - g_hi.astype(f32)).astype(bf16)
    pos = pos_ref[...]

    def base(e, tile):
        return pos0_ref[e * (n_tiles + 1) + tile]

    def count(e):
        return base(e, i + 1) - base(e, i)

    def build(c, dst):
        oh = jnp.concatenate([_onehot_t(pos[e:e + 1, :], base(e, i) + c * SLOT_CHUNK, SLOT_CHUNK)
                              for e in range(E)], axis=0)
        oh = jnp.where(oh, 1.0, 0.0).astype(bf16)
        xr = jnp.dot(oh, xn, preferred_element_type=f32).astype(bf16)
        tiles = jnp.stack([xr[:, s * LANES:(s + 1) * LANES] for s in range(ROW_TILES)], axis=0)
        xbuf_ref[dst] = pltpu.einshape("srl->rsl", tiles)
        gbuf_ref[dst, :, 0, :] = (jnp.dot(oh, g_hi, preferred_element_type=f32)
                                  + jnp.dot(oh, g_lo, preferred_element_type=f32))

    def copies(e, c, src, tile):
        rows = pl.ds(e * SLOT_CHUNK, SLOT_CHUNK)
        out_rows = pl.ds(base(e, tile) + c * SLOT_CHUNK, SLOT_CHUNK)
        return (pltpu.make_async_copy(xbuf_ref.at[src, rows], xe_ref.at[e, out_rows], sem_ref.at[0, src]),
                pltpu.make_async_copy(gbuf_ref.at[src, rows], ge_ref.at[e, out_rows], sem_ref.at[1, src]))

    @pl.when(i == 0)
    def _():
        xbuf_ref[2] = jnp.zeros(xbuf_ref.shape[1:], bf16)
        gbuf_ref[2] = jnp.zeros(gbuf_ref.shape[1:], f32)
        for e in range(E):
            rows = pl.ds(e * SLOT_CHUNK, SLOT_CHUNK)
            pad = pl.ds(cap, SLOT_CHUNK)
            for cp in (pltpu.make_async_copy(xbuf_ref.at[2, rows], xe_ref.at[e, pad], sem_ref.at[0, 2]),
                       pltpu.make_async_copy(gbuf_ref.at[2, rows], ge_ref.at[e, pad], sem_ref.at[1, 2])):
                cp.start()
                cp.wait()

    build(0, slot)

    @pl.when(i > 0)
    def _():
        for e in range(E):
            for cp in copies(e, 0, 1 - slot, i - 1):
                cp.wait()

    for e in range(E):
        for cp in copies(e, 0, slot, i):
            cp.start()

    nch = jnp.int32(0)
    for e in range(E):
        nch = jnp.maximum(nch, (count(e) + SLOT_CHUNK - 1) // SLOT_CHUNK)

    def overflow(c, carry):
        build(c, 2)
        for e in range(E):
            @pl.when(count(e) > c * SLOT_CHUNK)
            def _():
                for cp in copies(e, c, 2, i):
                    cp.start()
                    cp.wait()
        return carry

    lax.fori_loop(1, nch, overflow, 0)

    @pl.when(i == n_tiles - 1)
    def _():
        for e in range(E):
            for cp in copies(e, 0, slot, i):
                cp.wait()


def dispatch(xm, g, pos, gates, pos0, cap):
    n, d = xm.shape
    E = pos.shape[0]
    n_tiles = n // ROUTE_TILE
    rows = cap + SLOT_CHUNK
    grid_spec = pltpu.PrefetchScalarGridSpec(
        num_scalar_prefetch=1,
        grid=(n_tiles,),
        in_specs=[
            pl.BlockSpec((ROUTE_TILE, d), lambda i, p0: (i, 0)),
            pl.BlockSpec((1, d), lambda i, p0: (0, 0)),
            pl.BlockSpec((E, ROUTE_TILE), lambda i, p0: (0, i)),
            pl.BlockSpec((E, ROUTE_TILE), lambda i, p0: (0, i)),
        ],
        out_specs=[pl.BlockSpec(memory_space=pl.ANY), pl.BlockSpec(memory_space=pl.ANY)],
        scratch_shapes=[pltpu.VMEM((3, E * SLOT_CHUNK, ROW_TILES, LANES), bf16),
                        pltpu.VMEM((3, E * SLOT_CHUNK, 1, LANES), f32),
                        pltpu.SemaphoreType.DMA((2, 3))],
    )
    return pl.pallas_call(
        functools.partial(_dispatch_kernel, cap=cap, n_tiles=n_tiles),
        grid_spec=grid_spec,
        out_shape=[jax.ShapeDtypeStruct((E, rows, ROW_TILES, LANES), bf16),
                   jax.ShapeDtypeStruct((E, rows, 1, LANES), f32)],
        compiler_params=_params("arbitrary"),
        name="dispatch",
    )(pos0, xm, g, pos, gates)


def _expert_up_kernel(xe_ref, wg_ref, wu_ref, h_ref, xb_ref):
    @pl.when(pl.program_id(2) == 0)
    def _():
        tiles = pltpu.einshape("rsl->srl", xe_ref[...])
        for s in range(ROW_TILES):
            xb_ref[:, s * LANES:(s + 1) * LANES] = tiles[s]

    xb = xb_ref[...]
    hg = jnp.dot(xb, wg_ref[...], preferred_element_type=f32)
    hu = jnp.dot(xb, wu_ref[...], preferred_element_type=f32)
    h_ref[...] = (hg * jax.nn.sigmoid(hg) * hu).astype(h_ref.dtype)


def _expert_down_kernel(h_ref, ge_ref, wd_ref, o_ref):
    gl = ge_ref[:, 0, :]
    lane = lax.broadcasted_iota(i32, gl.shape, 1)
    gate = jnp.sum(jnp.where(lane == pl.program_id(0), gl, 0.0), axis=-1, keepdims=True)
    o_ref[...] = (jnp.dot(h_ref[...], wd_ref[...], preferred_element_type=f32) * gate).astype(o_ref.dtype)


def expert_ffn(xe, ge, wg, wu, wd, cap, *, tm=1024, tf=1024, tn=1024):
    E = xe.shape[0]
    d = D_MODEL
    dff = wg.shape[2]
    tm = min(tm, cap)
    h = pl.pallas_call(
        _expert_up_kernel,
        grid=(E, cap // tm, dff // tf),
        in_specs=[
            pl.BlockSpec((None, tm, ROW_TILES, LANES), lambda e_, i, f: (e_, i, 0, 0)),
            pl.BlockSpec((None, d, tf), lambda e_, i, f: (e_, 0, f)),
            pl.BlockSpec((None, d, tf), lambda e_, i, f: (e_, 0, f)),
        ],
        out_specs=pl.BlockSpec((None, tm, tf), lambda e_, i, f: (e_, i, f)),
        out_shape=jax.ShapeDtypeStruct((E, cap, dff), bf16),
        scratch_shapes=[pltpu.VMEM((tm, d), bf16)],
        compiler_params=_params("parallel", "parallel", "arbitrary"),
        name="expert_up",
    )(xe, wg, wu)
    return pl.pallas_call(
        _expert_down_kernel,
        grid=(E, cap // tm, d // tn),
        in_specs=[
            pl.BlockSpec((None, tm, dff), lambda e_, i, j: (e_, i, 0)),
            pl.BlockSpec((None, tm, 1, LANES), lambda e_, i, j: (e_, i, 0, 0)),
            pl.BlockSpec((None, dff, tn), lambda e_, i, j: (e_, 0, j)),
        ],
        out_specs=pl.BlockSpec((None, tm, tn), lambda e_, i, j: (e_, i, j)),
        out_shape=jax.ShapeDtypeStruct((E, cap, d), bf16),
        compiler_params=_params("parallel", "parallel", "arbitrary"),
        name="expert_down",
    )(h, ge, wd)


def _combine_kernel(pos0_ref, x_ref, g_ref, pos_ref, ye_ref, o_ref, buf_ref, sem_ref, *, cap, n_tiles):
    i = pl.program_id(0)
    E, tm = pos_ref.shape
    slot = i % 2

    def base(e, tile):
        return pos0_ref[e * (n_tiles + 1) + tile]

    def count(e):
        return base(e, i + 1) - base(e, i)

    def start_row(e, c, tile):
        st = base(e, tile) + c * SLOT_CHUNK
        st = jnp.minimum((st // BF16_ROWS) * BF16_ROWS, cap - READ_CHUNK)
        return pl.multiple_of(st, BF16_ROWS)

    def copy(e, c, dst, tile):
        return pltpu.make_async_copy(ye_ref.at[e, pl.ds(start_row(e, c, tile), READ_CHUNK)],
                                     buf_ref.at[dst, pl.ds(e * READ_CHUNK, READ_CHUNK)], sem_ref.at[dst])

    @pl.when(i == 0)
    def _():
        for e in range(E):
            copy(e, 0, slot, i).start()

    @pl.when(i + 1 < n_tiles)
    def _():
        for e in range(E):
            copy(e, 0, 1 - slot, i + 1).start()

    pos = pos_ref[...]

    def onehot(c):
        parts = []
        for e in range(E):
            pe = pos[e:e + 1, :]
            local = pe - base(e, i)
            keep = (local >= c * SLOT_CHUNK) & (local < (c + 1) * SLOT_CHUNK) & (pe >= 0)
            oh = _onehot_t(pe, start_row(e, c, i), READ_CHUNK) & keep
            parts.append(jnp.where(oh, 1.0, 0.0).astype(bf16))
        return jnp.concatenate(parts, axis=0)

    for e in range(E):
        copy(e, 0, slot, i).wait()
    acc = lax.dot_general(onehot(0), buf_ref[slot], (((0,), (0,)), ((), ())), preferred_element_type=f32)

    nch = jnp.int32(0)
    for e in range(E):
        nch = jnp.maximum(nch, (count(e) + SLOT_CHUNK - 1) // SLOT_CHUNK)

    def overflow(c, acc):
        for e in range(E):
            cp = copy(e, c, 2, i)
            cp.start()
            cp.wait()
        return acc + lax.dot_general(onehot(c), buf_ref[2], (((0,), (0,)), ((), ())), preferred_element_type=f32)

    acc = lax.fori_loop(1, nch, overflow, acc)
    o_ref[...] = _rms(x_ref[...] + acc, g_ref[...])


def combine(xm, g_final, pos, ye, pos0, cap):
    n, d = xm.shape
    E = pos.shape[0]
    n_tiles = n // ROUTE_TILE
    grid_spec = pltpu.PrefetchScalarGridSpec(
        num_scalar_prefetch=1,
        grid=(n_tiles,),
        in_specs=[
            pl.BlockSpec((ROUTE_TILE, d), lambda i, p0: (i, 0)),
            pl.BlockSpec((1, d), lambda i, p0: (0, 0)),
            pl.BlockSpec((E, ROUTE_TILE), lambda i, p0: (0, i)),
            pl.BlockSpec(memory_space=pl.ANY),
        ],
        out_specs=pl.BlockSpec((ROUTE_TILE, d), lambda i, p0: (i, 0)),
        scratch_shapes=[pltpu.VMEM((3, E * READ_CHUNK, d), bf16), pltpu.SemaphoreType.DMA((3,))],
    )
    return pl.pallas_call(
        functools.partial(_combine_kernel, cap=cap, n_tiles=n_tiles),
        grid_spec=grid_spec,
        out_shape=jax.ShapeDtypeStruct((n, d), f32),
        compiler_params=_params("arbitrary"),
        name="combine",
    )(pos0, xm, g_final, pos, ye)


def moe_block(xm, g_moe, wr_t, wg, wu, wd, g_final):
    n = xm.shape[0]
    cap = EC_CAPACITY * n // N_EXPERTS
    n_tiles = n // ROUTE_TILE
    probs = router_probs(xm, g_moe, wr_t)
    pos3, gate3, offs = select_tokens(probs.reshape(N_EXPERTS, n // LANES, LANES), cap)
    pos = pos3.reshape(N_EXPERTS, n)
    gates = gate3.reshape(N_EXPERTS, n)
    tile_off = offs[:, ::ROUTE_TILE // LANES, 0]
    pos0 = jnp.concatenate([tile_off, jnp.full((N_EXPERTS, 1), cap, i32)], axis=1).reshape(-1)
    xe, ge = dispatch(xm, g_moe, pos, gates, pos0, cap)
    ye = expert_ffn(xe, ge, wg, wu, wd, cap)
    return combine(xm, g_final, pos, ye, pos0, cap)


N2 = 128


def _stage1_tables(H):
    n_total = 2 * N2 * H
    k1 = jnp.arange(H, dtype=jnp.int32)
    idx = jnp.arange(max(H, N2), dtype=jnp.int32)

    def phase(m):
        a = (m % (2 * n_total)).astype(f32) * (math.pi / n_total)
        return jnp.cos(a), jnp.sin(a)

    c1, s1 = phase((2 * k1 + 1)[:, None] * (N2 * idx[:H])[None, :])
    c2, s2 = phase((2 * k1 + 1)[None, :] * idx[:N2, None])
    c = c2[:, :, None] * c1[None] - s2[:, :, None] * s1[None]
    s_ = s2[:, :, None] * c1[None] + c2[:, :, None] * s1[None]
    fwd = jnp.concatenate([c, -s_], axis=1).astype(bf16)
    scale = 1.0 / (N2 * H)
    ct, st = jnp.swapaxes(c, 1, 2), jnp.swapaxes(s_, 1, 2)
    inv = (jnp.concatenate([ct, -st], axis=2) * scale).astype(bf16)
    return fwd, inv


def _stage2_tables():
    j = jnp.arange(N2, dtype=jnp.int32)
    th = ((j[:, None] * j[None, :]) % N2).astype(f32) * (2.0 * math.pi / N2)
    c, s = jnp.cos(th), jnp.sin(th)
    fwd = jnp.concatenate([jnp.concatenate([c, s], 1), jnp.concatenate([-s, c], 1)], 0)
    inv = jnp.concatenate([jnp.concatenate([c, -s], 1), jnp.concatenate([s, c], 1)], 0)
    return fwd.astype(bf16), inv.astype(bf16)


def _shortconv_kernel(x_ref, w_ref, b_ref, o_ref, s_ref, *, rows):
    L = x_ref.shape[0]
    zeros8 = jnp.zeros((8, x_ref.shape[1]), f32)
    s_ref[0:8, :] = zeros8
    s_ref[L + 8:L + 16, :] = zeros8

    def load(i, c):
        r0 = pl.multiple_of(i * rows, rows)
        s_ref[pl.ds(r0 + 8, rows), :] = x_ref[pl.ds(r0, rows), :].astype(f32)
        return c

    lax.fori_loop(0, L // rows, load, 0)
    w0, w1, w2, b = w_ref[0:1, :], w_ref[1:2, :], w_ref[2:3, :], b_ref[...]

    def conv(i, c):
        r0 = pl.multiple_of(i * rows, rows)
        prev = s_ref[pl.ds(r0 + 7, rows), :]
        cur = s_ref[pl.ds(r0 + 8, rows), :]
        nxt = s_ref[pl.ds(r0 + 9, rows), :]
        o_ref[pl.ds(r0, rows), :] = (prev * w0 + cur * w1 + nxt * w2 + b).astype(o_ref.dtype)
        return c

    lax.fori_loop(0, L // rows, conv, 0)


def short_conv(p3, w, b, col0, ncols, out_dtype, *, rows=256):
    B, L, _ = p3.shape
    cb = col0 // 128
    return pl.pallas_call(
        functools.partial(_shortconv_kernel, rows=rows),
        grid=(B, ncols // 128),
        in_specs=[
            pl.BlockSpec((None, L, 128), lambda b_, j: (b_, 0, cb + j)),
            pl.BlockSpec((3, 128), lambda b_, j: (0, cb + j)),
            pl.BlockSpec((1, 128), lambda b_, j: (0, cb + j)),
        ],
        out_specs=pl.BlockSpec((None, L, 128), lambda b_, j: (b_, 0, j)),
        out_shape=jax.ShapeDtypeStruct((B, L, ncols), out_dtype),
        scratch_shapes=[pltpu.VMEM((L + 16, 128), f32)],
        compiler_params=_params("parallel", "parallel"),
        name="short_conv",
    )(p3, w, b)


def _dft1_kernel(tab_ref, z_ref, o_ref, *, n2b, K):
    g = pl.program_id(2)
    rows = [z_ref[pl.ds(pl.multiple_of(n1 * N2 + g * n2b, n2b), n2b), :] for n1 in range(K)]
    zs = pltpu.einshape("kqc->qkc", jnp.stack(rows, axis=0)).astype(bf16)
    for q in range(n2b):
        o_ref[q] = jnp.dot(tab_ref[g * n2b + q], zs[q], preferred_element_type=f32).astype(o_ref.dtype)


def dft_stage1(z, tab, *, n2b=16):
    B, Lz, C = z.shape
    _, M, K = tab.shape
    assert Lz == K * N2
    return pl.pallas_call(
        functools.partial(_dft1_kernel, n2b=n2b, K=K),
        grid=(B, C // 128, N2 // n2b),
        in_specs=[
            pl.BlockSpec((N2, M, K), lambda b_, c, g: (0, 0, 0)),
            pl.BlockSpec((None, Lz, 128), lambda b_, c, g: (b_, 0, c)),
        ],
        out_specs=pl.BlockSpec((None, n2b, M, 128), lambda b_, c, g: (b_, g, 0, c)),
        out_shape=jax.ShapeDtypeStruct((B, N2, M, C), bf16),
        compiler_params=_params("parallel", "parallel", "arbitrary"),
        name="dft_stage1",
    )(tab, z)


def _mid_kernel(m2_ref, m2i_ref, a_ref, kf_ref, o_ref, *, k1b):
    are = pltpu.einshape("nkc->knc", a_ref[:, 0])
    aim = pltpu.einshape("nkc->knc", a_ref[:, 1])
    bre, bim = [], []
    for q in range(k1b):
        x = jnp.dot(m2_ref[...], jnp.concatenate([are[q], aim[q]], axis=0), preferred_element_type=f32)
        xr, xi = x[:N2], x[N2:]
        kf = kf_ref[q].astype(f32)
        kr, ki = kf[:N2], kf[N2:]
        y = jnp.concatenate([xr * kr - xi * ki, xr * ki + xi * kr], axis=0).astype(bf16)
        bq = jnp.dot(m2i_ref[...], y, preferred_element_type=f32).astype(bf16)
        bre.append(bq[:N2])
        bim.append(bq[N2:])
    o_ref[:, 0] = pltpu.einshape("knc->nkc", jnp.stack(bre, axis=0))
    o_ref[:, 1] = pltpu.einshape("knc->nkc", jnp.stack(bim, axis=0))


def spectral_mid(a, kf, m2, m2i, order, *, k1b=16, cb=256):
    B, _, _, H, C = a.shape
    k1b = min(k1b, H)
    cpo = C // cb
    return pl.pallas_call(
        functools.partial(_mid_kernel, k1b=k1b),
        grid=(B, cpo, H // k1b),
        in_specs=[
            pl.BlockSpec((2 * N2, 2 * N2), lambda b_, c, k: (0, 0)),
            pl.BlockSpec((2 * N2, 2 * N2), lambda b_, c, k: (0, 0)),
            pl.BlockSpec((None, N2, 2, k1b, cb), lambda b_, c, k: (b_, 0, 0, k, c)),
            pl.BlockSpec((k1b, 2 * N2, cb), lambda b_, c, k: (k, 0, order * cpo + c)),
        ],
        out_specs=pl.BlockSpec((None, N2, 2, k1b, cb), lambda b_, c, k: (b_, 0, 0, k, c)),
        out_shape=jax.ShapeDtypeStruct(a.shape, bf16),
        compiler_params=_params("parallel", "parallel", "arbitrary"),
        name="spectral_mid",
    )(m2, m2i, a, kf)


def _idft1_gate_kernel(tab_ref, b_ref, x_ref, v_ref, d_ref, o_ref, y_ref, *, n2b, H, rows):
    g = pl.program_id(2)
    ys = [jnp.dot(tab_ref[g * n2b + q], b_ref[q], preferred_element_type=f32) for q in range(n2b)]
    yt = pltpu.einshape("qhc->hqc", jnp.stack(ys, axis=0))
    for n1 in range(H):
        y_ref[pl.ds(pl.multiple_of(n1 * N2 + g * n2b, n2b), n2b), :] = yt[n1]

    @pl.when(g == pl.num_programs(2) - 1)
    def _():
        d = d_ref[...]

        def gate(i, c):
            r = pl.ds(pl.multiple_of(i * rows, rows), rows)
            o_ref[r, :] = (x_ref[r, :].astype(f32) * (y_ref[r, :] + v_ref[r, :].astype(f32) * d)).astype(o_ref.dtype)
            return c

        lax.fori_loop(0, (H * N2) // rows, gate, 0)


def idft1_gate(bm, tab, xg, xcol0, v, d, *, n2b=16, rows=256):
    B, _, M, C = bm.shape
    H = M // 2
    L = H * N2
    xb = xcol0 // 128
    return pl.pallas_call(
        functools.partial(_idft1_gate_kernel, n2b=n2b, H=H, rows=rows),
        grid=(B, C // 128, N2 // n2b),
        in_specs=[
            pl.BlockSpec((N2, H, M), lambda b_, c, g: (0, 0, 0)),
            pl.BlockSpec((None, n2b, M, 128), lambda b_, c, g: (b_, g, 0, c)),
            pl.BlockSpec((None, L, 128), lambda b_, c, g: (b_, 0, xb + c)),
            pl.BlockSpec((None, L, 128), lambda b_, c, g: (b_, 0, c)),
            pl.BlockSpec((1, 128), lambda b_, c, g: (0, c)),
        ],
        out_specs=pl.BlockSpec((None, L, 128), lambda b_, c, g: (b_, 0, c)),
        out_shape=jax.ShapeDtypeStruct((B, L, C), bf16),
        scratch_shapes=[pltpu.VMEM((L, 128), f32)],
        compiler_params=_params("parallel", "parallel", "arbitrary"),
        name="idft1_gate",
    )(tab, bm, xg, v, d)


def _filter_mlp_kernel(feat_ref, w1_ref, b1_ref, w2_ref, b2_ref, w3_ref, b3_ref, w4a_ref, w4b_ref, fr_ref, dl_ref,
                       pq_ref, nrm_ref):
    hp = lax.Precision.HIGHEST
    i = pl.program_id(0)
    feat = feat_ref[...]
    fr = fr_ref[...]
    h = jnp.sin(fr * (jnp.dot(feat, w1_ref[...], precision=hp, preferred_element_type=f32) + b1_ref[...]))
    h = jnp.sin(fr * (jnp.dot(h, w2_ref[...], precision=hp, preferred_element_type=f32) + b2_ref[...]))
    h = jnp.sin(fr * (jnp.dot(h, w3_ref[...], precision=hp, preferred_element_type=f32) + b3_ref[...]))
    h_hi = h.astype(bf16)
    h_lo = (h - h_hi.astype(f32)).astype(bf16)
    h_cat = jnp.concatenate([h_hi, h_lo], axis=1)

    def proj(cols):
        return (jnp.dot(h_cat, w4a_ref[:, cols], preferred_element_type=f32)
                + jnp.dot(h_hi, w4b_ref[:, cols], preferred_element_type=f32))

    win = jnp.exp(-feat[:, 0:1] * dl_ref[...])
    row = lax.broadcasted_iota(jnp.int32, win.shape, 0) + i * feat.shape[0]
    C = D_HYENA
    nrm = []
    for o in range(HYENA_ORDER):
        c0 = o * N_DIR * C
        fwd = proj(slice(c0, c0 + C)) * win
        bwd = proj(slice(c0 + C, c0 + 2 * C)) * win
        bwd = jnp.where(row == 0, 0.0, bwd)
        pq_ref[0, :, o * C:(o + 1) * C] = (fwd + bwd).astype(pq_ref.dtype)
        pq_ref[1, :, o * C:(o + 1) * C] = (fwd - bwd).astype(pq_ref.dtype)
        nrm.append(jnp.sum(jnp.abs(fwd) + jnp.abs(bwd), axis=0, keepdims=True))
    nrm = jnp.concatenate(nrm, axis=1)

    @pl.when(i == 0)
    def _():
        nrm_ref[...] = nrm

    @pl.when(i != 0)
    def _():
        nrm_ref[...] += nrm


def filter_mlp(feat, w1, b1, w2, b2, w3, b3, w4a, w4b, freq, deltas, *, tm=512):
    L = feat.shape[0]
    CO = HYENA_ORDER * D_HYENA
    full = lambda a: pl.BlockSpec(a.shape, lambda i: (0,) * a.ndim)
    args = (w1, b1, w2, b2, w3, b3, w4a, w4b, freq, deltas)
    return pl.pallas_call(
        _filter_mlp_kernel,
        grid=(L // tm,),
        in_specs=[pl.BlockSpec((tm, 128), lambda i: (i, 0))] + [full(a) for a in args],
        out_specs=[pl.BlockSpec((2, tm, CO), lambda i: (0, i, 0)), pl.BlockSpec((1, CO), lambda i: (0, 0))],
        out_shape=[jax.ShapeDtypeStruct((2, L, CO), bf16), jax.ShapeDtypeStruct((1, CO), f32)],
        compiler_params=_params("arbitrary"),
        name="filter_mlp",
    )(feat, *args)


def _filter_stage2_kernel(m2_ref, ap_ref, aq_ref, nrm_ref, o_ref, *, k1b):
    inv = 1.0 / nrm_ref[...]
    pr, pi = pltpu.einshape("nkc->knc", ap_ref[:, 0]), pltpu.einshape("nkc->knc", ap_ref[:, 1])
    qr, qi = pltpu.einshape("nkc->knc", aq_ref[:, 0]), pltpu.einshape("nkc->knc", aq_ref[:, 1])
    for q in range(k1b):
        ap = jnp.concatenate([pr[q], pi[q]], axis=0)
        aq = jnp.concatenate([qr[q], qi[q]], axis=0)
        re = jnp.dot(m2_ref[0:N2, :], ap, preferred_element_type=f32) * inv
        im = jnp.dot(m2_ref[N2:2 * N2, :], aq, preferred_element_type=f32) * inv
        o_ref[q] = jnp.concatenate([re, im], axis=0).astype(o_ref.dtype)


def filter_stage2(apq, m2, nrm, *, k1b=16, cb=256):
    _, _, _, H, CO = apq.shape
    k1b = min(k1b, H)
    return pl.pallas_call(
        functools.partial(_filter_stage2_kernel, k1b=k1b),
        grid=(CO // cb, H // k1b),
        in_specs=[
            pl.BlockSpec((2 * N2, 2 * N2), lambda c, k: (0, 0)),
            pl.BlockSpec((None, N2, 2, k1b, cb), lambda c, k: (0, 0, 0, k, c)),
            pl.BlockSpec((None, N2, 2, k1b, cb), lambda c, k: (1, 0, 0, k, c)),
            pl.BlockSpec((1, cb), lambda c, k: (0, c)),
        ],
        out_specs=pl.BlockSpec((k1b, 2 * N2, cb), lambda c, k: (k, 0, c)),
        out_shape=jax.ShapeDtypeStruct((H, 2 * N2, CO), bf16),
        compiler_params=_params("parallel", "arbitrary"),
        name="filter_stage2",
    )(m2, apq, apq, nrm)


def _pad2(a, rows, cols):
    return jnp.pad(a, ((0, rows - a.shape[0]), (0, cols - a.shape[1])))


def hyena_filter(L, w1, b1, w2, b2, w3, b3, w4, freq, tab_f, m2):
    pos = jnp.arange(L, dtype=f32)[:, None]
    t = jnp.linspace(0.0, 1.0, L, dtype=f32)[:, None]
    bands = (FILTER_EMB - 1) // 2
    z = jnp.linspace(1e-4, bands - 1, bands, dtype=f32)[None, :] * ((2.0 * math.pi / L) * pos)
    feat = _pad2(jnp.concatenate([t, jnp.cos(z), -jnp.sin(z)], axis=-1), L, 128)
    deltas = jnp.abs(jnp.linspace(math.log(DECAY_TARGET) / DECAY_SLOW_PCT,
                                  math.log(DECAY_TARGET) / DECAY_FAST_PCT, D_HYENA, dtype=f32))[None, :]
    row = lambda v: _pad2(v[None, :], 1, 128)
    w4p = _pad2(w4, 128, w4.shape[1])
    w4_hi = w4p.astype(bf16)
    w4_lo = (w4p - w4_hi.astype(f32)).astype(bf16)
    pq, nrm = filter_mlp(feat, _pad2(w1, 128, 128), row(b1), _pad2(w2, 128, 128), row(b2), _pad2(w3, 128, 128),
                         row(b3), jnp.concatenate([w4_hi, w4_hi], axis=0), w4_lo, row(freq), deltas)
    apq = dft_stage1(pq, tab_f)
    H = L // N2
    return filter_stage2(apq.reshape(2, N2, 2, H, HYENA_ORDER * D_HYENA), m2, nrm)


def hyena_mixer(p3, conv_w, conv_b, hy_d, kf, tabs):
    tab_f, tab_i, m2, m2i = tabs
    B, L, _ = p3.shape
    C = D_HYENA
    H = L // N2
    cb = conv_b[None, :]
    xg = short_conv(p3, conv_w, cb, 0, 2 * C, bf16)
    z = short_conv(p3, conv_w, cb, 2 * C, C, bf16)
    for o in range(HYENA_ORDER):
        a = dft_stage1(z, tab_f).reshape(B, N2, 2, H, C)
        bm = spectral_mid(a, kf, m2, m2i, o).reshape(B, N2, 2 * H, C)
        z = idft1_gate(bm, tab_i, xg, o * C, z, hy_d[o:o + 1])
    return z


def _gelu(x):
    return 0.5 * x * (1.0 + lax.erf(x * (1.0 / math.sqrt(2.0))))


def _sgu_kernel(pu_ref, pv_ref, g_ref, b_ref, ws_ref, bs_ref, o_ref):
    v = _gelu(pv_ref[...].astype(f32))
    mu = jnp.mean(v, axis=-1, keepdims=True)
    vc = v - mu
    var = jnp.mean(vc * vc, axis=-1, keepdims=True)
    vn = (vc * lax.rsqrt(var + LN_EPS) * g_ref[...] + b_ref[...]).astype(bf16)
    for c in range(pv_ref.shape[0] // CHUNK):
        rows = slice(c * CHUNK, (c + 1) * CHUNK)
        for g in range(SGU_GROUPS):
            cols = slice(g * SGU_GROUP_DIM, (g + 1) * SGU_GROUP_DIM)
            s = jnp.dot(ws_ref[g], vn[rows, cols], preferred_element_type=f32) + bs_ref[:, g:g + 1]
            u = _gelu(pu_ref[rows, cols].astype(f32))
            o_ref[rows, cols] = (u * s).astype(o_ref.dtype)


def sgu(p, ln_g, ln_b, w_s, b_s_t, *, tm=512):
    n = p.shape[0]
    ub = OFF_SG // D_SGU
    return pl.pallas_call(
        _sgu_kernel,
        grid=(n // tm,),
        in_specs=[
            pl.BlockSpec((tm, D_SGU), lambda i: (i, ub)),
            pl.BlockSpec((tm, D_SGU), lambda i: (i, ub + 1)),
            pl.BlockSpec((1, D_SGU), lambda i: (0, 0)),
            pl.BlockSpec((1, D_SGU), lambda i: (0, 0)),
            pl.BlockSpec((SGU_GROUPS, CHUNK, CHUNK), lambda i: (0, 0, 0)),
            pl.BlockSpec((CHUNK, SGU_GROUPS), lambda i: (0, 0)),
        ],
        out_specs=pl.BlockSpec((tm, D_SGU), lambda i: (i, 0)),
        out_shape=jax.ShapeDtypeStruct((n, D_SGU), bf16),
        compiler_params=_params("parallel"),
        name="sgu",
    )(p, p, ln_g, ln_b, w_s, b_s_t)


def kernel(x_prompt, x_sample, norm_mix_g, w_in, hy_conv_w, hy_conv_b, flt_w1, flt_b1, flt_w2, flt_b2,
           flt_w3, flt_b3, flt_w4, flt_sin_freq, hy_d, sg_ln_g, sg_ln_b, sg_w, sg_b, w_branch_hy,
           w_branch_sg, w_out, norm_moe_g, w_router, w_gate, w_up, w_down, norm_final_g):
    w_in_b = w_in[0].astype(bf16)
    w_bh_b = w_branch_hy[0].astype(bf16)
    w_bs_b = w_branch_sg[0].astype(bf16)
    w_out_b = w_out[0].astype(bf16)
    wg_b = w_gate[0].astype(bf16)
    wu_b = w_up[0].astype(bf16)
    wd_b = w_down[0].astype(bf16)
    g_mix = norm_mix_g[0].reshape(1, D_MODEL)
    sg_w_b = sg_w[0].astype(bf16)
    sg_b_t = sg_b[0].T

    m2, m2i = _stage2_tables()

    def mixer(x):
        B, L, D = x.shape
        n = B * L
        xt = x.reshape(n, D)
        p = in_proj(xt, g_mix, w_in_b)
        p3 = p.reshape(B, L, D_IN)
        tab_f, tab_i = _stage1_tables(L // N2)
        kf = hyena_filter(L, flt_w1[0], flt_b1[0], flt_w2[0], flt_b2[0], flt_w3[0], flt_b3[0],
                          flt_w4[0], flt_sin_freq[0], tab_f, m2)
        y_hy = hyena_mixer(p3, hy_conv_w[0], hy_conv_b[0], hy_d[0], kf,
                           (tab_f, tab_i, m2, m2i)).reshape(n, D_HYENA)
        y_sg = sgu(p, sg_ln_g, sg_ln_b, sg_w_b, sg_b_t)
        merged = branch_merge(y_hy, y_sg, w_bh_b, w_bs_b, p)
        return out_proj(merged, w_out_b, xt)

    g_moe = norm_moe_g[0].reshape(1, D_MODEL)
    g_fin = norm_final_g.reshape(1, D_MODEL)
    wr_t = w_router[0].T
    outs = []
    for x in (x_prompt, x_sample):
        y = moe_block(mixer(x), g_moe, wr_t, wg_b, wu_b, wd_b, g_fin)
        outs.append(y.reshape(x.shape))
    return tuple(outs)
```

```python
import functools
import math

import jax
import jax.numpy as jnp
from jax import lax
from jax.experimental import pallas as pl
from jax.experimental.pallas import tpu as pltpu

f32 = jnp.float32
bf16 = jnp.bfloat16
i32 = jnp.int32

D_MODEL = 2048
D_HYENA = D_MODEL // 2
HYENA_ORDER = 2
N_DIR = 2
FILTER_EMB = 33
DECAY_FAST_PCT = 0.3
DECAY_SLOW_PCT = 1.5
DECAY_TARGET = 1e-2
D_SGU = D_MODEL // 2
CHUNK = 128
SGU_GROUPS = 8
SGU_GROUP_DIM = D_SGU // SGU_GROUPS
N_EXPERTS = 16
EC_CAPACITY = 2
NORM_EPS = 1e-6
LN_EPS = 1e-5
OFF_SG = 3 * D_HYENA
OFF_GH = OFF_SG + 2 * D_SGU
OFF_GS = OFF_GH + D_MODEL
D_IN = OFF_GS + D_MODEL

VMEM_LIMIT_BYTES = 56 * 1024 * 1024


def _params(*sem):
    return pltpu.CompilerParams(dimension_semantics=sem, vmem_limit_bytes=VMEM_LIMIT_BYTES)


def _inproj_kernel(x_ref, g_ref, w_ref, o_ref, xn_ref):
    @pl.when(pl.program_id(1) == 0)
    def _():
        x = x_ref[...]
        ms = jnp.mean(x * x, axis=-1, keepdims=True)
        xn_ref[...] = (x * lax.rsqrt(ms + NORM_EPS) * g_ref[...]).astype(bf16)

    o_ref[...] = jnp.dot(xn_ref[...], w_ref[...], preferred_element_type=f32).astype(o_ref.dtype)


def in_proj(x, g, w, *, tm=1024, tn=2304):
    n, d = x.shape
    dn = w.shape[1]
    return pl.pallas_call(
        _inproj_kernel,
        grid=(n // tm, dn // tn),
        in_specs=[
            pl.BlockSpec((tm, d), lambda i, j: (i, 0)),
            pl.BlockSpec((1, d), lambda i, j: (0, 0)),
            pl.BlockSpec((d, tn), lambda i, j: (0, j)),
        ],
        out_specs=pl.BlockSpec((tm, tn), lambda i, j: (i, j)),
        out_shape=jax.ShapeDtypeStruct((n, dn), bf16),
        scratch_shapes=[pltpu.VMEM((tm, d), bf16)],
        compiler_params=_params("parallel", "arbitrary"),
        name="in_proj",
    )(x, g, w)


def _merge_kernel(yh_ref, ys_ref, wbh_ref, wbs_ref, gh_ref, gs_ref, o_ref):
    a = jnp.dot(yh_ref[...], wbh_ref[...], preferred_element_type=f32)
    b = jnp.dot(ys_ref[...], wbs_ref[...], preferred_element_type=f32)
    o = jax.nn.sigmoid(gh_ref[...].astype(f32)) * a + jax.nn.sigmoid(gs_ref[...].astype(f32)) * b
    o_ref[...] = o.astype(o_ref.dtype)


def branch_merge(y_hy, y_sg, w_bh, w_bs, p, *, tm=1024, tn=1024):
    n, dh = y_hy.shape
    d = w_bh.shape[1]
    gh0 = OFF_GH // tn
    gs0 = OFF_GS // tn
    return pl.pallas_call(
        _merge_kernel,
        grid=(n // tm, d // tn),
        in_specs=[
            pl.BlockSpec((tm, dh), lambda i, j: (i, 0)),
            pl.BlockSpec((tm, dh), lambda i, j: (i, 0)),
            pl.BlockSpec((dh, tn), lambda i, j: (0, j)),
            pl.BlockSpec((dh, tn), lambda i, j: (0, j)),
            pl.BlockSpec((tm, tn), lambda i, j: (i, gh0 + j)),
            pl.BlockSpec((tm, tn), lambda i, j: (i, gs0 + j)),
        ],
        out_specs=pl.BlockSpec((tm, tn), lambda i, j: (i, j)),
        out_shape=jax.ShapeDtypeStruct((n, d), bf16),
        compiler_params=_params("parallel", "arbitrary"),
        name="branch_merge",
    )(y_hy, y_sg, w_bh, w_bs, p, p)


def _outproj_kernel(m_ref, w_ref, x_ref, o_ref):
    o_ref[...] = x_ref[...] + jnp.dot(m_ref[...], w_ref[...], preferred_element_type=f32)


def out_proj(merged, w, x, *, tm=1024, tn=1024):
    n, d = merged.shape
    dn = w.shape[1]
    return pl.pallas_call(
        _outproj_kernel,
        grid=(n // tm, dn // tn),
        in_specs=[
            pl.BlockSpec((tm, d), lambda i, j: (i, 0)),
            pl.BlockSpec((d, tn), lambda i, j: (0, j)),
            pl.BlockSpec((tm, tn), lambda i, j: (i, j)),
        ],
        out_specs=pl.BlockSpec((tm, tn), lambda i, j: (i, j)),
        out_shape=jax.ShapeDtypeStruct((n, dn), f32),
        compiler_params=_params("parallel", "arbitrary"),
        name="out_proj",
    )(merged, w, x)


ROUTE_TILE = 256
SLOT_CHUNK = 48
LANES = 128
ROW_TILES = D_MODEL // LANES
BF16_ROWS = 16
READ_CHUNK = SLOT_CHUNK + BF16_ROWS


def _rms(x, g):
    return x * lax.rsqrt(jnp.mean(x * x, axis=-1, keepdims=True) + NORM_EPS) * g


def _router_kernel(x_ref, g_ref, wh_ref, wl_ref, o_ref):
    xn = _rms(x_ref[...], g_ref[...])
    x_hi = xn.astype(bf16)
    x_lo = (xn - x_hi.astype(f32)).astype(bf16)
    nt = (((1,), (1,)), ((), ()))
    logits = (lax.dot_general(wh_ref[...], x_hi, nt, preferred_element_type=f32)
              + lax.dot_general(wl_ref[...], x_hi, nt, preferred_element_type=f32)
              + lax.dot_general(wh_ref[...], x_lo, nt, preferred_element_type=f32))
    m = jnp.max(logits, axis=0, keepdims=True)
    e = jnp.exp(logits - m)
    o_ref[...] = e / jnp.sum(e, axis=0, keepdims=True)


def router_probs(xm, g, wr_t, *, tm=512):
    n, d = xm.shape
    w_hi = wr_t.astype(bf16)
    w_lo = (wr_t - w_hi.astype(f32)).astype(bf16)
    wspec = pl.BlockSpec((N_EXPERTS, d), lambda i: (0, 0))
    return pl.pallas_call(
        _router_kernel,
        grid=(n // tm,),
        in_specs=[pl.BlockSpec((tm, d), lambda i: (i, 0)), pl.BlockSpec((1, d), lambda i: (0, 0)), wspec, wspec],
        out_specs=pl.BlockSpec((N_EXPERTS, tm), lambda i: (0, i)),
        out_shape=jax.ShapeDtypeStruct((N_EXPERTS, n), f32),
        compiler_params=_params("parallel"),
        name="router_probs",
    )(xm, g, w_hi, w_lo)


def _select_kernel(p_ref, pos_ref, gate_ref, offs_ref, *, cap):
    p = p_ref[...]
    R = p.shape[0]
    bits = pltpu.bitcast(p, i32)

    def body(i, prefix):
        cand = prefix | lax.shift_left(jnp.int32(1), 30 - i)
        cnt = jnp.sum((bits >= cand).astype(i32))
        return jnp.where(cnt >= cap, cand, prefix)

    thr = lax.fori_loop(0, 31, body, jnp.int32(0))
    gt = bits > thr
    eq = bits == thr
    need = cap - jnp.sum(gt.astype(i32))

    li = lax.broadcasted_iota(i32, (LANES, LANES), 0)
    lj = lax.broadcasted_iota(i32, (LANES, LANES), 1)
    upper = (li <= lj).astype(bf16)
    ri = lax.broadcasted_iota(i32, (R, R), 0)
    rj = lax.broadcasted_iota(i32, (R, R), 1)
    rows_before = (rj < ri).astype(bf16)

    def prefix_counts(mask):
        m = mask.astype(bf16)
        incl = jnp.dot(m, upper, preferred_element_type=f32)
        tot = jnp.broadcast_to(incl[:, LANES - 1:LANES], (R, LANES)).astype(bf16)
        row_off = jnp.dot(rows_before, tot, preferred_element_type=f32)
        return incl - mask.astype(f32) + row_off, row_off

    eq_excl, _ = prefix_counts(eq)
    sel = gt | (eq & (eq_excl < need.astype(f32)))
    pos, row_off = prefix_counts(sel)
    pos_ref[...] = jnp.where(sel, pos.astype(i32), -1)
    gate_ref[...] = jnp.where(sel, p, 0.0)
    offs_ref[...] = row_off[:, 0:1].astype(i32)


def select_tokens(probs3, cap):
    E, R, _ = probs3.shape
    blk = pl.BlockSpec((None, R, LANES), lambda e: (e, 0, 0))
    return pl.pallas_call(
        functools.partial(_select_kernel, cap=cap),
        grid=(E,),
        in_specs=[blk],
        out_specs=[blk, blk, pl.BlockSpec((None, R, 1), lambda e: (e, 0, 0))],
        out_shape=[jax.ShapeDtypeStruct((E, R, LANES), i32), jax.ShapeDtypeStruct((E, R, LANES), f32),
                   jax.ShapeDtypeStruct((E, R, 1), i32)],
        compiler_params=_params("parallel"),
        name="select_tokens",
    )(probs3)


def _onehot_t(pos, start, width):
    j = lax.broadcasted_iota(i32, (width, pos.shape[1]), 0)
    return j == (pos - start)


def _dispatch_kernel(pos0_ref, x_ref, g_ref, pos_ref, gate_ref, xe_ref, ge_ref, xbuf_ref, gbuf_ref, sem_ref,
                     *, cap, n_tiles):
    i = pl.program_id(0)
    E, tm = pos_ref.shape
    slot = i % 2

    xn = _rms(x_ref[...], g_ref[...]).astype(bf16)
    eye = (lax.broadcasted_iota(i32, (E, LANES), 0) == lax.broadcasted_iota(i32, (E, LANES), 1)).astype(f32)
    gt = lax.dot_general(gate_ref[...], eye, (((0,), (0,)), ((), ())), precision=lax.Precision.HIGHEST,
                         preferred_element_type=f32)
    g_hi = gt.astype(bf16)
    g_lo = (gt - g_hi.astype(f32)).astype(bf16)
    pos = pos_ref[...]

    def base(e, tile):
        return pos0_ref[e * (n_tiles + 1) + tile]

    def count(e):
        return base(e, i + 1) - base(e, i)

    def build(c, dst):
        oh = jnp.concatenate([_onehot_t(pos[e:e + 1, :], base(e, i) + c * SLOT_CHUNK, SLOT_CHUNK)
                              for e in range(E)], axis=0)
        oh = jnp.where(oh, 1.0, 0.0).astype(bf16)
        xr = jnp.dot(oh, xn, preferred_element_type=f32).astype(bf16)
        tiles = jnp.stack([xr[:, s * LANES:(s + 1) * LANES] for s in range(ROW_TILES)], axis=0)
        xbuf_ref[dst] = pltpu.einshape("srl->rsl", tiles)
        gbuf_ref[dst, :, 0, :] = (jnp.dot(oh, g_hi, preferred_element_type=f32)
                                  + jnp.dot(oh, g_lo, preferred_element_type=f32))

    def copies(e, c, src, tile):
        rows = pl.ds(e * SLOT_CHUNK, SLOT_CHUNK)
        out_rows = pl.ds(base(e, tile) + c * SLOT_CHUNK, SLOT_CHUNK)
        return (pltpu.make_async_copy(xbuf_ref.at[src, rows], xe_ref.at[e, out_rows], sem_ref.at[0, src]),
                pltpu.make_async_copy(gbuf_ref.at[src, rows], ge_ref.at[e, out_rows], sem_ref.at[1, src]))

    @pl.when(i == 0)
    def _():
        xbuf_ref[2] = jnp.zeros(xbuf_ref.shape[1:], bf16)
        gbuf_ref[2] = jnp.zeros(gbuf_ref.shape[1:], f32)
        for e in range(E):
            rows = pl.ds(e * SLOT_CHUNK, SLOT_CHUNK)
            pad = pl.ds(cap, SLOT_CHUNK)
            for cp in (pltpu.make_async_copy(xbuf_ref.at[2, rows], xe_ref.at[e, pad], sem_ref.at[0, 2]),
                       pltpu.make_async_copy(gbuf_ref.at[2, rows], ge_ref.at[e, pad], sem_ref.at[1, 2])):
                cp.start()
                cp.wait()

    build(0, slot)

    @pl.when(i > 0)
    def _():
        for e in range(E):
            for cp in copies(e, 0, 1 - slot, i - 1):
                cp.wait()

    for e in range(E):
        for cp in copies(e, 0, slot, i):
            cp.start()

    nch = jnp.int32(0)
    for e in range(E):
        nch = jnp.maximum(nch, (count(e) + SLOT_CHUNK - 1) // SLOT_CHUNK)

    def overflow(c, carry):
        build(c, 2)
        for e in range(E):
            @pl.when(count(e) > c * SLOT_CHUNK)
            def _():
                for cp in copies(e, c, 2, i):
                    cp.start()
                    cp.wait()
        return carry

    lax.fori_loop(1, nch, overflow, 0)

    @pl.when(i == n_tiles - 1)
    def _():
        for e in range(E):
            for cp in copies(e, 0, slot, i):
                cp.wait()


def dispatch(xm, g, pos, gates, pos0, cap):
    n, d = xm.shape
    E = pos.shape[0]
    n_tiles = n // ROUTE_TILE
    rows = cap + SLOT_CHUNK
    grid_spec = pltpu.PrefetchScalarGridSpec(
        num_scalar_prefetch=1,
        grid=(n_tiles,),
        in_specs=[
            pl.BlockSpec((ROUTE_TILE, d), lambda i, p0: (i, 0)),
            pl.BlockSpec((1, d), lambda i, p0: (0, 0)),
            pl.BlockSpec((E, ROUTE_TILE), lambda i, p0: (0, i)),
            pl.BlockSpec((E, ROUTE_TILE), lambda i, p0: (0, i)),
        ],
        out_specs=[pl.BlockSpec(memory_space=pl.ANY), pl.BlockSpec(memory_space=pl.ANY)],
        scratch_shapes=[pltpu.VMEM((3, E * SLOT_CHUNK, ROW_TILES, LANES), bf16),
                        pltpu.VMEM((3, E * SLOT_CHUNK, 1, LANES), f32),
                        pltpu.SemaphoreType.DMA((2, 3))],
    )
    return pl.pallas_call(
        functools.partial(_dispatch_kernel, cap=cap, n_tiles=n_tiles),
        grid_spec=grid_spec,
        out_shape=[jax.ShapeDtypeStruct((E, rows, ROW_TILES, LANES), bf16),
                   jax.ShapeDtypeStruct((E, rows, 1, LANES), f32)],
        compiler_params=_params("arbitrary"),
        name="dispatch",
    )(pos0, xm, g, pos, gates)


def _expert_up_kernel(xe_ref, wg_ref, wu_ref, h_ref, xb_ref):
    @pl.when(pl.program_id(2) == 0)
    def _():
        tiles = pltpu.einshape("rsl->srl", xe_ref[...])
        for s in range(ROW_TILES):
            xb_ref[:, s * LANES:(s + 1) * LANES] = tiles[s]

    xb = xb_ref[...]
    hg = jnp.dot(xb, wg_ref[...], preferred_element_type=f32)
    hu = jnp.dot(xb, wu_ref[...], preferred_element_type=f32)
    h_ref[...] = (hg * jax.nn.sigmoid(hg) * hu).astype(h_ref.dtype)


def _expert_down_kernel(h_ref, ge_ref, wd_ref, o_ref):
    gl = ge_ref[:, 0, :]
    lane = lax.broadcasted_iota(i32, gl.shape, 1)
    gate = jnp.sum(jnp.where(lane == pl.program_id(0), gl, 0.0), axis=-1, keepdims=True)
    o_ref[...] = (jnp.dot(h_ref[...], wd_ref[...], preferred_element_type=f32) * gate).astype(o_ref.dtype)


def expert_ffn(xe, ge, wg, wu, wd, cap, *, tm=1024, tf=1024, tn=1024):
    E = xe.shape[0]
    d = D_MODEL
    dff = wg.shape[2]
    tm = min(tm, cap)
    h = pl.pallas_call(
        _expert_up_kernel,
        grid=(E, cap // tm, dff // tf),
        in_specs=[
            pl.BlockSpec((None, tm, ROW_TILES, LANES), lambda e_, i, f: (e_, i, 0, 0)),
            pl.BlockSpec((None, d, tf), lambda e_, i, f: (e_, 0, f)),
            pl.BlockSpec((None, d, tf), lambda e_, i, f: (e_, 0, f)),
        ],
        out_specs=pl.BlockSpec((None, tm, tf), lambda e_, i, f: (e_, i, f)),
        out_shape=jax.ShapeDtypeStruct((E, cap, dff), bf16),
        scratch_shapes=[pltpu.VMEM((tm, d), bf16)],
        compiler_params=_params("parallel", "parallel", "arbitrary"),
        name="expert_up",
    )(xe, wg, wu)
    return pl.pallas_call(
        _expert_down_kernel,
        grid=(E, cap // tm, d // tn),
        in_specs=[
            pl.BlockSpec((None, tm, dff), lambda e_, i, j: (e_, i, 0)),
            pl.BlockSpec((None, tm, 1, LANES), lambda e_, i, j: (e_, i, 0, 0)),
            pl.BlockSpec((None, dff, tn), lambda e_, i, j: (e_, 0, j)),
        ],
        out_specs=pl.BlockSpec((None, tm, tn), lambda e_, i, j: (e_, i, j)),
        out_shape=jax.ShapeDtypeStruct((E, cap, d), bf16),
        compiler_params=_params("parallel", "parallel", "arbitrary"),
        name="expert_down",
    )(h, ge, wd)


def _combine_kernel(pos0_ref, x_ref, g_ref, pos_ref, ye_ref, o_ref, buf_ref, sem_ref, *, cap, n_tiles):
    i = pl.program_id(0)
    E, tm = pos_ref.shape
    slot = i % 2

    def base(e, tile):
        return pos0_ref[e * (n_tiles + 1) + tile]

    def count(e):
        return base(e, i + 1) - base(e, i)

    def start_row(e, c, tile):
        st = base(e, tile) + c * SLOT_CHUNK
        st = jnp.minimum((st // BF16_ROWS) * BF16_ROWS, cap - READ_CHUNK)
        return pl.multiple_of(st, BF16_ROWS)

    def copy(e, c, dst, tile):
        return pltpu.make_async_copy(ye_ref.at[e, pl.ds(start_row(e, c, tile), READ_CHUNK)],
                                     buf_ref.at[dst, pl.ds(e * READ_CHUNK, READ_CHUNK)], sem_ref.at[dst])

    @pl.when(i == 0)
    def _():
        for e in range(E):
            copy(e, 0, slot, i).start()

    @pl.when(i + 1 < n_tiles)
    def _():
        for e in range(E):
            copy(e, 0, 1 - slot, i + 1).start()

    pos = pos_ref[...]

    def onehot(c):
        parts = []
        for e in range(E):
            pe = pos[e:e + 1, :]
            local = pe - base(e, i)
            keep = (local >= c * SLOT_CHUNK) & (local < (c + 1) * SLOT_CHUNK) & (pe >= 0)
            oh = _onehot_t(pe, start_row(e, c, i), READ_CHUNK) & keep
            parts.append(jnp.where(oh, 1.0, 0.0).astype(bf16))
        return jnp.concatenate(parts, axis=0)

    for e in range(E):
        copy(e, 0, slot, i).wait()
    acc = lax.dot_general(onehot(0), buf_ref[slot], (((0,), (0,)), ((), ())), preferred_element_type=f32)

    nch = jnp.int32(0)
    for e in range(E):
        nch = jnp.maximum(nch, (count(e) + SLOT_CHUNK - 1) // SLOT_CHUNK)

    def overflow(c, acc):
        for e in range(E):
            cp = copy(e, c, 2, i)
            cp.start()
            cp.wait()
        return acc + lax.dot_general(onehot(c), buf_ref[2], (((0,), (0,)), ((), ())), preferred_element_type=f32)

    acc = lax.fori_loop(1, nch, overflow, acc)
    o_ref[...] = _rms(x_ref[...] + acc, g_ref[...])


def combine(xm, g_final, pos, ye, pos0, cap):
    n, d = xm.shape
    E = pos.shape[0]
    n_tiles = n // ROUTE_TILE
    grid_spec = pltpu.PrefetchScalarGridSpec(
        num_scalar_prefetch=1,
        grid=(n_tiles,),
        in_specs=[
            pl.BlockSpec((ROUTE_TILE, d), lambda i, p0: (i, 0)),
            pl.BlockSpec((1, d), lambda i, p0: (0, 0)),
            pl.BlockSpec((E, ROUTE_TILE), lambda i, p0: (0, i)),
            pl.BlockSpec(memory_space=pl.ANY),
        ],
        out_specs=pl.BlockSpec((ROUTE_TILE, d), lambda i, p0: (i, 0)),
        scratch_shapes=[pltpu.VMEM((3, E * READ_CHUNK, d), bf16), pltpu.SemaphoreType.DMA((3,))],
    )
    return pl.pallas_call(
        functools.partial(_combine_kernel, cap=cap, n_tiles=n_tiles),
        grid_spec=grid_spec,
        out_shape=jax.ShapeDtypeStruct((n, d), f32),
        compiler_params=_params("arbitrary"),
        name="combine",
    )(pos0, xm, g_final, pos, ye)


def moe_block(xm, g_moe, wr_t, wg, wu, wd, g_final):
    n = xm.shape[0]
    cap = EC_CAPACITY * n // N_EXPERTS
    n_tiles = n // ROUTE_TILE
    probs = router_probs(xm, g_moe, wr_t)
    pos3, gate3, offs = select_tokens(probs.reshape(N_EXPERTS, n // LANES, LANES), cap)
    pos = pos3.reshape(N_EXPERTS, n)
    gates = gate3.reshape(N_EXPERTS, n)
    tile_off = offs[:, ::ROUTE_TILE // LANES, 0]
    pos0 = jnp.concatenate([tile_off, jnp.full((N_EXPERTS, 1), cap, i32)], axis=1).reshape(-1)
    xe, ge = dispatch(xm, g_moe, pos, gates, pos0, cap)
    ye = expert_ffn(xe, ge, wg, wu, wd, cap)
    return combine(xm, g_final, pos, ye, pos0, cap)


N2 = 128


def _stage1_tables(H):
    n_total = 2 * N2 * H
    k1 = jnp.arange(H, dtype=jnp.int32)
    idx = jnp.arange(max(H, N2), dtype=jnp.int32)

    def phase(m):
        a = (m % (2 * n_total)).astype(f32) * (math.pi / n_total)
        return jnp.cos(a), jnp.sin(a)

    c1, s1 = phase((2 * k1 + 1)[:, None] * (N2 * idx[:H])[None, :])
    c2, s2 = phase((2 * k1 + 1)[None, :] * idx[:N2, None])
    c = c2[:, :, None] * c1[None] - s2[:, :, None] * s1[None]
    s_ = s2[:, :, None] * c1[None] + c2[:, :, None] * s1[None]
    fwd = jnp.concatenate([c, -s_], axis=1).astype(bf16)
    scale = 1.0 / (N2 * H)
    ct, st = jnp.swapaxes(c, 1, 2), jnp.swapaxes(s_, 1, 2)
    inv = (jnp.concatenate([ct, -st], axis=2) * scale).astype(bf16)
    return fwd, inv


def _stage2_tables():
    j = jnp.arange(N2, dtype=jnp.int32)
    th = ((j[:, None] * j[None, :]) % N2).astype(f32) * (2.0 * math.pi / N2)
    c, s = jnp.cos(th), jnp.sin(th)
    fwd = jnp.concatenate([jnp.concatenate([c, s], 1), jnp.concatenate([-s, c], 1)], 0)
    inv = jnp.concatenate([jnp.concatenate([c, -s], 1), jnp.concatenate([s, c], 1)], 0)
    return fwd.astype(bf16), inv.astype(bf16)


def _shortconv_kernel(x_ref, w_ref, b_ref, o_ref, s_ref, *, rows):
    L = x_ref.shape[0]
    zeros8 = jnp.zeros((8, x_ref.shape[1]), f32)
    s_ref[0:8, :] = zeros8
    s_ref[L + 8:L + 16, :] = zeros8

    def load(i, c):
        r0 = pl.multiple_of(i * rows, rows)
        s_ref[pl.ds(r0 + 8, rows), :] = x_ref[pl.ds(r0, rows), :].astype(f32)
        return c

    lax.fori_loop(0, L // rows, load, 0)
    w0, w1, w2, b = w_ref[0:1, :], w_ref[1:2, :], w_ref[2:3, :], b_ref[...]

    def conv(i, c):
        r0 = pl.multiple_of(i * rows, rows)
        prev = s_ref[pl.ds(r0 + 7, rows), :]
        cur = s_ref[pl.ds(r0 + 8, rows), :]
        nxt = s_ref[pl.ds(r0 + 9, rows), :]
        o_ref[pl.ds(r0, rows), :] = (prev * w0 + cur * w1 + nxt * w2 + b).astype(o_ref.dtype)
        return c

    lax.fori_loop(0, L // rows, conv, 0)


def short_conv(p3, w, b, col0, ncols, out_dtype, *, rows=256, cw=128):
    B, L, _ = p3.shape
    cb = col0 // cw
    return pl.pallas_call(
        functools.partial(_shortconv_kernel, rows=rows),
        grid=(B, ncols // cw),
        in_specs=[
            pl.BlockSpec((None, L, cw), lambda b_, j: (b_, 0, cb + j)),
            pl.BlockSpec((3, cw), lambda b_, j: (0, cb + j)),
            pl.BlockSpec((1, cw), lambda b_, j: (0, cb + j)),
        ],
        out_specs=pl.BlockSpec((None, L, cw), lambda b_, j: (b_, 0, j)),
        out_shape=jax.ShapeDtypeStruct((B, L, ncols), out_dtype),
        scratch_shapes=[pltpu.VMEM((L + 16, cw), f32)],
        compiler_params=_params("parallel", "parallel"),
        name="short_conv",
    )(p3, w, b)


def _dft1_kernel(tab_ref, z_ref, o_ref, *, n2b, K):
    g = pl.program_id(2)
    rows = [z_ref[pl.ds(pl.multiple_of(n1 * N2 + g * n2b, n2b), n2b), :] for n1 in range(K)]
    zs = pltpu.einshape("kqc->qkc", jnp.stack(rows, axis=0)).astype(bf16)
    for q in range(n2b):
        o_ref[q] = jnp.dot(tab_ref[g * n2b + q], zs[q], preferred_element_type=f32).astype(o_ref.dtype)


def dft_stage1(z, tab, *, n2b=16, cb=256):
    B, Lz, C = z.shape
    _, M, K = tab.shape
    assert Lz == K * N2
    return pl.pallas_call(
        functools.partial(_dft1_kernel, n2b=n2b, K=K),
        grid=(B, C // cb, N2 // n2b),
        in_specs=[
            pl.BlockSpec((N2, M, K), lambda b_, c, g: (0, 0, 0)),
            pl.BlockSpec((None, Lz, cb), lambda b_, c, g: (b_, 0, c)),
        ],
        out_specs=pl.BlockSpec((None, n2b, M, cb), lambda b_, c, g: (b_, g, 0, c)),
        out_shape=jax.ShapeDtypeStruct((B, N2, M, C), bf16),
        compiler_params=_params("parallel", "parallel", "arbitrary"),
        name="dft_stage1",
    )(tab, z)


def _mid_kernel(m2_ref, m2i_ref, a_ref, kf_ref, o_ref, *, k1b):
    are = pltpu.einshape("nkc->knc", a_ref[:, 0])
    aim = pltpu.einshape("nkc->knc", a_ref[:, 1])
    bre, bim = [], []
    for q in range(k1b):
        x = jnp.dot(m2_ref[...], jnp.concatenate([are[q], aim[q]], axis=0), preferred_element_type=f32)
        xr, xi = x[:N2], x[N2:]
        kf = kf_ref[q].astype(f32)
        kr, ki = kf[:N2], kf[N2:]
        y = jnp.concatenate([xr * kr - xi * ki, xr * ki + xi * kr], axis=0).astype(bf16)
        bq = jnp.dot(m2i_ref[...], y, preferred_element_type=f32).astype(bf16)
        bre.append(bq[:N2])
        bim.append(bq[N2:])
    o_ref[:, 0] = pltpu.einshape("knc->nkc", jnp.stack(bre, axis=0))
    o_ref[:, 1] = pltpu.einshape("knc->nkc", jnp.stack(bim, axis=0))


def spectral_mid(a, kf, m2, m2i, order, *, k1b=16, cb=256):
    B, _, _, H, C = a.shape
    k1b = min(k1b, H)
    cpo = C // cb
    return pl.pallas_call(
        functools.partial(_mid_kernel, k1b=k1b),
        grid=(B, cpo, H // k1b),
        in_specs=[
            pl.BlockSpec((2 * N2, 2 * N2), lambda b_, c, k: (0, 0)),
            pl.BlockSpec((2 * N2, 2 * N2), lambda b_, c, k: (0, 0)),
            pl.BlockSpec((None, N2, 2, k1b, cb), lambda b_, c, k: (b_, 0, 0, k, c)),
            pl.BlockSpec((k1b, 2 * N2, cb), lambda b_, c, k: (k, 0, order * cpo + c)),
        ],
        out_specs=pl.BlockSpec((None, N2, 2, k1b, cb), lambda b_, c, k: (b_, 0, 0, k, c)),
        out_shape=jax.ShapeDtypeStruct(a.shape, bf16),
        compiler_params=_params("parallel", "parallel", "arbitrary"),
        name="spectral_mid",
    )(m2, m2i, a, kf)


def _idft1_gate_kernel(tab_ref, b_ref, x_ref, v_ref, d_ref, o_ref, y_ref, *, n2b, H, rows):
    g = pl.program_id(2)
    ys = [jnp.dot(tab_ref[g * n2b + q], b_ref[q], preferred_element_type=f32) for q in range(n2b)]
    yt = pltpu.einshape("qhc->hqc", jnp.stack(ys, axis=0))
    for n1 in range(H):
        y_ref[pl.ds(pl.multiple_of(n1 * N2 + g * n2b, n2b), n2b), :] = yt[n1]

    @pl.when(g == pl.num_programs(2) - 1)
    def _():
        d = d_ref[...]

        def gate(i, c):
            r = pl.ds(pl.multiple_of(i * rows, rows), rows)
            o_ref[r, :] = (x_ref[r, :].astype(f32) * (y_ref[r, :] + v_ref[r, :].astype(f32) * d)).astype(o_ref.dtype)
            return c

        lax.fori_loop(0, (H * N2) // rows, gate, 0)


def idft1_gate(bm, tab, xg, xcol0, v, d, *, n2b=16, rows=256):
    B, _, M, C = bm.shape
    H = M // 2
    L = H * N2
    xb = xcol0 // 128
    return pl.pallas_call(
        functools.partial(_idft1_gate_kernel, n2b=n2b, H=H, rows=rows),
        grid=(B, C // 128, N2 // n2b),
        in_specs=[
            pl.BlockSpec((N2, H, M), lambda b_, c, g: (0, 0, 0)),
            pl.BlockSpec((None, n2b, M, 128), lambda b_, c, g: (b_, g, 0, c)),
            pl.BlockSpec((None, L, 128), lambda b_, c, g: (b_, 0, xb + c)),
            pl.BlockSpec((None, L, 128), lambda b_, c, g: (b_, 0, c)),
            pl.BlockSpec((1, 128), lambda b_, c, g: (0, c)),
        ],
        out_specs=pl.BlockSpec((None, L, 128), lambda b_, c, g: (b_, 0, c)),
        out_shape=jax.ShapeDtypeStruct((B, L, C), bf16),
        scratch_shapes=[pltpu.VMEM((L, 128), f32)],
        compiler_params=_params("parallel", "parallel", "arbitrary"),
        name="idft1_gate",
    )(tab, bm, xg, v, d)


def _filter_mlp_kernel(feat_ref, w1_ref, b1_ref, w2_ref, b2_ref, w3_ref, b3_ref, w4a_ref, w4b_ref, fr_ref, dl_ref,
                       pq_ref, nrm_ref):
    hp = lax.Precision.HIGHEST
    i = pl.program_id(0)
    feat = feat_ref[...]
    fr = fr_ref[...]
    h = jnp.sin(fr * (jnp.dot(feat, w1_ref[...], precision=hp, preferred_element_type=f32) + b1_ref[...]))
    h = jnp.sin(fr * (jnp.dot(h, w2_ref[...], precision=hp, preferred_element_type=f32) + b2_ref[...]))
    h = jnp.sin(fr * (jnp.dot(h, w3_ref[...], precision=hp, preferred_element_type=f32) + b3_ref[...]))
    h_hi = h.astype(bf16)
    h_lo = (h - h_hi.astype(f32)).astype(bf16)
    h_cat = jnp.concatenate([h_hi, h_lo], axis=1)

    def proj(cols):
        return (jnp.dot(h_cat, w4a_ref[:, cols], preferred_element_type=f32)
                + jnp.dot(h_hi, w4b_ref[:, cols], preferred_element_type=f32))

    win = jnp.exp(-feat[:, 0:1] * dl_ref[...])
    row = lax.broadcasted_iota(jnp.int32, win.shape, 0) + i * feat.shape[0]
    C = D_HYENA
    nrm = []
    for o in range(HYENA_ORDER):
        c0 = o * N_DIR * C
        fwd = proj(slice(c0, c0 + C)) * win
        bwd = proj(slice(c0 + C, c0 + 2 * C)) * win
        bwd = jnp.where(row == 0, 0.0, bwd)
        pq_ref[0, :, o * C:(o + 1) * C] = (fwd + bwd).astype(pq_ref.dtype)
        pq_ref[1, :, o * C:(o + 1) * C] = (fwd - bwd).astype(pq_ref.dtype)
        nrm.append(jnp.sum(jnp.abs(fwd) + jnp.abs(bwd), axis=0, keepdims=True))
    nrm = jnp.concatenate(nrm, axis=1)

    @pl.when(i == 0)
    def _():
        nrm_ref[...] = nrm

    @pl.when(i != 0)
    def _():
        nrm_ref[...] += nrm


def filter_mlp(feat, w1, b1, w2, b2, w3, b3, w4a, w4b, freq, deltas, *, tm=512):
    L = feat.shape[0]
    CO = HYENA_ORDER * D_HYENA
    full = lambda a: pl.BlockSpec(a.shape, lambda i: (0,) * a.ndim)
    args = (w1, b1, w2, b2, w3, b3, w4a, w4b, freq, deltas)
    return pl.pallas_call(
        _filter_mlp_kernel,
        grid=(L // tm,),
        in_specs=[pl.BlockSpec((tm, 128), lambda i: (i, 0))] + [full(a) for a in args],
        out_specs=[pl.BlockSpec((2, tm, CO), lambda i: (0, i, 0)), pl.BlockSpec((1, CO), lambda i: (0, 0))],
        out_shape=[jax.ShapeDtypeStruct((2, L, CO), bf16), jax.ShapeDtypeStruct((1, CO), f32)],
        compiler_params=_params("arbitrary"),
        name="filter_mlp",
    )(feat, *args)


def _filter_stage2_kernel(m2_ref, ap_ref, aq_ref, nrm_ref, o_ref, *, k1b):
    inv = 1.0 / nrm_ref[...]
    pr, pi = pltpu.einshape("nkc->knc", ap_ref[:, 0]), pltpu.einshape("nkc->knc", ap_ref[:, 1])
    qr, qi = pltpu.einshape("nkc->knc", aq_ref[:, 0]), pltpu.einshape("nkc->knc", aq_ref[:, 1])
    for q in range(k1b):
        ap = jnp.concatenate([pr[q], pi[q]], axis=0)
        aq = jnp.concatenate([qr[q], qi[q]], axis=0)
        re = jnp.dot(m2_ref[0:N2, :], ap, preferred_element_type=f32) * inv
        im = jnp.dot(m2_ref[N2:2 * N2, :], aq, preferred_element_type=f32) * inv
        o_ref[q] = jnp.concatenate([re, im], axis=0).astype(o_ref.dtype)


def filter_stage2(apq, m2, nrm, *, k1b=16, cb=256):
    _, _, _, H, CO = apq.shape
    k1b = min(k1b, H)
    return pl.pallas_call(
        functools.partial(_filter_stage2_kernel, k1b=k1b),
        grid=(CO // cb, H // k1b),
        in_specs=[
            pl.BlockSpec((2 * N2, 2 * N2), lambda c, k: (0, 0)),
            pl.BlockSpec((None, N2, 2, k1b, cb), lambda c, k: (0, 0, 0, k, c)),
            pl.BlockSpec((None, N2, 2, k1b, cb), lambda c, k: (1, 0, 0, k, c)),
            pl.BlockSpec((1, cb), lambda c, k: (0, c)),
        ],
        out_specs=pl.BlockSpec((k1b, 2 * N2, cb), lambda c, k: (k, 0, c)),
        out_shape=jax.ShapeDtypeStruct((H, 2 * N2, CO), bf16),
        compiler_params=_params("parallel", "arbitrary"),
        name="filter_stage2",
    )(m2, apq, apq, nrm)


def _pad2(a, rows, cols):
    return jnp.pad(a, ((0, rows - a.shape[0]), (0, cols - a.shape[1])))


def hyena_filter(L, w1, b1, w2, b2, w3, b3, w4, freq, tab_f, m2):
    pos = jnp.arange(L, dtype=f32)[:, None]
    t = jnp.linspace(0.0, 1.0, L, dtype=f32)[:, None]
    bands = (FILTER_EMB - 1) // 2
    z = jnp.linspace(1e-4, bands - 1, bands, dtype=f32)[None, :] * ((2.0 * math.pi / L) * pos)
    feat = _pad2(jnp.concatenate([t, jnp.cos(z), -jnp.sin(z)], axis=-1), L, 128)
    deltas = jnp.abs(jnp.linspace(math.log(DECAY_TARGET) / DECAY_SLOW_PCT,
                                  math.log(DECAY_TARGET) / DECAY_FAST_PCT, D_HYENA, dtype=f32))[None, :]
    row = lambda v: _pad2(v[None, :], 1, 128)
    w4p = _pad2(w4, 128, w4.shape[1])
    w4_hi = w4p.astype(bf16)
    w4_lo = (w4p - w4_hi.astype(f32)).astype(bf16)
    pq, nrm = filter_mlp(feat, _pad2(w1, 128, 128), row(b1), _pad2(w2, 128, 128), row(b2), _pad2(w3, 128, 128),
                         row(b3), jnp.concatenate([w4_hi, w4_hi], axis=0), w4_lo, row(freq), deltas)
    apq = dft_stage1(pq, tab_f)
    H = L // N2
    return filter_stage2(apq.reshape(2, N2, 2, H, HYENA_ORDER * D_HYENA), m2, nrm)


def hyena_mixer(p3, conv_w, conv_b, hy_d, kf, tabs):
    tab_f, tab_i, m2, m2i = tabs
    B, L, _ = p3.shape
    C = D_HYENA
    H = L // N2
    cb = conv_b[None, :]
    xg = short_conv(p3, conv_w, cb, 0, 2 * C, bf16)
    z = short_conv(p3, conv_w, cb, 2 * C, C, bf16)
    for o in range(HYENA_ORDER):
        a = dft_stage1(z, tab_f).reshape(B, N2, 2, H, C)
        bm = spectral_mid(a, kf, m2, m2i, o).reshape(B, N2, 2 * H, C)
        z = idft1_gate(bm, tab_i, xg, o * C, z, hy_d[o:o + 1])
    return z


def _gelu(x):
    return 0.5 * x * (1.0 + lax.erf(x * (1.0 / math.sqrt(2.0))))


def _sgu_kernel(pu_ref, pv_ref, g_ref, b_ref, ws_ref, bs_ref, o_ref):
    v = _gelu(pv_ref[...].astype(f32))
    mu = jnp.mean(v, axis=-1, keepdims=True)
    vc = v - mu
    var = jnp.mean(vc * vc, axis=-1, keepdims=True)
    vn = (vc * lax.rsqrt(var + LN_EPS) * g_ref[...] + b_ref[...]).astype(bf16)
    for c in range(pv_ref.shape[0] // CHUNK):
        rows = slice(c * CHUNK, (c + 1) * CHUNK)
        for g in range(SGU_GROUPS):
            cols = slice(g * SGU_GROUP_DIM, (g + 1) * SGU_GROUP_DIM)
            s = jnp.dot(ws_ref[g], vn[rows, cols], preferred_element_type=f32) + bs_ref[:, g:g + 1]
            u = _gelu(pu_ref[rows, cols].astype(f32))
            o_ref[rows, cols] = (u * s).astype(o_ref.dtype)


def sgu(p, ln_g, ln_b, w_s, b_s_t, *, tm=512):
    n = p.shape[0]
    ub = OFF_SG // D_SGU
    return pl.pallas_call(
        _sgu_kernel,
        grid=(n // tm,),
        in_specs=[
            pl.BlockSpec((tm, D_SGU), lambda i: (i, ub)),
            pl.BlockSpec((tm, D_SGU), lambda i: (i, ub + 1)),
            pl.BlockSpec((1, D_SGU), lambda i: (0, 0)),
            pl.BlockSpec((1, D_SGU), lambda i: (0, 0)),
            pl.BlockSpec((SGU_GROUPS, CHUNK, CHUNK), lambda i: (0, 0, 0)),
            pl.BlockSpec((CHUNK, SGU_GROUPS), lambda i: (0, 0)),
        ],
        out_specs=pl.BlockSpec((tm, D_SGU), lambda i: (i, 0)),
        out_shape=jax.ShapeDtypeStruct((n, D_SGU), bf16),
        compiler_params=_params("parallel"),
        name="sgu",
    )(p, p, ln_g, ln_b, w_s, b_s_t)


def kernel(x_prompt, x_sample, norm_mix_g, w_in, hy_conv_w, hy_conv_b, flt_w1, flt_b1, flt_w2, flt_b2,
           flt_w3, flt_b3, flt_w4, flt_sin_freq, hy_d, sg_ln_g, sg_ln_b, sg_w, sg_b, w_branch_hy,
           w_branch_sg, w_out, norm_moe_g, w_router, w_gate, w_up, w_down, norm_final_g):
    w_in_b = w_in[0].astype(bf16)
    w_bh_b = w_branch_hy[0].astype(bf16)
    w_bs_b = w_branch_sg[0].astype(bf16)
    w_out_b = w_out[0].astype(bf16)
    wg_b = w_gate[0].astype(bf16)
    wu_b = w_up[0].astype(bf16)
    wd_b = w_down[0].astype(bf16)
    g_mix = norm_mix_g[0].reshape(1, D_MODEL)
    sg_w_b = sg_w[0].astype(bf16)
    sg_b_t = sg_b[0].T

    m2, m2i = _stage2_tables()

    def mixer(x):
        B, L, D = x.shape
        n = B * L
        xt = x.reshape(n, D)
        p = in_proj(xt, g_mix, w_in_b)
        p3 = p.reshape(B, L, D_IN)
        tab_f, tab_i = _stage1_tables(L // N2)
        kf = hyena_filter(L, flt_w1[0], flt_b1[0], flt_w2[0], flt_b2[0], flt_w3[0], flt_b3[0],
                          flt_w4[0], flt_sin_freq[0], tab_f, m2)
        y_hy = hyena_mixer(p3, hy_conv_w[0], hy_conv_b[0], hy_d[0], kf,
                           (tab_f, tab_i, m2, m2i)).reshape(n, D_HYENA)
        y_sg = sgu(p, sg_ln_g, sg_ln_b, sg_w_b, sg_b_t)
        merged = branch_merge(y_hy, y_sg, w_bh_b, w_bs_b, p)
        return out_proj(merged, w_out_b, xt)

    g_moe = norm_moe_g[0].reshape(1, D_MODEL)
    g_fin = norm_final_g.reshape(1, D_MODEL)
    wr_t = w_router[0].T
    outs = []
    for x in (x_prompt, x_sample):
        y = moe_block(mixer(x), g_moe, wr_t, wg_b, wu_b, wd_b, g_fin)
        outs.append(y.reshape(x.shape))
    return tuple(outs)
```

```python
import functools
import math

import jax
import jax.numpy as jnp
from jax import lax
from jax.experimental import pallas as pl
from jax.experimental.pallas import tpu as pltpu

f32 = jnp.float32
bf16 = jnp.bfloat16
i32 = jnp.int32

D_MODEL = 2048
D_HYENA = D_MODEL // 2
HYENA_ORDER = 2
N_DIR = 2
FILTER_EMB = 33
DECAY_FAST_PCT = 0.3
DECAY_SLOW_PCT = 1.5
DECAY_TARGET = 1e-2
D_SGU = D_MODEL // 2
CHUNK = 128
SGU_GROUPS = 8
SGU_GROUP_DIM = D_SGU // SGU_GROUPS
N_EXPERTS = 16
EC_CAPACITY = 2
NORM_EPS = 1e-6
LN_EPS = 1e-5
OFF_SG = 3 * D_HYENA
OFF_GH = OFF_SG + 2 * D_SGU
OFF_GS = OFF_GH + D_MODEL
D_IN = OFF_GS + D_MODEL

VMEM_LIMIT_BYTES = 56 * 1024 * 1024


def _params(*sem):
    return pltpu.CompilerParams(dimension_semantics=sem, vmem_limit_bytes=VMEM_LIMIT_BYTES)


def _cast_job(side, n_i, n_j):
    rows = side.shape[0] // (n_i * n_j)
    assert rows * n_i * n_j == side.shape[0]
    spec = pl.BlockSpec((rows, side.shape[1]), lambda i, j: (i * n_j + j, 0))
    return spec, jax.ShapeDtypeStruct(side.shape, bf16)


def _call_with_cast(kernel, grid, in_specs, out_spec, out_shape, args, cast, **kw):
    if cast is not None:
        cspec, cshape = _cast_job(cast, *grid)
        in_specs, args = in_specs + [cspec], args + (cast,)
        out_spec, out_shape = [out_spec, cspec], [out_shape, cshape]
    res = pl.pallas_call(functools.partial(kernel, cast=cast is not None), grid=grid, in_specs=in_specs,
                         out_specs=out_spec, out_shape=out_shape, **kw)(*args)
    return tuple(res) if cast is not None else res


def _inproj_kernel(x_ref, g_ref, w_ref, *rest, cast):
    if cast:
        ci_ref, o_ref, co_ref, xn_ref = rest
        co_ref[...] = ci_ref[...].astype(bf16)
    else:
        o_ref, xn_ref = rest

    @pl.when(pl.program_id(1) == 0)
    def _():
        x = x_ref[...]
        ms = jnp.mean(x * x, axis=-1, keepdims=True)
        xn_ref[...] = (x * lax.rsqrt(ms + NORM_EPS) * g_ref[...]).astype(bf16)

    o_ref[...] = jnp.dot(xn_ref[...], w_ref[...], preferred_element_type=f32).astype(o_ref.dtype)


def in_proj(x, g, w, *, cast=None, tm=1024, tn=None):
    n, d = x.shape
    dn = w.shape[1]
    if tn is None:
        tn = dn // 8 if cast is not None else dn // 4
    return _call_with_cast(
        _inproj_kernel, (n // tm, dn // tn),
        [pl.BlockSpec((tm, d), lambda i, j: (i, 0)),
         pl.BlockSpec((1, d), lambda i, j: (0, 0)),
         pl.BlockSpec((d, tn), lambda i, j: (0, j))],
        pl.BlockSpec((tm, tn), lambda i, j: (i, j)),
        jax.ShapeDtypeStruct((n, dn), bf16),
        (x, g, w), cast,
        scratch_shapes=[pltpu.VMEM((tm, d), bf16)],
        compiler_params=_params("parallel", "arbitrary"),
        name="in_proj",
    )


def _merge_kernel(yh_ref, ys_ref, wbh_ref, wbs_ref, gh_ref, gs_ref, *rest, cast):
    if cast:
        ci_ref, o_ref, co_ref = rest
        co_ref[...] = ci_ref[...].astype(bf16)
    else:
        (o_ref,) = rest
    a = jnp.dot(yh_ref[...], wbh_ref[...], preferred_element_type=f32)
    b = jnp.dot(ys_ref[...], wbs_ref[...], preferred_element_type=f32)
    o = jax.nn.sigmoid(gh_ref[...].astype(f32)) * a + jax.nn.sigmoid(gs_ref[...].astype(f32)) * b
    o_ref[...] = o.astype(o_ref.dtype)


def branch_merge(y_hy, y_sg, w_bh, w_bs, p, *, cast=None, tm=1024, tn=1024):
    n, dh = y_hy.shape
    d = w_bh.shape[1]
    gh0 = OFF_GH // tn
    gs0 = OFF_GS // tn
    return _call_with_cast(
        _merge_kernel, (n // tm, d // tn),
        [pl.BlockSpec((tm, dh), lambda i, j: (i, 0)),
         pl.BlockSpec((tm, dh), lambda i, j: (i, 0)),
         pl.BlockSpec((dh, tn), lambda i, j: (0, j)),
         pl.BlockSpec((dh, tn), lambda i, j: (0, j)),
         pl.BlockSpec((tm, tn), lambda i, j: (i, gh0 + j)),
         pl.BlockSpec((tm, tn), lambda i, j: (i, gs0 + j))],
        pl.BlockSpec((tm, tn), lambda i, j: (i, j)),
        jax.ShapeDtypeStruct((n, d), bf16),
        (y_hy, y_sg, w_bh, w_bs, p, p), cast,
        compiler_params=_params("parallel", "arbitrary"),
        name="branch_merge",
    )


def _outproj_kernel(m_ref, w_ref, x_ref, *rest, cast):
    if cast:
        ci_ref, o_ref, co_ref = rest
        co_ref[...] = ci_ref[...].astype(bf16)
    else:
        (o_ref,) = rest
    o_ref[...] = x_ref[...] + jnp.dot(m_ref[...], w_ref[...], preferred_element_type=f32)


def out_proj(merged, w, x, *, cast=None, tm=1024, tn=1024):
    n, d = merged.shape
    dn = w.shape[1]
    return _call_with_cast(
        _outproj_kernel, (n // tm, dn // tn),
        [pl.BlockSpec((tm, d), lambda i, j: (i, 0)),
         pl.BlockSpec((d, tn), lambda i, j: (0, j)),
         pl.BlockSpec((tm, tn), lambda i, j: (i, j))],
        pl.BlockSpec((tm, tn), lambda i, j: (i, j)),
        jax.ShapeDtypeStruct((n, dn), f32),
        (merged, w, x), cast,
        compiler_params=_params("parallel", "arbitrary"),
        name="out_proj",
    )


ROUTE_TILE = 256
SLOT_CHUNK = 48
LANES = 128
ROW_TILES = D_MODEL // LANES
BF16_ROWS = 16
READ_CHUNK = SLOT_CHUNK + BF16_ROWS


def _rms(x, g):
    return x * lax.rsqrt(jnp.mean(x * x, axis=-1, keepdims=True) + NORM_EPS) * g


def _router_kernel(x_ref, g_ref, wh_ref, wl_ref, o_ref):
    xn = _rms(x_ref[...], g_ref[...])
    x_hi = xn.astype(bf16)
    x_lo = (xn - x_hi.astype(f32)).astype(bf16)
    nt = (((1,), (1,)), ((), ()))
    logits = (lax.dot_general(wh_ref[...], x_hi, nt, preferred_element_type=f32)
              + lax.dot_general(wl_ref[...], x_hi, nt, preferred_element_type=f32)
              + lax.dot_general(wh_ref[...], x_lo, nt, preferred_element_type=f32))
    m = jnp.max(logits, axis=0, keepdims=True)
    e = jnp.exp(logits - m)
    o_ref[...] = e / jnp.sum(e, axis=0, keepdims=True)


def router_probs(xm, g, wr_t, *, tm=512):
    n, d = xm.shape
    w_hi = wr_t.astype(bf16)
    w_lo = (wr_t - w_hi.astype(f32)).astype(bf16)
    wspec = pl.BlockSpec((N_EXPERTS, d), lambda i: (0, 0))
    return pl.pallas_call(
        _router_kernel,
        grid=(n // tm,),
        in_specs=[pl.BlockSpec((tm, d), lambda i: (i, 0)), pl.BlockSpec((1, d), lambda i: (0, 0)), wspec, wspec],
        out_specs=pl.BlockSpec((N_EXPERTS, tm), lambda i: (0, i)),
        out_shape=jax.ShapeDtypeStruct((N_EXPERTS, n), f32),
        compiler_params=_params("parallel"),
        name="router_probs",
    )(xm, g, w_hi, w_lo)


def _select_kernel(p_ref, pos_ref, gate_ref, offs_ref, *, cap):
    p = p_ref[...]
    R = p.shape[0]
    bits = pltpu.bitcast(p, i32)

    def body(i, prefix):
        cand = prefix | lax.shift_left(jnp.int32(1), 30 - i)
        cnt = jnp.sum((bits >= cand).astype(i32))
        return jnp.where(cnt >= cap, cand, prefix)

    thr = lax.fori_loop(0, 31, body, jnp.int32(0))
    gt = bits > thr
    eq = bits == thr
    need = cap - jnp.sum(gt.astype(i32))

    li = lax.broadcasted_iota(i32, (LANES, LANES), 0)
    lj = lax.broadcasted_iota(i32, (LANES, LANES), 1)
    upper = (li <= lj).astype(bf16)
    ri = lax.broadcasted_iota(i32, (R, R), 0)
    rj = lax.broadcasted_iota(i32, (R, R), 1)
    rows_before = (rj < ri).astype(bf16)

    def prefix_counts(mask):
        m = mask.astype(bf16)
        incl = jnp.dot(m, upper, preferred_element_type=f32)
        tot = jnp.broadcast_to(incl[:, LANES - 1:LANES], (R, LANES)).astype(bf16)
        row_off = jnp.dot(rows_before, tot, preferred_element_type=f32)
        return incl - mask.astype(f32) + row_off, row_off

    eq_excl, _ = prefix_counts(eq)
    sel = gt | (eq & (eq_excl < need.astype(f32)))
    pos, row_off = prefix_counts(sel)
    pos_ref[...] = jnp.where(sel, pos.astype(i32), -1)
    gate_ref[...] = jnp.where(sel, p, 0.0)
    offs_ref[...] = row_off[:, 0:1].astype(i32)


def select_tokens(probs3, cap):
    E, R, _ = probs3.shape
    blk = pl.BlockSpec((None, R, LANES), lambda e: (e, 0, 0))
    return pl.pallas_call(
        functools.partial(_select_kernel, cap=cap),
        grid=(E,),
        in_specs=[blk],
        out_specs=[blk, blk, pl.BlockSpec((None, R, 1), lambda e: (e, 0, 0))],
        out_shape=[jax.ShapeDtypeStruct((E, R, LANES), i32), jax.ShapeDtypeStruct((E, R, LANES), f32),
                   jax.ShapeDtypeStruct((E, R, 1), i32)],
        compiler_params=_params("parallel"),
        name="select_tokens",
    )(probs3)


def _onehot_t(pos, start, width):
    j = lax.broadcasted_iota(i32, (width, pos.shape[1]), 0)
    return j == (pos - start)


def _dispatch_kernel(pos0_ref, x_ref, g_ref, pos_ref, gate_ref, xe_ref, ge_ref, xbuf_ref, gbuf_ref, sem_ref,
                     *, cap, n_tiles):
    i = pl.program_id(0)
    E, tm = pos_ref.shape
    slot = i % 2

    xn = _rms(x_ref[...], g_ref[...]).astype(bf16)
    eye = (lax.broadcasted_iota(i32, (E, LANES), 0) == lax.broadcasted_iota(i32, (E, LANES), 1)).astype(f32)
    gt = lax.dot_general(gate_ref[...], eye, (((0,), (0,)), ((), ())), precision=lax.Precision.HIGHEST,
                         preferred_element_type=f32)
    g_hi = gt.astype(bf16)
    g_lo = (gt - g_hi.astype(f32)).astype(bf16)
    pos = pos_ref[...]

    def base(e, tile):
        return pos0_ref[e * (n_tiles + 1) + tile]

    def count(e):
        return base(e, i + 1) - base(e, i)

    def build(c, dst):
        oh = jnp.concatenate([_onehot_t(pos[e:e + 1, :], base(e, i) + c * SLOT_CHUNK, SLOT_CHUNK)
                              for e in range(E)], axis=0)
        oh = jnp.where(oh, 1.0, 0.0).astype(bf16)
        xr = jnp.dot(oh, xn, preferred_element_type=f32).astype(bf16)
        tiles = jnp.stack([xr[:, s * LANES:(s + 1) * LANES] for s in range(ROW_TILES)], axis=0)
        xbuf_ref[dst] = pltpu.einshape("srl->rsl", tiles)
        gbuf_ref[dst, :, 0, :] = (jnp.dot(oh, g_hi, preferred_element_type=f32)
                                  + jnp.dot(oh, g_lo, preferred_element_type=f32))

    def copies(e, c, src, tile):
        rows = pl.ds(e * SLOT_CHUNK, SLOT_CHUNK)
        out_rows = pl.ds(base(e, tile) + c * SLOT_CHUNK, SLOT_CHUNK)
        return (pltpu.make_async_copy(xbuf_ref.at[src, rows], xe_ref.at[e, out_rows], sem_ref.at[0, src]),
                pltpu.make_async_copy(gbuf_ref.at[src, rows], ge_ref.at[e, out_rows], sem_ref.at[1, src]))

    @pl.when(i == 0)
    def _():
        xbuf_ref[2] = jnp.zeros(xbuf_ref.shape[1:], bf16)
        gbuf_ref[2] = jnp.zeros(gbuf_ref.shape[1:], f32)
        for e in range(E):
            rows = pl.ds(e * SLOT_CHUNK, SLOT_CHUNK)
            pad = pl.ds(cap, SLOT_CHUNK)
            for cp in (pltpu.make_async_copy(xbuf_ref.at[2, rows], xe_ref.at[e, pad], sem_ref.at[0, 2]),
                       pltpu.make_async_copy(gbuf_ref.at[2, rows], ge_ref.at[e, pad], sem_ref.at[1, 2])):
                cp.start()
                cp.wait()

    build(0, slot)

    @pl.when(i > 0)
    def _():
        for e in range(E):
            for cp in copies(e, 0, 1 - slot, i - 1):
                cp.wait()

    for e in range(E):
        for cp in copies(e, 0, slot, i):
            cp.start()

    nch = jnp.int32(0)
    for e in range(E):
        nch = jnp.maximum(nch, (count(e) + SLOT_CHUNK - 1) // SLOT_CHUNK)

    def overflow(c, carry):
        build(c, 2)
        for e in range(E):
            @pl.when(count(e) > c * SLOT_CHUNK)
            def _():
                for cp in copies(e, c, 2, i):
                    cp.start()
                    cp.wait()
        return carry

    lax.fori_loop(1, nch, overflow, 0)

    @pl.when(i == n_tiles - 1)
    def _():
        for e in range(E):
            for cp in copies(e, 0, slot, i):
                cp.wait()


def dispatch(xm, g, pos, gates, pos0, cap):
    n, d = xm.shape
    E = pos.shape[0]
    n_tiles = n // ROUTE_TILE
    rows = cap + SLOT_CHUNK
    grid_spec = pltpu.PrefetchScalarGridSpec(
        num_scalar_prefetch=1,
        grid=(n_tiles,),
        in_specs=[
            pl.BlockSpec((ROUTE_TILE, d), lambda i, p0: (i, 0)),
            pl.BlockSpec((1, d), lambda i, p0: (0, 0)),
            pl.BlockSpec((E, ROUTE_TILE), lambda i, p0: (0, i)),
            pl.BlockSpec((E, ROUTE_TILE), lambda i, p0: (0, i)),
        ],
        out_specs=[pl.BlockSpec(memory_space=pl.ANY), pl.BlockSpec(memory_space=pl.ANY)],
        scratch_shapes=[pltpu.VMEM((3, E * SLOT_CHUNK, ROW_TILES, LANES), bf16),
                        pltpu.VMEM((3, E * SLOT_CHUNK, 1, LANES), f32),
                        pltpu.SemaphoreType.DMA((2, 3))],
    )
    return pl.pallas_call(
        functools.partial(_dispatch_kernel, cap=cap, n_tiles=n_tiles),
        grid_spec=grid_spec,
        out_shape=[jax.ShapeDtypeStruct((E, rows, ROW_TILES, LANES), bf16),
                   jax.ShapeDtypeStruct((E, rows, 1, LANES), f32)],
        compiler_params=_params("arbitrary"),
        name="dispatch",
    )(pos0, xm, g, pos, gates)


def _expert_up_kernel(xe_ref, wg_ref, wu_ref, h_ref, xb_ref):
    @pl.when(pl.program_id(2) == 0)
    def _():
        tiles = pltpu.einshape("rsl->srl", xe_ref[...])
        for s in range(ROW_TILES):
            xb_ref[:, s * LANES:(s + 1) * LANES] = tiles[s]

    xb = xb_ref[...]
    hg = jnp.dot(xb, wg_ref[...], preferred_element_type=f32)
    hu = jnp.dot(xb, wu_ref[...], preferred_element_type=f32)
    h_ref[...] = (hg * jax.nn.sigmoid(hg) * hu).astype(h_ref.dtype)


def _expert_down_kernel(h_ref, ge_ref, wd_ref, o_ref):
    gl = ge_ref[:, 0, :]
    lane = lax.broadcasted_iota(i32, gl.shape, 1)
    gate = jnp.sum(jnp.where(lane == pl.program_id(0), gl, 0.0), axis=-1, keepdims=True)
    o_ref[...] = (jnp.dot(h_ref[...], wd_ref[...], preferred_element_type=f32) * gate).astype(o_ref.dtype)


def expert_ffn(xe, ge, wg, wu, wd, cap, *, tm=1024, tf=1024, tn=1024):
    E = xe.shape[0]
    d = D_MODEL
    dff = wg.shape[2]
    tm = min(tm, cap)
    h = pl.pallas_call(
        _expert_up_kernel,
        grid=(E, cap // tm, dff // tf),
        in_specs=[
            pl.BlockSpec((None, tm, ROW_TILES, LANES), lambda e_, i, f: (e_, i, 0, 0)),
            pl.BlockSpec((None, d, tf), lambda e_, i, f: (e_, 0, f)),
            pl.BlockSpec((None, d, tf), lambda e_, i, f: (e_, 0, f)),
        ],
        out_specs=pl.BlockSpec((None, tm, tf), lambda e_, i, f: (e_, i, f)),
        out_shape=jax.ShapeDtypeStruct((E, cap, dff), bf16),
        scratch_shapes=[pltpu.VMEM((tm, d), bf16)],
        compiler_params=_params("parallel", "parallel", "arbitrary"),
        name="expert_up",
    )(xe, wg, wu)
    return pl.pallas_call(
        _expert_down_kernel,
        grid=(E, cap // tm, d // tn),
        in_specs=[
            pl.BlockSpec((None, tm, dff), lambda e_, i, j: (e_, i, 0)),
            pl.BlockSpec((None, tm, 1, LANES), lambda e_, i, j: (e_, i, 0, 0)),
            pl.BlockSpec((None, dff, tn), lambda e_, i, j: (e_, 0, j)),
        ],
        out_specs=pl.BlockSpec((None, tm, tn), lambda e_, i, j: (e_, i, j)),
        out_shape=jax.ShapeDtypeStruct((E, cap, d), bf16),
        compiler_params=_params("parallel", "parallel", "arbitrary"),
        name="expert_down",
    )(h, ge, wd)


def _combine_kernel(pos0_ref, x_ref, g_ref, pos_ref, ye_ref, o_ref, buf_ref, sem_ref, *, cap, n_tiles):
    i = pl.program_id(0)
    E, tm = pos_ref.shape
    slot = i % 2

    def base(e, tile):
        return pos0_ref[e * (n_tiles + 1) + tile]

    def count(e):
        return base(e, i + 1) - base(e, i)

    def start_row(e, c, tile):
        st = base(e, tile) + c * SLOT_CHUNK
        st = jnp.minimum((st // BF16_ROWS) * BF16_ROWS, cap - READ_CHUNK)
        return pl.multiple_of(st, BF16_ROWS)

    def copy(e, c, dst, tile):
        return pltpu.make_async_copy(ye_ref.at[e, pl.ds(start_row(e, c, tile), READ_CHUNK)],
                                     buf_ref.at[dst, pl.ds(e * READ_CHUNK, READ_CHUNK)], sem_ref.at[dst])

    @pl.when(i == 0)
    def _():
        for e in range(E):
            copy(e, 0, slot, i).start()

    @pl.when(i + 1 < n_tiles)
    def _():
        for e in range(E):
            copy(e, 0, 1 - slot, i + 1).start()

    pos = pos_ref[...]

    def onehot(c):
        parts = []
        for e in range(E):
            pe = pos[e:e + 1, :]
            local = pe - base(e, i)
            keep = (local >= c * SLOT_CHUNK) & (local < (c + 1) * SLOT_CHUNK) & (pe >= 0)
            oh = _onehot_t(pe, start_row(e, c, i), READ_CHUNK) & keep
            parts.append(jnp.where(oh, 1.0, 0.0).astype(bf16))
        return jnp.concatenate(parts, axis=0)

    for e in range(E):
        copy(e, 0, slot, i).wait()
    acc = lax.dot_general(onehot(0), buf_ref[slot], (((0,), (0,)), ((), ())), preferred_element_type=f32)

    nch = jnp.int32(0)
    for e in range(E):
        nch = jnp.maximum(nch, (count(e) + SLOT_CHUNK - 1) // SLOT_CHUNK)

    def overflow(c, acc):
        for e in range(E):
            cp = copy(e, c, 2, i)
            cp.start()
            cp.wait()
        return acc + lax.dot_general(onehot(c), buf_ref[2], (((0,), (0,)), ((), ())), preferred_element_type=f32)

    acc = lax.fori_loop(1, nch, overflow, acc)
    o_ref[...] = _rms(x_ref[...] + acc, g_ref[...])


def combine(xm, g_final, pos, ye, pos0, cap):
    n, d = xm.shape
    E = pos.shape[0]
    n_tiles = n // ROUTE_TILE
    grid_spec = pltpu.PrefetchScalarGridSpec(
        num_scalar_prefetch=1,
        grid=(n_tiles,),
        in_specs=[
            pl.BlockSpec((ROUTE_TILE, d), lambda i, p0: (i, 0)),
            pl.BlockSpec((1, d), lambda i, p0: (0, 0)),
            pl.BlockSpec((E, ROUTE_TILE), lambda i, p0: (0, i)),
            pl.BlockSpec(memory_space=pl.ANY),
        ],
        out_specs=pl.BlockSpec((ROUTE_TILE, d), lambda i, p0: (i, 0)),
        scratch_shapes=[pltpu.VMEM((3, E * READ_CHUNK, d), bf16), pltpu.SemaphoreType.DMA((3,))],
    )
    return pl.pallas_call(
        functools.partial(_combine_kernel, cap=cap, n_tiles=n_tiles),
        grid_spec=grid_spec,
        out_shape=jax.ShapeDtypeStruct((n, d), f32),
        compiler_params=_params("arbitrary"),
        name="combine",
    )(pos0, xm, g_final, pos, ye)


def moe_block(xm, g_moe, wr_t, wg, wu, wd, g_final):
    n = xm.shape[0]
    cap = EC_CAPACITY * n // N_EXPERTS
    n_tiles = n // ROUTE_TILE
    probs = router_probs(xm, g_moe, wr_t)
    pos3, gate3, offs = select_tokens(probs.reshape(N_EXPERTS, n // LANES, LANES), cap)
    pos = pos3.reshape(N_EXPERTS, n)
    gates = gate3.reshape(N_EXPERTS, n)
    tile_off = offs[:, ::ROUTE_TILE // LANES, 0]
    pos0 = jnp.concatenate([tile_off, jnp.full((N_EXPERTS, 1), cap, i32)], axis=1).reshape(-1)
    xe, ge = dispatch(xm, g_moe, pos, gates, pos0, cap)
    ye = expert_ffn(xe, ge, wg, wu, wd, cap)
    return combine(xm, g_final, pos, ye, pos0, cap)


N2 = 128


def _stage1_tables(H):
    n_total = 2 * N2 * H
    k1 = jnp.arange(H, dtype=jnp.int32)
    idx = jnp.arange(max(H, N2), dtype=jnp.int32)

    def phase(m):
        a = (m % (2 * n_total)).astype(f32) * (math.pi / n_total)
        return jnp.cos(a), jnp.sin(a)

    c1, s1 = phase((2 * k1 + 1)[:, None] * (N2 * idx[:H])[None, :])
    c2, s2 = phase((2 * k1 + 1)[None, :] * idx[:N2, None])
    c = c2[:, :, None] * c1[None] - s2[:, :, None] * s1[None]
    s_ = s2[:, :, None] * c1[None] + c2[:, :, None] * s1[None]
    fwd = jnp.concatenate([c, -s_], axis=1).astype(bf16)
    scale = 1.0 / (N2 * H)
    ct, st = jnp.swapaxes(c, 1, 2), jnp.swapaxes(s_, 1, 2)
    inv = (jnp.concatenate([ct, -st], axis=2) * scale).astype(bf16)
    return fwd, inv


def _stage2_tables():
    j = jnp.arange(N2, dtype=jnp.int32)
    th = ((j[:, None] * j[None, :]) % N2).astype(f32) * (2.0 * math.pi / N2)
    c, s = jnp.cos(th), jnp.sin(th)
    fwd = jnp.concatenate([jnp.concatenate([c, s], 1), jnp.concatenate([-s, c], 1)], 0)
    inv = jnp.concatenate([jnp.concatenate([c, -s], 1), jnp.concatenate([s, c], 1)], 0)
    return fwd.astype(bf16), inv.astype(bf16)


def _shortconv_kernel(x_ref, w_ref, b_ref, o_ref, s_ref, *, rows):
    L = x_ref.shape[0]
    zeros8 = jnp.zeros((8, x_ref.shape[1]), f32)
    s_ref[0:8, :] = zeros8
    s_ref[L + 8:L + 16, :] = zeros8

    def load(i, c):
        r0 = pl.multiple_of(i * rows, rows)
        s_ref[pl.ds(r0 + 8, rows), :] = x_ref[pl.ds(r0, rows), :].astype(f32)
        return c

    lax.fori_loop(0, L // rows, load, 0)
    w0, w1, w2, b = w_ref[0:1, :], w_ref[1:2, :], w_ref[2:3, :], b_ref[...]

    def conv(i, c):
        r0 = pl.multiple_of(i * rows, rows)
        prev = s_ref[pl.ds(r0 + 7, rows), :]
        cur = s_ref[pl.ds(r0 + 8, rows), :]
        nxt = s_ref[pl.ds(r0 + 9, rows), :]
        o_ref[pl.ds(r0, rows), :] = (prev * w0 + cur * w1 + nxt * w2 + b).astype(o_ref.dtype)
        return c

    lax.fori_loop(0, L // rows, conv, 0)


def short_conv(p3, w, b, col0, ncols, out_dtype, *, rows=256, cw=128):
    B, L, _ = p3.shape
    cb = col0 // cw
    return pl.pallas_call(
        functools.partial(_shortconv_kernel, rows=rows),
        grid=(B, ncols // cw),
        in_specs=[
            pl.BlockSpec((None, L, cw), lambda b_, j: (b_, 0, cb + j)),
            pl.BlockSpec((3, cw), lambda b_, j: (0, cb + j)),
            pl.BlockSpec((1, cw), lambda b_, j: (0, cb + j)),
        ],
        out_specs=pl.BlockSpec((None, L, cw), lambda b_, j: (b_, 0, j)),
        out_shape=jax.ShapeDtypeStruct((B, L, ncols), out_dtype),
        scratch_shapes=[pltpu.VMEM((L + 16, cw), f32)],
        compiler_params=_params("parallel", "parallel"),
        name="short_conv",
    )(p3, w, b)


def _dft1_kernel(tab_ref, z_ref, o_ref, *, n2b, K):
    g = pl.program_id(2)
    rows = [z_ref[pl.ds(pl.multiple_of(n1 * N2 + g * n2b, n2b), n2b), :] for n1 in range(K)]
    zs = pltpu.einshape("kqc->qkc", jnp.stack(rows, axis=0)).astype(bf16)
    for q in range(n2b):
        o_ref[q] = jnp.dot(tab_ref[g * n2b + q], zs[q], preferred_element_type=f32).astype(o_ref.dtype)


def dft_stage1(z, tab, *, n2b=16, cb=256):
    B, Lz, C = z.shape
    _, M, K = tab.shape
    assert Lz == K * N2
    return pl.pallas_call(
        functools.partial(_dft1_kernel, n2b=n2b, K=K),
        grid=(B, C // cb, N2 // n2b),
        in_specs=[
            pl.BlockSpec((N2, M, K), lambda b_, c, g: (0, 0, 0)),
            pl.BlockSpec((None, Lz, cb), lambda b_, c, g: (b_, 0, c)),
        ],
        out_specs=pl.BlockSpec((None, n2b, M, cb), lambda b_, c, g: (b_, g, 0, c)),
        out_shape=jax.ShapeDtypeStruct((B, N2, M, C), bf16),
        compiler_params=_params("parallel", "parallel", "arbitrary"),
        name="dft_stage1",
    )(tab, z)


def _mid_kernel(m2_ref, m2i_ref, a_ref, kf_ref, o_ref, *, k1b):
    are = pltpu.einshape("nkc->knc", a_ref[:, 0])
    aim = pltpu.einshape("nkc->knc", a_ref[:, 1])
    bre, bim = [], []
    for q in range(k1b):
        x = jnp.dot(m2_ref[...], jnp.concatenate([are[q], aim[q]], axis=0), preferred_element_type=f32)
        xr, xi = x[:N2], x[N2:]
        kf = kf_ref[q].astype(f32)
        kr, ki = kf[:N2], kf[N2:]
        y = jnp.concatenate([xr * kr - xi * ki, xr * ki + xi * kr], axis=0).astype(bf16)
        bq = jnp.dot(m2i_ref[...], y, preferred_element_type=f32).astype(bf16)
        bre.append(bq[:N2])
        bim.append(bq[N2:])
    o_ref[:, 0] = pltpu.einshape("knc->nkc", jnp.stack(bre, axis=0))
    o_ref[:, 1] = pltpu.einshape("knc->nkc", jnp.stack(bim, axis=0))


def spectral_mid(a, kf, m2, m2i, order, *, k1b=16, cb=256):
    B, _, _, H, C = a.shape
    k1b = min(k1b, H)
    cpo = C // cb
    return pl.pallas_call(
        functools.partial(_mid_kernel, k1b=k1b),
        grid=(B, cpo, H // k1b),
        in_specs=[
            pl.BlockSpec((2 * N2, 2 * N2), lambda b_, c, k: (0, 0)),
            pl.BlockSpec((2 * N2, 2 * N2), lambda b_, c, k: (0, 0)),
            pl.BlockSpec((None, N2, 2, k1b, cb), lambda b_, c, k: (b_, 0, 0, k, c)),
            pl.BlockSpec((k1b, 2 * N2, cb), lambda b_, c, k: (k, 0, order * cpo + c)),
        ],
        out_specs=pl.BlockSpec((None, N2, 2, k1b, cb), lambda b_, c, k: (b_, 0, 0, k, c)),
        out_shape=jax.ShapeDtypeStruct(a.shape, bf16),
        compiler_params=_params("parallel", "parallel", "arbitrary"),
        name="spectral_mid",
    )(m2, m2i, a, kf)


def _idft1_gate_kernel(tab_ref, b_ref, x_ref, v_ref, d_ref, o_ref, y_ref, *, n2b, H, rows):
    g = pl.program_id(2)
    ys = [jnp.dot(tab_ref[g * n2b + q], b_ref[q], preferred_element_type=f32) for q in range(n2b)]
    yt = pltpu.einshape("qhc->hqc", jnp.stack(ys, axis=0))
    for n1 in range(H):
        y_ref[pl.ds(pl.multiple_of(n1 * N2 + g * n2b, n2b), n2b), :] = yt[n1]

    @pl.when(g == pl.num_programs(2) - 1)
    def _():
        d = d_ref[...]

        def gate(i, c):
            r = pl.ds(pl.multiple_of(i * rows, rows), rows)
            o_ref[r, :] = (x_ref[r, :].astype(f32) * (y_ref[r, :] + v_ref[r, :].astype(f32) * d)).astype(o_ref.dtype)
            return c

        lax.fori_loop(0, (H * N2) // rows, gate, 0)


def idft1_gate(bm, tab, xg, xcol0, v, d, *, n2b=16, rows=256):
    B, _, M, C = bm.shape
    H = M // 2
    L = H * N2
    xb = xcol0 // 128
    return pl.pallas_call(
        functools.partial(_idft1_gate_kernel, n2b=n2b, H=H, rows=rows),
        grid=(B, C // 128, N2 // n2b),
        in_specs=[
            pl.BlockSpec((N2, H, M), lambda b_, c, g: (0, 0, 0)),
            pl.BlockSpec((None, n2b, M, 128), lambda b_, c, g: (b_, g, 0, c)),
            pl.BlockSpec((None, L, 128), lambda b_, c, g: (b_, 0, xb + c)),
            pl.BlockSpec((None, L, 128), lambda b_, c, g: (b_, 0, c)),
            pl.BlockSpec((1, 128), lambda b_, c, g: (0, c)),
        ],
        out_specs=pl.BlockSpec((None, L, 128), lambda b_, c, g: (b_, 0, c)),
        out_shape=jax.ShapeDtypeStruct((B, L, C), bf16),
        scratch_shapes=[pltpu.VMEM((L, 128), f32)],
        compiler_params=_params("parallel", "parallel", "arbitrary"),
        name="idft1_gate",
    )(tab, bm, xg, v, d)


def _filter_mlp_kernel(feat_ref, w1_ref, b1_ref, w2_ref, b2_ref, w3_ref, b3_ref, w4a_ref, w4b_ref, fr_ref, dl_ref,
                       pq_ref, nrm_ref):
    hp = lax.Precision.HIGHEST
    i = pl.program_id(0)
    feat = feat_ref[...]
    fr = fr_ref[...]
    h = jnp.sin(fr * (jnp.dot(feat, w1_ref[...], precision=hp, preferred_element_type=f32) + b1_ref[...]))
    h = jnp.sin(fr * (jnp.dot(h, w2_ref[...], precision=hp, preferred_element_type=f32) + b2_ref[...]))
    h = jnp.sin(fr * (jnp.dot(h, w3_ref[...], precision=hp, preferred_element_type=f32) + b3_ref[...]))
    h_hi = h.astype(bf16)
    h_lo = (h - h_hi.astype(f32)).astype(bf16)
    h_cat = jnp.concatenate([h_hi, h_lo], axis=1)

    def proj(cols):
        return (jnp.dot(h_cat, w4a_ref[:, cols], preferred_element_type=f32)
                + jnp.dot(h_hi, w4b_ref[:, cols], preferred_element_type=f32))

    win = jnp.exp(-feat[:, 0:1] * dl_ref[...])
    row = lax.broadcasted_iota(jnp.int32, win.shape, 0) + i * feat.shape[0]
    C = D_HYENA
    nrm = []
    for o in range(HYENA_ORDER):
        c0 = o * N_DIR * C
        fwd = proj(slice(c0, c0 + C)) * win
        bwd = proj(slice(c0 + C, c0 + 2 * C)) * win
        bwd = jnp.where(row == 0, 0.0, bwd)
        pq_ref[0, :, o * C:(o + 1) * C] = (fwd + bwd).astype(pq_ref.dtype)
        pq_ref[1, :, o * C:(o + 1) * C] = (fwd - bwd).astype(pq_ref.dtype)
        nrm.append(jnp.sum(jnp.abs(fwd) + jnp.abs(bwd), axis=0, keepdims=True))
    nrm = jnp.concatenate(nrm, axis=1)

    @pl.when(i == 0)
    def _():
        nrm_ref[...] = nrm

    @pl.when(i != 0)
    def _():
        nrm_ref[...] += nrm


def filter_mlp(feat, w1, b1, w2, b2, w3, b3, w4a, w4b, freq, deltas, *, tm=512):
    L = feat.shape[0]
    CO = HYENA_ORDER * D_HYENA
    full = lambda a: pl.BlockSpec(a.shape, lambda i: (0,) * a.ndim)
    args = (w1, b1, w2, b2, w3, b3, w4a, w4b, freq, deltas)
    return pl.pallas_call(
        _filter_mlp_kernel,
        grid=(L // tm,),
        in_specs=[pl.BlockSpec((tm, 128), lambda i: (i, 0))] + [full(a) for a in args],
        out_specs=[pl.BlockSpec((2, tm, CO), lambda i: (0, i, 0)), pl.BlockSpec((1, CO), lambda i: (0, 0))],
        out_shape=[jax.ShapeDtypeStruct((2, L, CO), bf16), jax.ShapeDtypeStruct((1, CO), f32)],
        compiler_params=_params("arbitrary"),
        name="filter_mlp",
    )(feat, *args)


def _filter_stage2_kernel(m2_ref, ap_ref, aq_ref, nrm_ref, o_ref, *, k1b):
    inv = 1.0 / nrm_ref[...]
    pr, pi = pltpu.einshape("nkc->knc", ap_ref[:, 0]), pltpu.einshape("nkc->knc", ap_ref[:, 1])
    qr, qi = pltpu.einshape("nkc->knc", aq_ref[:, 0]), pltpu.einshape("nkc->knc", aq_ref[:, 1])
    for q in range(k1b):
        ap = jnp.concatenate([pr[q], pi[q]], axis=0)
        aq = jnp.concatenate([qr[q], qi[q]], axis=0)
        re = jnp.dot(m2_ref[0:N2, :], ap, preferred_element_type=f32) * inv
        im = jnp.dot(m2_ref[N2:2 * N2, :], aq, preferred_element_type=f32) * inv
        o_ref[q] = jnp.concatenate([re, im], axis=0).astype(o_ref.dtype)


def filter_stage2(apq, m2, nrm, *, k1b=16, cb=256):
    _, _, _, H, CO = apq.shape
    k1b = min(k1b, H)
    return pl.pallas_call(
        functools.partial(_filter_stage2_kernel, k1b=k1b),
        grid=(CO // cb, H // k1b),
        in_specs=[
            pl.BlockSpec((2 * N2, 2 * N2), lambda c, k: (0, 0)),
            pl.BlockSpec((None, N2, 2, k1b, cb), lambda c, k: (0, 0, 0, k, c)),
            pl.BlockSpec((None, N2, 2, k1b, cb), lambda c, k: (1, 0, 0, k, c)),
            pl.BlockSpec((1, cb), lambda c, k: (0, c)),
        ],
        out_specs=pl.BlockSpec((k1b, 2 * N2, cb), lambda c, k: (k, 0, c)),
        out_shape=jax.ShapeDtypeStruct((H, 2 * N2, CO), bf16),
        compiler_params=_params("parallel", "arbitrary"),
        name="filter_stage2",
    )(m2, apq, apq, nrm)


def _pad2(a, rows, cols):
    return jnp.pad(a, ((0, rows - a.shape[0]), (0, cols - a.shape[1])))


def hyena_filter(L, w1, b1, w2, b2, w3, b3, w4, freq, tab_f, m2):
    pos = jnp.arange(L, dtype=f32)[:, None]
    t = jnp.linspace(0.0, 1.0, L, dtype=f32)[:, None]
    bands = (FILTER_EMB - 1) // 2
    z = jnp.linspace(1e-4, bands - 1, bands, dtype=f32)[None, :] * ((2.0 * math.pi / L) * pos)
    feat = _pad2(jnp.concatenate([t, jnp.cos(z), -jnp.sin(z)], axis=-1), L, 128)
    deltas = jnp.abs(jnp.linspace(math.log(DECAY_TARGET) / DECAY_SLOW_PCT,
                                  math.log(DECAY_TARGET) / DECAY_FAST_PCT, D_HYENA, dtype=f32))[None, :]
    row = lambda v: _pad2(v[None, :], 1, 128)
    w4p = _pad2(w4, 128, w4.shape[1])
    w4_hi = w4p.astype(bf16)
    w4_lo = (w4p - w4_hi.astype(f32)).astype(bf16)
    pq, nrm = filter_mlp(feat, _pad2(w1, 128, 128), row(b1), _pad2(w2, 128, 128), row(b2), _pad2(w3, 128, 128),
                         row(b3), jnp.concatenate([w4_hi, w4_hi], axis=0), w4_lo, row(freq), deltas)
    apq = dft_stage1(pq, tab_f)
    H = L // N2
    return filter_stage2(apq.reshape(2, N2, 2, H, HYENA_ORDER * D_HYENA), m2, nrm)


def hyena_mixer(p3, conv_w, conv_b, hy_d, kf, tabs):
    tab_f, tab_i, m2, m2i = tabs
    B, L, _ = p3.shape
    C = D_HYENA
    H = L // N2
    cb = conv_b[None, :]
    xg = short_conv(p3, conv_w, cb, 0, 2 * C, bf16)
    z = short_conv(p3, conv_w, cb, 2 * C, C, bf16)
    for o in range(HYENA_ORDER):
        a = dft_stage1(z, tab_f).reshape(B, N2, 2, H, C)
        bm = spectral_mid(a, kf, m2, m2i, o).reshape(B, N2, 2 * H, C)
        z = idft1_gate(bm, tab_i, xg, o * C, z, hy_d[o:o + 1])
    return z


def _gelu(x):
    return 0.5 * x * (1.0 + lax.erf(x * (1.0 / math.sqrt(2.0))))


def _sgu_kernel(pu_ref, pv_ref, g_ref, b_ref, ws_ref, bs_ref, o_ref):
    v = _gelu(pv_ref[...].astype(f32))
    mu = jnp.mean(v, axis=-1, keepdims=True)
    vc = v - mu
    var = jnp.mean(vc * vc, axis=-1, keepdims=True)
    vn = (vc * lax.rsqrt(var + LN_EPS) * g_ref[...] + b_ref[...]).astype(bf16)
    for c in range(pv_ref.shape[0] // CHUNK):
        rows = slice(c * CHUNK, (c + 1) * CHUNK)
        for g in range(SGU_GROUPS):
            cols = slice(g * SGU_GROUP_DIM, (g + 1) * SGU_GROUP_DIM)
            s = jnp.dot(ws_ref[g], vn[rows, cols], preferred_element_type=f32) + bs_ref[:, g:g + 1]
            u = _gelu(pu_ref[rows, cols].astype(f32))
            o_ref[rows, cols] = (u * s).astype(o_ref.dtype)


def sgu(p, ln_g, ln_b, w_s, b_s_t, *, tm=512):
    n = p.shape[0]
    ub = OFF_SG // D_SGU
    return pl.pallas_call(
        _sgu_kernel,
        grid=(n // tm,),
        in_specs=[
            pl.BlockSpec((tm, D_SGU), lambda i: (i, ub)),
            pl.BlockSpec((tm, D_SGU), lambda i: (i, ub + 1)),
            pl.BlockSpec((1, D_SGU), lambda i: (0, 0)),
            pl.BlockSpec((1, D_SGU), lambda i: (0, 0)),
            pl.BlockSpec((SGU_GROUPS, CHUNK, CHUNK), lambda i: (0, 0, 0)),
            pl.BlockSpec((CHUNK, SGU_GROUPS), lambda i: (0, 0)),
        ],
        out_specs=pl.BlockSpec((tm, D_SGU), lambda i: (i, 0)),
        out_shape=jax.ShapeDtypeStruct((n, D_SGU), bf16),
        compiler_params=_params("parallel"),
        name="sgu",
    )(p, p, ln_g, ln_b, w_s, b_s_t)


def kernel(x_prompt, x_sample, norm_mix_g, w_in, hy_conv_w, hy_conv_b, flt_w1, flt_b1, flt_w2, flt_b2,
           flt_w3, flt_b3, flt_w4, flt_sin_freq, hy_d, sg_ln_g, sg_ln_b, sg_w, sg_b, w_branch_hy,
           w_branch_sg, w_out, norm_moe_g, w_router, w_gate, w_up, w_down, norm_final_g):
    w_in_b = w_in[0].astype(bf16)
    w_bh_b = w_branch_hy[0].astype(bf16)
    w_bs_b = w_branch_sg[0].astype(bf16)
    w_out_b = w_out[0].astype(bf16)
    g_mix = norm_mix_g[0].reshape(1, D_MODEL)
    sg_w_b = sg_w[0].astype(bf16)
    sg_b_t = sg_b[0].T

    m2, m2i = _stage2_tables()

    def mixer(x, casts=(None, None, None)):
        B, L, D = x.shape
        n = B * L
        xt = x.reshape(n, D)
        p = in_proj(xt, g_mix, w_in_b, cast=casts[0])
        p, c0 = p if casts[0] is not None else (p, None)
        p3 = p.reshape(B, L, D_IN)
        tab_f, tab_i = _stage1_tables(L // N2)
        kf = hyena_filter(L, flt_w1[0], flt_b1[0], flt_w2[0], flt_b2[0], flt_w3[0], flt_b3[0],
                          flt_w4[0], flt_sin_freq[0], tab_f, m2)
        y_hy = hyena_mixer(p3, hy_conv_w[0], hy_conv_b[0], hy_d[0], kf,
                           (tab_f, tab_i, m2, m2i)).reshape(n, D_HYENA)
        y_sg = sgu(p, sg_ln_g, sg_ln_b, sg_w_b, sg_b_t)
        merged = branch_merge(y_hy, y_sg, w_bh_b, w_bs_b, p, cast=casts[1])
        merged, c1 = merged if casts[1] is not None else (merged, None)
        xm = out_proj(merged, w_out_b, xt, cast=casts[2])
        xm, c2 = xm if casts[2] is not None else (xm, None)
        return xm, (c0, c1, c2)

    g_moe = norm_moe_g[0].reshape(1, D_MODEL)
    g_fin = norm_final_g.reshape(1, D_MODEL)
    wr_t = w_router[0].T
    flat = lambda w: w[0].reshape(N_EXPERTS * D_MODEL, D_MODEL)
    xm_p, (wd_b, wg_b, wu_b) = mixer(x_prompt, (flat(w_down), flat(w_gate), flat(w_up)))
    shape3 = (N_EXPERTS, D_MODEL, D_MODEL)
    wg_b, wu_b, wd_b = wg_b.reshape(shape3), wu_b.reshape(shape3), wd_b.reshape(shape3)
    y_p = moe_block(xm_p, g_moe, wr_t, wg_b, wu_b, wd_b, g_fin)
    xm_s, _ = mixer(x_sample)
    y_s = moe_block(xm_s, g_moe, wr_t, wg_b, wu_b, wd_b, g_fin)
    return (y_p.reshape(x_prompt.shape), y_s.reshape(x_sample.shape))
```

```python
import functools
import math

import jax
import jax.numpy as jnp
from jax import lax
from jax.experimental import pallas as pl
from jax.experimental.pallas import tpu as pltpu

f32 = jnp.float32
bf16 = jnp.bfloat16
i32 = jnp.int32

D_MODEL = 2048
D_HYENA = D_MODEL // 2
HYENA_ORDER = 2
N_DIR = 2
FILTER_EMB = 33
DECAY_FAST_PCT = 0.3
DECAY_SLOW_PCT = 1.5
DECAY_TARGET = 1e-2
D_SGU = D_MODEL // 2
CHUNK = 128
SGU_GROUPS = 8
SGU_GROUP_DIM = D_SGU // SGU_GROUPS
N_EXPERTS = 16
EC_CAPACITY = 2
NORM_EPS = 1e-6
LN_EPS = 1e-5
OFF_SG = 3 * D_HYENA
OFF_GH = OFF_SG + 2 * D_SGU
OFF_GS = OFF_GH + D_MODEL
D_IN = OFF_GS + D_MODEL

VMEM_LIMIT_BYTES = 56 * 1024 * 1024


def _params(*sem):
    return pltpu.CompilerParams(dimension_semantics=sem, vmem_limit_bytes=VMEM_LIMIT_BYTES)


def _inproj_kernel(x_ref, g_ref, w_ref, o_ref, xn_ref):
    @pl.when(pl.program_id(1) == 0)
    def _():
        x = x_ref[...]
        ms = jnp.mean(x * x, axis=-1, keepdims=True)
        xn_ref[...] = (x * lax.rsqrt(ms + NORM_EPS) * g_ref[...]).astype(bf16)

    o_ref[...] = jnp.dot(xn_ref[...], w_ref[...], preferred_element_type=f32).astype(o_ref.dtype)


def in_proj(x, g, w, *, tm=1024, tn=2304):
    n, d = x.shape
    dn = w.shape[1]
    return pl.pallas_call(
        _inproj_kernel,
        grid=(n // tm, dn // tn),
        in_specs=[
            pl.BlockSpec((tm, d), lambda i, j: (i, 0)),
            pl.BlockSpec((1, d), lambda i, j: (0, 0)),
            pl.BlockSpec((d, tn), lambda i, j: (0, j)),
        ],
        out_specs=pl.BlockSpec((tm, tn), lambda i, j: (i, j)),
        out_shape=jax.ShapeDtypeStruct((n, dn), bf16),
        scratch_shapes=[pltpu.VMEM((tm, d), bf16)],
        compiler_params=_params("parallel", "arbitrary"),
        name="in_proj",
    )(x, g, w)


def _merge_kernel(yh_ref, ys_ref, wbh_ref, wbs_ref, gh_ref, gs_ref, o_ref):
    a = jnp.dot(yh_ref[...], wbh_ref[...], preferred_element_type=f32)
    b = jnp.dot(ys_ref[...], wbs_ref[...], preferred_element_type=f32)
    o = jax.nn.sigmoid(gh_ref[...].astype(f32)) * a + jax.nn.sigmoid(gs_ref[...].astype(f32)) * b
    o_ref[...] = o.astype(o_ref.dtype)


def branch_merge(y_hy, y_sg, w_bh, w_bs, p, *, tm=1024, tn=1024):
    n, dh = y_hy.shape
    d = w_bh.shape[1]
    gh0 = OFF_GH // tn
    gs0 = OFF_GS // tn
    return pl.pallas_call(
        _merge_kernel,
        grid=(n // tm, d // tn),
        in_specs=[
            pl.BlockSpec((tm, dh), lambda i, j: (i, 0)),
            pl.BlockSpec((tm, dh), lambda i, j: (i, 0)),
            pl.BlockSpec((dh, tn), lambda i, j: (0, j)),
            pl.BlockSpec((dh, tn), lambda i, j: (0, j)),
            pl.BlockSpec((tm, tn), lambda i, j: (i, gh0 + j)),
            pl.BlockSpec((tm, tn), lambda i, j: (i, gs0 + j)),
        ],
        out_specs=pl.BlockSpec((tm, tn), lambda i, j: (i, j)),
        out_shape=jax.ShapeDtypeStruct((n, d), bf16),
        compiler_params=_params("parallel", "arbitrary"),
        name="branch_merge",
    )(y_hy, y_sg, w_bh, w_bs, p, p)


def _outproj_kernel(m_ref, w_ref, x_ref, o_ref):
    o_ref[...] = x_ref[...] + jnp.dot(m_ref[...], w_ref[...], preferred_element_type=f32)


def out_proj(merged, w, x, *, tm=1024, tn=1024):
    n, d = merged.shape
    dn = w.shape[1]
    return pl.pallas_call(
        _outproj_kernel,
        grid=(n // tm, dn // tn),
        in_specs=[
            pl.BlockSpec((tm, d), lambda i, j: (i, 0)),
            pl.BlockSpec((d, tn), lambda i, j: (0, j)),
            pl.BlockSpec((tm, tn), lambda i, j: (i, j)),
        ],
        out_specs=pl.BlockSpec((tm, tn), lambda i, j: (i, j)),
        out_shape=jax.ShapeDtypeStruct((n, dn), f32),
        compiler_params=_params("parallel", "arbitrary"),
        name="out_proj",
    )(merged, w, x)


ROUTE_TILE = 256
SLOT_CHUNK = 48
LANES = 128
ROW_TILES = D_MODEL // LANES
BF16_ROWS = 16
READ_CHUNK = SLOT_CHUNK + BF16_ROWS


def _rms(x, g):
    return x * lax.rsqrt(jnp.mean(x * x, axis=-1, keepdims=True) + NORM_EPS) * g


def _router_kernel(x_ref, g_ref, wh_ref, wl_ref, o_ref):
    xn = _rms(x_ref[...], g_ref[...])
    x_hi = xn.astype(bf16)
    x_lo = (xn - x_hi.astype(f32)).astype(bf16)
    nt = (((1,), (1,)), ((), ()))
    logits = (lax.dot_general(wh_ref[...], x_hi, nt, preferred_element_type=f32)
              + lax.dot_general(wl_ref[...], x_hi, nt, preferred_element_type=f32)
              + lax.dot_general(wh_ref[...], x_lo, nt, preferred_element_type=f32))
    m = jnp.max(logits, axis=0, keepdims=True)
    e = jnp.exp(logits - m)
    o_ref[...] = e / jnp.sum(e, axis=0, keepdims=True)


def router_probs(xm, g, wr_t, *, tm=512):
    n, d = xm.shape
    w_hi = wr_t.astype(bf16)
    w_lo = (wr_t - w_hi.astype(f32)).astype(bf16)
    wspec = pl.BlockSpec((N_EXPERTS, d), lambda i: (0, 0))
    return pl.pallas_call(
        _router_kernel,
        grid=(n // tm,),
        in_specs=[pl.BlockSpec((tm, d), lambda i: (i, 0)), pl.BlockSpec((1, d), lambda i: (0, 0)), wspec, wspec],
        out_specs=pl.BlockSpec((N_EXPERTS, tm), lambda i: (0, i)),
        out_shape=jax.ShapeDtypeStruct((N_EXPERTS, n), f32),
        compiler_params=_params("parallel"),
        name="router_probs",
    )(xm, g, w_hi, w_lo)


def _select_kernel(p_ref, pos_ref, gate_ref, offs_ref, *, cap):
    p = p_ref[...]
    R = p.shape[0]
    bits = pltpu.bitcast(p, i32)

    def body(i, prefix):
        cand = prefix | lax.shift_left(jnp.int32(1), 30 - i)
        cnt = jnp.sum((bits >= cand).astype(i32))
        return jnp.where(cnt >= cap, cand, prefix)

    thr = lax.fori_loop(0, 31, body, jnp.int32(0))
    gt = bits > thr
    eq = bits == thr
    need = cap - jnp.sum(gt.astype(i32))

    li = lax.broadcasted_iota(i32, (LANES, LANES), 0)
    lj = lax.broadcasted_iota(i32, (LANES, LANES), 1)
    upper = (li <= lj).astype(bf16)
    ri = lax.broadcasted_iota(i32, (R, R), 0)
    rj = lax.broadcasted_iota(i32, (R, R), 1)
    rows_before = (rj < ri).astype(bf16)

    def prefix_counts(mask):
        m = mask.astype(bf16)
        incl = jnp.dot(m, upper, preferred_element_type=f32)
        tot = jnp.broadcast_to(incl[:, LANES - 1:LANES], (R, LANES)).astype(bf16)
        row_off = jnp.dot(rows_before, tot, preferred_element_type=f32)
        return incl - mask.astype(f32) + row_off, row_off

    eq_excl, _ = prefix_counts(eq)
    sel = gt | (eq & (eq_excl < need.astype(f32)))
    pos, row_off = prefix_counts(sel)
    pos_ref[...] = jnp.where(sel, pos.astype(i32), -1)
    gate_ref[...] = jnp.where(sel, p, 0.0)
    offs_ref[...] = row_off[:, 0:1].astype(i32)


def select_tokens(probs3, cap):
    E, R, _ = probs3.shape
    blk = pl.BlockSpec((None, R, LANES), lambda e: (e, 0, 0))
    return pl.pallas_call(
        functools.partial(_select_kernel, cap=cap),
        grid=(E,),
        in_specs=[blk],
        out_specs=[blk, blk, pl.BlockSpec((None, R, 1), lambda e: (e, 0, 0))],
        out_shape=[jax.ShapeDtypeStruct((E, R, LANES), i32), jax.ShapeDtypeStruct((E, R, LANES), f32),
                   jax.ShapeDtypeStruct((E, R, 1), i32)],
        compiler_params=_params("parallel"),
        name="select_tokens",
    )(probs3)


def _onehot_t(pos, start, width):
    j = lax.broadcasted_iota(i32, (width, pos.shape[1]), 0)
    return j == (pos - start)


def _dispatch_kernel(pos0_ref, x_ref, g_ref, pos_ref, gate_ref, xe_ref, ge_ref, xbuf_ref, gbuf_ref, sem_ref,
                     *, cap, n_tiles):
    i = pl.program_id(0)
    E, tm = pos_ref.shape
    slot = i % 2

    xn = _rms(x_ref[...], g_ref[...]).astype(bf16)
    eye = (lax.broadcasted_iota(i32, (E, LANES), 0) == lax.broadcasted_iota(i32, (E, LANES), 1)).astype(f32)
    gt = lax.dot_general(gate_ref[...], eye, (((0,), (0,)), ((), ())), precision=lax.Precision.HIGHEST,
                         preferred_element_type=f32)
    g_hi = gt.astype(bf16)
    g_lo = (gt - g_hi.astype(f32)).astype(bf16)
    pos = pos_ref[...]

    def base(e, tile):
        return pos0_ref[e * (n_tiles + 1) + tile]

    def count(e):
        return base(e, i + 1) - base(e, i)

    def build(c, dst):
        oh = jnp.concatenate([_onehot_t(pos[e:e + 1, :], base(e, i) + c * SLOT_CHUNK, SLOT_CHUNK)
                              for e in range(E)], axis=0)
        oh = jnp.where(oh, 1.0, 0.0).astype(bf16)
        xr = jnp.dot(oh, xn, preferred_element_type=f32).astype(bf16)
        tiles = jnp.stack([xr[:, s * LANES:(s + 1) * LANES] for s in range(ROW_TILES)], axis=0)
        xbuf_ref[dst] = pltpu.einshape("srl->rsl", tiles)
        gbuf_ref[dst, :, 0, :] = (jnp.dot(oh, g_hi, preferred_element_type=f32)
                                  + jnp.dot(oh, g_lo, preferred_element_type=f32))

    def copies(e, c, src, tile):
        rows = pl.ds(e * SLOT_CHUNK, SLOT_CHUNK)
        out_rows = pl.ds(base(e, tile) + c * SLOT_CHUNK, SLOT_CHUNK)
        return (pltpu.make_async_copy(xbuf_ref.at[src, rows], xe_ref.at[e, out_rows], sem_ref.at[0, src]),
                pltpu.make_async_copy(gbuf_ref.at[src, rows], ge_ref.at[e, out_rows], sem_ref.at[1, src]))

    @pl.when(i == 0)
    def _():
        xbuf_ref[2] = jnp.zeros(xbuf_ref.shape[1:], bf16)
        gbuf_ref[2] = jnp.zeros(gbuf_ref.shape[1:], f32)
        for e in range(E):
            rows = pl.ds(e * SLOT_CHUNK, SLOT_CHUNK)
            pad = pl.ds(cap, SLOT_CHUNK)
            for cp in (pltpu.make_async_copy(xbuf_ref.at[2, rows], xe_ref.at[e, pad], sem_ref.at[0, 2]),
                       pltpu.make_async_copy(gbuf_ref.at[2, rows], ge_ref.at[e, pad], sem_ref.at[1, 2])):
                cp.start()
                cp.wait()

    build(0, slot)

    @pl.when(i > 0)
    def _():
        for e in range(E):
            for cp in copies(e, 0, 1 - slot, i - 1):
                cp.wait()

    for e in range(E):
        for cp in copies(e, 0, slot, i):
            cp.start()

    nch = jnp.int32(0)
    for e in range(E):
        nch = jnp.maximum(nch, (count(e) + SLOT_CHUNK - 1) // SLOT_CHUNK)

    def overflow(c, carry):
        build(c, 2)
        for e in range(E):
            @pl.when(count(e) > c * SLOT_CHUNK)
            def _():
                for cp in copies(e, c, 2, i):
                    cp.start()
                    cp.wait()
        return carry

    lax.fori_loop(1, nch, overflow, 0)

    @pl.when(i == n_tiles - 1)
    def _():
        for e in range(E):
            for cp in copies(e, 0, slot, i):
                cp.wait()


def dispatch(xm, g, pos, gates, pos0, cap):
    n, d = xm.shape
    E = pos.shape[0]
    n_tiles = n // ROUTE_TILE
    rows = cap + SLOT_CHUNK
    grid_spec = pltpu.PrefetchScalarGridSpec(
        num_scalar_prefetch=1,
        grid=(n_tiles,),
        in_specs=[
            pl.BlockSpec((ROUTE_TILE, d), lambda i, p0: (i, 0)),
            pl.BlockSpec((1, d), lambda i, p0: (0, 0)),
            pl.BlockSpec((E, ROUTE_TILE), lambda i, p0: (0, i)),
            pl.BlockSpec((E, ROUTE_TILE), lambda i, p0: (0, i)),
        ],
        out_specs=[pl.BlockSpec(memory_space=pl.ANY), pl.BlockSpec(memory_space=pl.ANY)],
        scratch_shapes=[pltpu.VMEM((3, E * SLOT_CHUNK, ROW_TILES, LANES), bf16),
                        pltpu.VMEM((3, E * SLOT_CHUNK, 1, LANES), f32),
                        pltpu.SemaphoreType.DMA((2, 3))],
    )
    return pl.pallas_call(
        functools.partial(_dispatch_kernel, cap=cap, n_tiles=n_tiles),
        grid_spec=grid_spec,
        out_shape=[jax.ShapeDtypeStruct((E, rows, ROW_TILES, LANES), bf16),
                   jax.ShapeDtypeStruct((E, rows, 1, LANES), f32)],
        compiler_params=_params("arbitrary"),
        name="dispatch",
    )(pos0, xm, g, pos, gates)


def _expert_up_kernel(xe_ref, wg_ref, wu_ref, h_ref, xb_ref):
    @pl.when(pl.program_id(2) == 0)
    def _():
        tiles = pltpu.einshape("rsl->srl", xe_ref[...])
        for s in range(ROW_TILES):
            xb_ref[:, s * LANES:(s + 1) * LANES] = tiles[s]

    xb = xb_ref[...]
    hg = jnp.dot(xb, wg_ref[...], preferred_element_type=f32)
    hu = jnp.dot(xb, wu_ref[...], preferred_element_type=f32)
    h_ref[...] = (hg * jax.nn.sigmoid(hg) * hu).astype(h_ref.dtype)


def _expert_down_kernel(h_ref, ge_ref, wd_ref, o_ref):
    gl = ge_ref[:, 0, :]
    lane = lax.broadcasted_iota(i32, gl.shape, 1)
    gate = jnp.sum(jnp.where(lane == pl.program_id(0), gl, 0.0), axis=-1, keepdims=True)
    o_ref[...] = (jnp.dot(h_ref[...], wd_ref[...], preferred_element_type=f32) * gate).astype(o_ref.dtype)


def expert_ffn(xe, ge, wg, wu, wd, cap, *, tm=1024, tf=1024, tn=2048):
    E = xe.shape[0]
    d = D_MODEL
    dff = wg.shape[2]
    tm = min(tm, cap)
    h = pl.pallas_call(
        _expert_up_kernel,
        grid=(E, cap // tm, dff // tf),
        in_specs=[
            pl.BlockSpec((None, tm, ROW_TILES, LANES), lambda e_, i, f: (e_, i, 0, 0)),
            pl.BlockSpec((None, d, tf), lambda e_, i, f: (e_, 0, f)),
            pl.BlockSpec((None, d, tf), lambda e_, i, f: (e_, 0, f)),
        ],
        out_specs=pl.BlockSpec((None, tm, tf), lambda e_, i, f: (e_, i, f)),
        out_shape=jax.ShapeDtypeStruct((E, cap, dff), bf16),
        scratch_shapes=[pltpu.VMEM((tm, d), bf16)],
        compiler_params=_params("parallel", "parallel", "arbitrary"),
        name="expert_up",
    )(xe, wg, wu)
    return pl.pallas_call(
        _expert_down_kernel,
        grid=(E, cap // tm, d // tn),
        in_specs=[
            pl.BlockSpec((None, tm, dff), lambda e_, i, j: (e_, i, 0)),
            pl.BlockSpec((None, tm, 1, LANES), lambda e_, i, j: (e_, i, 0, 0)),
            pl.BlockSpec((None, dff, tn), lambda e_, i, j: (e_, 0, j)),
        ],
        out_specs=pl.BlockSpec((None, tm, tn), lambda e_, i, j: (e_, i, j)),
        out_shape=jax.ShapeDtypeStruct((E, cap, d), bf16),
        compiler_params=_params("parallel", "parallel", "arbitrary"),
        name="expert_down",
    )(h, ge, wd)


def _combine_kernel(pos0_ref, x_ref, g_ref, pos_ref, ye_ref, o_ref, buf_ref, sem_ref, *, cap, n_tiles):
    i = pl.program_id(0)
    E, tm = pos_ref.shape
    slot = i % 2

    def base(e, tile):
        return pos0_ref[e * (n_tiles + 1) + tile]

    def count(e):
        return base(e, i + 1) - base(e, i)

    def start_row(e, c, tile):
        st = base(e, tile) + c * SLOT_CHUNK
        st = jnp.minimum((st // BF16_ROWS) * BF16_ROWS, cap - READ_CHUNK)
        return pl.multiple_of(st, BF16_ROWS)

    def copy(e, c, dst, tile):
        return pltpu.make_async_copy(ye_ref.at[e, pl.ds(start_row(e, c, tile), READ_CHUNK)],
                                     buf_ref.at[dst, pl.ds(e * READ_CHUNK, READ_CHUNK)], sem_ref.at[dst])

    @pl.when(i == 0)
    def _():
        for e in range(E):
            copy(e, 0, slot, i).start()

    @pl.when(i + 1 < n_tiles)
    def _():
        for e in range(E):
            copy(e, 0, 1 - slot, i + 1).start()

    pos = pos_ref[...]

    def onehot(c):
        parts = []
        for e in range(E):
            pe = pos[e:e + 1, :]
            local = pe - base(e, i)
            keep = (local >= c * SLOT_CHUNK) & (local < (c + 1) * SLOT_CHUNK) & (pe >= 0)
            oh = _onehot_t(pe, start_row(e, c, i), READ_CHUNK) & keep
            parts.append(jnp.where(oh, 1.0, 0.0).astype(bf16))
        return jnp.concatenate(parts, axis=0)

    for e in range(E):
        copy(e, 0, slot, i).wait()
    acc = lax.dot_general(onehot(0), buf_ref[slot], (((0,), (0,)), ((), ())), preferred_element_type=f32)

    nch = jnp.int32(0)
    for e in range(E):
        nch = jnp.maximum(nch, (count(e) + SLOT_CHUNK - 1) // SLOT_CHUNK)

    def overflow(c, acc):
        for e in range(E):
            cp = copy(e, c, 2, i)
            cp.start()
            cp.wait()
        return acc + lax.dot_general(onehot(c), buf_ref[2], (((0,), (0,)), ((), ())), preferred_element_type=f32)

    acc = lax.fori_loop(1, nch, overflow, acc)
    o_ref[...] = _rms(x_ref[...] + acc, g_ref[...])


def combine(xm, g_final, pos, ye, pos0, cap):
    n, d = xm.shape
    E = pos.shape[0]
    n_tiles = n // ROUTE_TILE
    grid_spec = pltpu.PrefetchScalarGridSpec(
        num_scalar_prefetch=1,
        grid=(n_tiles,),
        in_specs=[
            pl.BlockSpec((ROUTE_TILE, d), lambda i, p0: (i, 0)),
            pl.BlockSpec((1, d), lambda i, p0: (0, 0)),
            pl.BlockSpec((E, ROUTE_TILE), lambda i, p0: (0, i)),
            pl.BlockSpec(memory_space=pl.ANY),
        ],
        out_specs=pl.BlockSpec((ROUTE_TILE, d), lambda i, p0: (i, 0)),
        scratch_shapes=[pltpu.VMEM((3, E * READ_CHUNK, d), bf16), pltpu.SemaphoreType.DMA((3,))],
    )
    return pl.pallas_call(
        functools.partial(_combine_kernel, cap=cap, n_tiles=n_tiles),
        grid_spec=grid_spec,
        out_shape=jax.ShapeDtypeStruct((n, d), f32),
        compiler_params=_params("arbitrary"),
        name="combine",
    )(pos0, xm, g_final, pos, ye)


def moe_block(xm, g_moe, wr_t, wg, wu, wd, g_final):
    n = xm.shape[0]
    cap = EC_CAPACITY * n // N_EXPERTS
    n_tiles = n // ROUTE_TILE
    probs = router_probs(xm, g_moe, wr_t)
    pos3, gate3, offs = select_tokens(probs.reshape(N_EXPERTS, n // LANES, LANES), cap)
    pos = pos3.reshape(N_EXPERTS, n)
    gates = gate3.reshape(N_EXPERTS, n)
    tile_off = offs[:, ::ROUTE_TILE // LANES, 0]
    pos0 = jnp.concatenate([tile_off, jnp.full((N_EXPERTS, 1), cap, i32)], axis=1).reshape(-1)
    xe, ge = dispatch(xm, g_moe, pos, gates, pos0, cap)
    ye = expert_ffn(xe, ge, wg, wu, wd, cap)
    return combine(xm, g_final, pos, ye, pos0, cap)


N2 = 128


def _stage1_tables(H):
    n_total = 2 * N2 * H
    k1 = jnp.arange(H, dtype=jnp.int32)
    idx = jnp.arange(max(H, N2), dtype=jnp.int32)

    def phase(m):
        a = (m % (2 * n_total)).astype(f32) * (math.pi / n_total)
        return jnp.cos(a), jnp.sin(a)

    c1, s1 = phase((2 * k1 + 1)[:, None] * (N2 * idx[:H])[None, :])
    c2, s2 = phase((2 * k1 + 1)[None, :] * idx[:N2, None])
    c = c2[:, :, None] * c1[None] - s2[:, :, None] * s1[None]
    s_ = s2[:, :, None] * c1[None] + c2[:, :, None] * s1[None]
    fwd = jnp.concatenate([c, -s_], axis=1).astype(bf16)
    scale = 1.0 / (N2 * H)
    ct, st = jnp.swapaxes(c, 1, 2), jnp.swapaxes(s_, 1, 2)
    inv = (jnp.concatenate([ct, -st], axis=2) * scale).astype(bf16)
    return fwd, inv


def _stage2_tables():
    j = jnp.arange(N2, dtype=jnp.int32)
    th = ((j[:, None] * j[None, :]) % N2).astype(f32) * (2.0 * math.pi / N2)
    c, s = jnp.cos(th), jnp.sin(th)
    fwd = jnp.concatenate([jnp.concatenate([c, s], 1), jnp.concatenate([-s, c], 1)], 0)
    inv = jnp.concatenate([jnp.concatenate([c, -s], 1), jnp.concatenate([s, c], 1)], 0)
    return fwd.astype(bf16), inv.astype(bf16)


def _shortconv_kernel(x_ref, w_ref, b_ref, o_ref, s_ref, *, rows):
    L = x_ref.shape[0]
    zeros8 = jnp.zeros((8, x_ref.shape[1]), f32)
    s_ref[0:8, :] = zeros8
    s_ref[L + 8:L + 16, :] = zeros8

    def load(i, c):
        r0 = pl.multiple_of(i * rows, rows)
        s_ref[pl.ds(r0 + 8, rows), :] = x_ref[pl.ds(r0, rows), :].astype(f32)
        return c

    lax.fori_loop(0, L // rows, load, 0)
    w0, w1, w2, b = w_ref[0:1, :], w_ref[1:2, :], w_ref[2:3, :], b_ref[...]

    def conv(i, c):
        r0 = pl.multiple_of(i * rows, rows)
        prev = s_ref[pl.ds(r0 + 7, rows), :]
        cur = s_ref[pl.ds(r0 + 8, rows), :]
        nxt = s_ref[pl.ds(r0 + 9, rows), :]
        o_ref[pl.ds(r0, rows), :] = (prev * w0 + cur * w1 + nxt * w2 + b).astype(o_ref.dtype)
        return c

    lax.fori_loop(0, L // rows, conv, 0)


def short_conv(p3, w, b, col0, ncols, out_dtype, *, rows=256, cw=128):
    B, L, _ = p3.shape
    cb = col0 // cw
    return pl.pallas_call(
        functools.partial(_shortconv_kernel, rows=rows),
        grid=(B, ncols // cw),
        in_specs=[
            pl.BlockSpec((None, L, cw), lambda b_, j: (b_, 0, cb + j)),
            pl.BlockSpec((3, cw), lambda b_, j: (0, cb + j)),
            pl.BlockSpec((1, cw), lambda b_, j: (0, cb + j)),
        ],
        out_specs=pl.BlockSpec((None, L, cw), lambda b_, j: (b_, 0, j)),
        out_shape=jax.ShapeDtypeStruct((B, L, ncols), out_dtype),
        scratch_shapes=[pltpu.VMEM((L + 16, cw), f32)],
        compiler_params=_params("parallel", "parallel"),
        name="short_conv",
    )(p3, w, b)


def _dft1_kernel(tab_ref, z_ref, o_ref, *, n2b, K):
    g = pl.program_id(2)
    rows = [z_ref[pl.ds(pl.multiple_of(n1 * N2 + g * n2b, n2b), n2b), :] for n1 in range(K)]
    zs = pltpu.einshape("kqc->qkc", jnp.stack(rows, axis=0)).astype(bf16)
    for q in range(n2b):
        o_ref[q] = jnp.dot(tab_ref[g * n2b + q], zs[q], preferred_element_type=f32).astype(o_ref.dtype)


def dft_stage1(z, tab, *, n2b=16, cb=256):
    B, Lz, C = z.shape
    _, M, K = tab.shape
    assert Lz == K * N2
    return pl.pallas_call(
        functools.partial(_dft1_kernel, n2b=n2b, K=K),
        grid=(B, C // cb, N2 // n2b),
        in_specs=[
            pl.BlockSpec((N2, M, K), lambda b_, c, g: (0, 0, 0)),
            pl.BlockSpec((None, Lz, cb), lambda b_, c, g: (b_, 0, c)),
        ],
        out_specs=pl.BlockSpec((None, n2b, M, cb), lambda b_, c, g: (b_, g, 0, c)),
        out_shape=jax.ShapeDtypeStruct((B, N2, M, C), bf16),
        compiler_params=_params("parallel", "parallel", "arbitrary"),
        name="dft_stage1",
    )(tab, z)


def _mid_kernel(m2_ref, m2i_ref, a_ref, kf_ref, o_ref, *, k1b):
    are = pltpu.einshape("nkc->knc", a_ref[:, 0])
    aim = pltpu.einshape("nkc->knc", a_ref[:, 1])
    bre, bim = [], []
    for q in range(k1b):
        x = jnp.dot(m2_ref[...], jnp.concatenate([are[q], aim[q]], axis=0), preferred_element_type=f32)
        xr, xi = x[:N2], x[N2:]
        kf = kf_ref[q].astype(f32)
        kr, ki = kf[:N2], kf[N2:]
        y = jnp.concatenate([xr * kr - xi * ki, xr * ki + xi * kr], axis=0).astype(bf16)
        bq = jnp.dot(m2i_ref[...], y, preferred_element_type=f32).astype(bf16)
        bre.append(bq[:N2])
        bim.append(bq[N2:])
    o_ref[:, 0] = pltpu.einshape("knc->nkc", jnp.stack(bre, axis=0))
    o_ref[:, 1] = pltpu.einshape("knc->nkc", jnp.stack(bim, axis=0))


def spectral_mid(a, kf, m2, m2i, order, *, k1b=16, cb=256):
    B, _, _, H, C = a.shape
    k1b = min(k1b, H)
    cpo = C // cb
    return pl.pallas_call(
        functools.partial(_mid_kernel, k1b=k1b),
        grid=(B, cpo, H // k1b),
        in_specs=[
            pl.BlockSpec((2 * N2, 2 * N2), lambda b_, c, k: (0, 0)),
            pl.BlockSpec((2 * N2, 2 * N2), lambda b_, c, k: (0, 0)),
            pl.BlockSpec((None, N2, 2, k1b, cb), lambda b_, c, k: (b_, 0, 0, k, c)),
            pl.BlockSpec((k1b, 2 * N2, cb), lambda b_, c, k: (k, 0, order * cpo + c)),
        ],
        out_specs=pl.BlockSpec((None, N2, 2, k1b, cb), lambda b_, c, k: (b_, 0, 0, k, c)),
        out_shape=jax.ShapeDtypeStruct(a.shape, bf16),
        compiler_params=_params("parallel", "parallel", "arbitrary"),
        name="spectral_mid",
    )(m2, m2i, a, kf)


def _idft1_gate_kernel(tab_ref, b_ref, x_ref, v_ref, d_ref, o_ref, y_ref, *, n2b, H, rows):
    g = pl.program_id(2)
    ys = [jnp.dot(tab_ref[g * n2b + q], b_ref[q], preferred_element_type=f32) for q in range(n2b)]
    yt = pltpu.einshape("qhc->hqc", jnp.stack(ys, axis=0))
    for n1 in range(H):
        y_ref[pl.ds(pl.multiple_of(n1 * N2 + g * n2b, n2b), n2b), :] = yt[n1]

    @pl.when(g == pl.num_programs(2) - 1)
    def _():
        d = d_ref[...]

        def gate(i, c):
            r = pl.ds(pl.multiple_of(i * rows, rows), rows)
            o_ref[r, :] = (x_ref[r, :].astype(f32) * (y_ref[r, :] + v_ref[r, :].astype(f32) * d)).astype(o_ref.dtype)
            return c

        lax.fori_loop(0, (H * N2) // rows, gate, 0)


def idft1_gate(bm, tab, xg, xcol0, v, d, *, n2b=16, rows=256):
    B, _, M, C = bm.shape
    H = M // 2
    L = H * N2
    xb = xcol0 // 128
    return pl.pallas_call(
        functools.partial(_idft1_gate_kernel, n2b=n2b, H=H, rows=rows),
        grid=(B, C // 128, N2 // n2b),
        in_specs=[
            pl.BlockSpec((N2, H, M), lambda b_, c, g: (0, 0, 0)),
            pl.BlockSpec((None, n2b, M, 128), lambda b_, c, g: (b_, g, 0, c)),
            pl.BlockSpec((None, L, 128), lambda b_, c, g: (b_, 0, xb + c)),
            pl.BlockSpec((None, L, 128), lambda b_, c, g: (b_, 0, c)),
            pl.BlockSpec((1, 128), lambda b_, c, g: (0, c)),
        ],
        out_specs=pl.BlockSpec((None, L, 128), lambda b_, c, g: (b_, 0, c)),
        out_shape=jax.ShapeDtypeStruct((B, L, C), bf16),
        scratch_shapes=[pltpu.VMEM((L, 128), f32)],
        compiler_params=_params("parallel", "parallel", "arbitrary"),
        name="idft1_gate",
    )(tab, bm, xg, v, d)


def _filter_mlp_kernel(feat_ref, w1_ref, b1_ref, w2_ref, b2_ref, w3_ref, b3_ref, w4a_ref, w4b_ref, fr_ref, dl_ref,
                       pq_ref, nrm_ref):
    hp = lax.Precision.HIGHEST
    i = pl.program_id(0)
    feat = feat_ref[...]
    fr = fr_ref[...]
    h = jnp.sin(fr * (jnp.dot(feat, w1_ref[...], precision=hp, preferred_element_type=f32) + b1_ref[...]))
    h = jnp.sin(fr * (jnp.dot(h, w2_ref[...], precision=hp, preferred_element_type=f32) + b2_ref[...]))
    h = jnp.sin(fr * (jnp.dot(h, w3_ref[...], precision=hp, preferred_element_type=f32) + b3_ref[...]))
    h_hi = h.astype(bf16)
    h_lo = (h - h_hi.astype(f32)).astype(bf16)
    h_cat = jnp.concatenate([h_hi, h_lo], axis=1)

    def proj(cols):
        return (jnp.dot(h_cat, w4a_ref[:, cols], preferred_element_type=f32)
                + jnp.dot(h_hi, w4b_ref[:, cols], preferred_element_type=f32))

    win = jnp.exp(-feat[:, 0:1] * dl_ref[...])
    row = lax.broadcasted_iota(jnp.int32, win.shape, 0) + i * feat.shape[0]
    C = D_HYENA
    nrm = []
    for o in range(HYENA_ORDER):
        c0 = o * N_DIR * C
        fwd = proj(slice(c0, c0 + C)) * win
        bwd = proj(slice(c0 + C, c0 + 2 * C)) * win
        bwd = jnp.where(row == 0, 0.0, bwd)
        pq_ref[0, :, o * C:(o + 1) * C] = (fwd + bwd).astype(pq_ref.dtype)
        pq_ref[1, :, o * C:(o + 1) * C] = (fwd - bwd).astype(pq_ref.dtype)
        nrm.append(jnp.sum(jnp.abs(fwd) + jnp.abs(bwd), axis=0, keepdims=True))
    nrm = jnp.concatenate(nrm, axis=1)

    @pl.when(i == 0)
    def _():
        nrm_ref[...] = nrm

    @pl.when(i != 0)
    def _():
        nrm_ref[...] += nrm


def filter_mlp(feat, w1, b1, w2, b2, w3, b3, w4a, w4b, freq, deltas, *, tm=512):
    L = feat.shape[0]
    CO = HYENA_ORDER * D_HYENA
    full = lambda a: pl.BlockSpec(a.shape, lambda i: (0,) * a.ndim)
    args = (w1, b1, w2, b2, w3, b3, w4a, w4b, freq, deltas)
    return pl.pallas_call(
        _filter_mlp_kernel,
        grid=(L // tm,),
        in_specs=[pl.BlockSpec((tm, 128), lambda i: (i, 0))] + [full(a) for a in args],
        out_specs=[pl.BlockSpec((2, tm, CO), lambda i: (0, i, 0)), pl.BlockSpec((1, CO), lambda i: (0, 0))],
        out_shape=[jax.ShapeDtypeStruct((2, L, CO), bf16), jax.ShapeDtypeStruct((1, CO), f32)],
        compiler_params=_params("arbitrary"),
        name="filter_mlp",
    )(feat, *args)


def _filter_stage2_kernel(m2_ref, ap_ref, aq_ref, nrm_ref, o_ref, *, k1b):
    inv = 1.0 / nrm_ref[...]
    pr, pi = pltpu.einshape("nkc->knc", ap_ref[:, 0]), pltpu.einshape("nkc->knc", ap_ref[:, 1])
    qr, qi = pltpu.einshape("nkc->knc", aq_ref[:, 0]), pltpu.einshape("nkc->knc", aq_ref[:, 1])
    for q in range(k1b):
        ap = jnp.concatenate([pr[q], pi[q]], axis=0)
        aq = jnp.concatenate([qr[q], qi[q]], axis=0)
        re = jnp.dot(m2_ref[0:N2, :], ap, preferred_element_type=f32) * inv
        im = jnp.dot(m2_ref[N2:2 * N2, :], aq, preferred_element_type=f32) * inv
        o_ref[q] = jnp.concatenate([re, im], axis=0).astype(o_ref.dtype)


def filter_stage2(apq, m2, nrm, *, k1b=16, cb=256):
    _, _, _, H, CO = apq.shape
    k1b = min(k1b, H)
    return pl.pallas_call(
        functools.partial(_filter_stage2_kernel, k1b=k1b),
        grid=(CO // cb, H // k1b),
        in_specs=[
            pl.BlockSpec((2 * N2, 2 * N2), lambda c, k: (0, 0)),
            pl.BlockSpec((None, N2, 2, k1b, cb), lambda c, k: (0, 0, 0, k, c)),
            pl.BlockSpec((None, N2, 2, k1b, cb), lambda c, k: (1, 0, 0, k, c)),
            pl.BlockSpec((1, cb), lambda c, k: (0, c)),
        ],
        out_specs=pl.BlockSpec((k1b, 2 * N2, cb), lambda c, k: (k, 0, c)),
        out_shape=jax.ShapeDtypeStruct((H, 2 * N2, CO), bf16),
        compiler_params=_params("parallel", "arbitrary"),
        name="filter_stage2",
    )(m2, apq, apq, nrm)


def _pad2(a, rows, cols):
    return jnp.pad(a, ((0, rows - a.shape[0]), (0, cols - a.shape[1])))


def hyena_filter(L, w1, b1, w2, b2, w3, b3, w4, freq, tab_f, m2):
    pos = jnp.arange(L, dtype=f32)[:, None]
    t = jnp.linspace(0.0, 1.0, L, dtype=f32)[:, None]
    bands = (FILTER_EMB - 1) // 2
    z = jnp.linspace(1e-4, bands - 1, bands, dtype=f32)[None, :] * ((2.0 * math.pi / L) * pos)
    feat = _pad2(jnp.concatenate([t, jnp.cos(z), -jnp.sin(z)], axis=-1), L, 128)
    deltas = jnp.abs(jnp.linspace(math.log(DECAY_TARGET) / DECAY_SLOW_PCT,
                                  math.log(DECAY_TARGET) / DECAY_FAST_PCT, D_HYENA, dtype=f32))[None, :]
    row = lambda v: _pad2(v[None, :], 1, 128)
    w4p = _pad2(w4, 128, w4.shape[1])
    w4_hi = w4p.astype(bf16)
    w4_lo = (w4p - w4_hi.astype(f32)).astype(bf16)
    pq, nrm = filter_mlp(feat, _pad2(w1, 128, 128), row(b1), _pad2(w2, 128, 128), row(b2), _pad2(w3, 128, 128),
                         row(b3), jnp.concatenate([w4_hi, w4_hi], axis=0), w4_lo, row(freq), deltas)
    apq = dft_stage1(pq, tab_f)
    H = L // N2
    return filter_stage2(apq.reshape(2, N2, 2, H, HYENA_ORDER * D_HYENA), m2, nrm)


def hyena_mixer(p3, conv_w, conv_b, hy_d, kf, tabs):
    tab_f, tab_i, m2, m2i = tabs
    B, L, _ = p3.shape
    C = D_HYENA
    H = L // N2
    cb = conv_b[None, :]
    xg = short_conv(p3, conv_w, cb, 0, 2 * C, bf16)
    z = short_conv(p3, conv_w, cb, 2 * C, C, bf16)
    for o in range(HYENA_ORDER):
        a = dft_stage1(z, tab_f).reshape(B, N2, 2, H, C)
        bm = spectral_mid(a, kf, m2, m2i, o).reshape(B, N2, 2 * H, C)
        z = idft1_gate(bm, tab_i, xg, o * C, z, hy_d[o:o + 1])
    return z


def _gelu(x):
    return 0.5 * x * (1.0 + lax.erf(x * (1.0 / math.sqrt(2.0))))


def _sgu_kernel(pu_ref, pv_ref, g_ref, b_ref, ws_ref, bs_ref, o_ref):
    v = _gelu(pv_ref[...].astype(f32))
    mu = jnp.mean(v, axis=-1, keepdims=True)
    vc = v - mu
    var = jnp.mean(vc * vc, axis=-1, keepdims=True)
    vn = (vc * lax.rsqrt(var + LN_EPS) * g_ref[...] + b_ref[...]).astype(bf16)
    for c in range(pv_ref.shape[0] // CHUNK):
        rows = slice(c * CHUNK, (c + 1) * CHUNK)
        for g in range(SGU_GROUPS):
            cols = slice(g * SGU_GROUP_DIM, (g + 1) * SGU_GROUP_DIM)
            s = jnp.dot(ws_ref[g], vn[rows, cols], preferred_element_type=f32) + bs_ref[:, g:g + 1]
            u = _gelu(pu_ref[rows, cols].astype(f32))
            o_ref[rows, cols] = (u * s).astype(o_ref.dtype)


def sgu(p, ln_g, ln_b, w_s, b_s_t, *, tm=1024):
    n = p.shape[0]
    ub = OFF_SG // D_SGU
    return pl.pallas_call(
        _sgu_kernel,
        grid=(n // tm,),
        in_specs=[
            pl.BlockSpec((tm, D_SGU), lambda i: (i, ub)),
            pl.BlockSpec((tm, D_SGU), lambda i: (i, ub + 1)),
            pl.BlockSpec((1, D_SGU), lambda i: (0, 0)),
            pl.BlockSpec((1, D_SGU), lambda i: (0, 0)),
            pl.BlockSpec((SGU_GROUPS, CHUNK, CHUNK), lambda i: (0, 0, 0)),
            pl.BlockSpec((CHUNK, SGU_GROUPS), lambda i: (0, 0)),
        ],
        out_specs=pl.BlockSpec((tm, D_SGU), lambda i: (i, 0)),
        out_shape=jax.ShapeDtypeStruct((n, D_SGU), bf16),
        compiler_params=_params("parallel"),
        name="sgu",
    )(p, p, ln_g, ln_b, w_s, b_s_t)


def kernel(x_prompt, x_sample, norm_mix_g, w_in, hy_conv_w, hy_conv_b, flt_w1, flt_b1, flt_w2, flt_b2,
           flt_w3, flt_b3, flt_w4, flt_sin_freq, hy_d, sg_ln_g, sg_ln_b, sg_w, sg_b, w_branch_hy,
           w_branch_sg, w_out, norm_moe_g, w_router, w_gate, w_up, w_down, norm_final_g):
    w_in_b = w_in[0].astype(bf16)
    w_bh_b = w_branch_hy[0].astype(bf16)
    w_bs_b = w_branch_sg[0].astype(bf16)
    w_out_b = w_out[0].astype(bf16)
    wg_b = w_gate[0].astype(bf16)
    wu_b = w_up[0].astype(bf16)
    wd_b = w_down[0].astype(bf16)
    g_mix = norm_mix_g[0].reshape(1, D_MODEL)
    sg_w_b = sg_w[0].astype(bf16)
    sg_b_t = sg_b[0].T

    m2, m2i = _stage2_tables()

    def mixer(x):
        B, L, D = x.shape
        n = B * L
        xt = x.reshape(n, D)
        p = in_proj(xt, g_mix, w_in_b)
        p3 = p.reshape(B, L, D_IN)
        tab_f, tab_i = _stage1_tables(L // N2)
        kf = hyena_filter(L, flt_w1[0], flt_b1[0], flt_w2[0], flt_b2[0], flt_w3[0], flt_b3[0],
                          flt_w4[0], flt_sin_freq[0], tab_f, m2)
        y_hy = hyena_mixer(p3, hy_conv_w[0], hy_conv_b[0], hy_d[0], kf,
                           (tab_f, tab_i, m2, m2i)).reshape(n, D_HYENA)
        y_sg = sgu(p, sg_ln_g, sg_ln_b, sg_w_b, sg_b_t)
        merged = branch_merge(y_hy, y_sg, w_bh_b, w_bs_b, p)
        return out_proj(merged, w_out_b, xt)

    g_moe = norm_moe_g[0].reshape(1, D_MODEL)
    g_fin = norm_final_g.reshape(1, D_MODEL)
    wr_t = w_router[0].T
    outs = []
    for x in (x_prompt, x_sample):
        y = moe_block(mixer(x), g_moe, wr_t, wg_b, wu_b, wd_b, g_fin)
        outs.append(y.reshape(x.shape))
    return tuple(outs)
```
